```python
import jax, jax.numpy as jnp
from jax import lax
import numpy as np

D_MODEL = 2048
BATCH = 4
SEQ = 4096
DEPTH = 1

CHUNK = 64
PLE_DIM = 256
EPS = 1e-6

GDN_HEADS = 8
GDN_DK = 128
GDN_DV = 128
GDN_CONV = 4
MLSTM_HEADS = 4
MLSTM_DQK = 128
MLSTM_DV = 256
GATE_SOFTCAP = 15.0
N_EXPERTS = 32
TOP_K = 4
D_EXPERT = 2048
SWIGLU_LIMIT = 7.0
SWIGLU_ALPHA = 1.702
MOE_BLOCK = 256

GDN_QK = GDN_HEADS * GDN_DK
GDN_V = GDN_HEADS * GDN_DV
ML_QK = MLSTM_HEADS * MLSTM_DQK
ML_V = MLSTM_HEADS * MLSTM_DV
MIX_WIDTH = GDN_V + ML_V
GDN_CONV_CH = 2 * GDN_QK + GDN_V
IN_SPLITS = (GDN_QK, GDN_QK, GDN_V, GDN_HEADS, GDN_HEADS, GDN_V,
             ML_QK, ML_QK, ML_V, MLSTM_HEADS, MLSTM_HEADS, ML_V)
D_IN = 2 * GDN_QK + 2 * GDN_V + 2 * GDN_HEADS + 2 * ML_QK + 2 * ML_V + 2 * MLSTM_HEADS

kernel_name = "hybrid_gdn_mlstm_moe_ple_block"


def rmsnorm(x, g):
    xf = x.astype(jnp.float32)
    y = xf * lax.rsqrt(jnp.mean(xf * xf, axis=-1, keepdims=True) + EPS)
    return (y * g.astype(jnp.float32)).astype(x.dtype)


def l2norm(x):
    return x * lax.rsqrt(jnp.sum(x * x, axis=-1, keepdims=True) + EPS)


def softcap(x, cap):
    return cap * jnp.tanh(x / cap)


def causal_dwconv(x, w):
    k = w.shape[0]
    return lax.conv_general_dilated(
        x, w[:, None, :].astype(x.dtype), window_strides=(1,), padding=[(k - 1, 0)],
        dimension_numbers=("NWC", "WIO", "NWC"), feature_group_count=x.shape[-1])


def to_chunks(t):
    b, s, h = t.shape[:3]
    t = t.reshape((b, s // CHUNK, CHUNK, h) + t.shape[3:])
    return t.transpose((1, 0, 3, 2) + tuple(range(4, t.ndim)))


def from_chunks(t):
    n, b, h, l, d = t.shape
    return t.transpose(1, 0, 3, 2, 4).reshape(b, n * l, h, d)


def gated_delta_chunked(q, k, v, g, beta):
    b_, s_, h_, dk = q.shape
    dv = v.shape[-1]
    qc, kc, vc = to_chunks(q), to_chunks(k), to_chunks(v)
    gc = jnp.cumsum(to_chunks(g), axis=-1)
    bc = to_chunks(beta)
    tril = jnp.tril(jnp.ones((CHUNK, CHUNK), bool))
    strict = jnp.tril(jnp.ones((CHUNK, CHUNK), bool), -1)
    decay = jnp.exp(jnp.where(tril, gc[..., :, None] - gc[..., None, :], -jnp.inf))
    kb = kc * bc[..., None]
    vb = vc * bc[..., None]
    m_low = jnp.where(strict, jnp.einsum("nbhld,nbhmd->nbhlm", kb, kc) * decay, 0.0)
    eye = jnp.eye(CHUNK, dtype=jnp.float32)
    a_mat = eye + m_low
    t_inv = lax.linalg.triangular_solve(a_mat, jnp.broadcast_to(eye, a_mat.shape),
                                        left_side=True, lower=True, unit_diagonal=True)
    u = jnp.einsum("nbhlm,nbhme->nbhle", t_inv, vb)
    w = jnp.einsum("nbhlm,nbhmd->nbhld", t_inv, kb * jnp.exp(gc)[..., None])
    attn = jnp.where(tril, jnp.einsum("nbhld,nbhmd->nbhlm", qc, kc) * decay, 0.0)
    q_dec = qc * jnp.exp(gc)[..., None]
    k_dec = kc * jnp.exp(gc[..., -1:] - gc)[..., None]
    g_last = jnp.exp(gc[..., -1])

    def step(state, xs):
        qd, kd, uc, wc, at, gl = xs
        v_new = uc - jnp.einsum("bhld,bhde->bhle", wc, state)
        o = jnp.einsum("bhld,bhde->bhle", qd, state) + jnp.einsum("bhlm,bhme->bhle", at, v_new)
        state = state * gl[..., None, None] + jnp.einsum("bhld,bhle->bhde", kd, v_new)
        return state, o

    s0 = jnp.zeros((b_, h_, dk, dv), jnp.float32)
    _, o = lax.scan(step, s0, (q_dec, k_dec, u, w, attn, g_last))
    return from_chunks(o)


def mlstm_chunked(q, k, v, i_pre, log_f):
    b_, s_, h_, dqk = q.shape
    dv = v.shape[-1]
    qc, kc, vc = to_chunks(q), to_chunks(k), to_chunks(v)
    fc = jnp.cumsum(to_chunks(log_f), axis=-1)
    ic = to_chunks(i_pre)
    tril = jnp.tril(jnp.ones((CHUNK, CHUNK), bool))
    d_mat = jnp.where(tril, fc[..., :, None] - fc[..., None, :] + ic[..., None, :], -jnp.inf)
    d_max = jnp.max(d_mat, axis=-1)
    qk = jnp.einsum("nbhld,nbhmd->nbhlm", qc, kc)
    a_end = fc[..., -1:] - fc + ic
    a_max = jnp.max(a_end, axis=-1)

    def step(carry, xs):
        c_st, n_st, m_st = carry
        q_c, k_c, v_c, f_c, dm, dmx, qk_c, ae, am = xs
        inter = f_c + m_st[..., None]
        m_i = jnp.maximum(dmx, inter)
        wts = jnp.exp(dm - m_i[..., None]) * qk_c
        s_inter = jnp.exp(inter - m_i)
        num = (s_inter[..., None] * jnp.einsum("bhld,bhde->bhle", q_c, c_st)
               + jnp.einsum("bhlm,bhme->bhle", wts, v_c))
        den = s_inter * jnp.einsum("bhld,bhd->bhl", q_c, n_st) + jnp.sum(wts, axis=-1)
        h = num / jnp.maximum(jnp.abs(den), jnp.exp(-m_i))[..., None]
        m_new = jnp.maximum(f_c[..., -1] + m_st, am)
        dec = jnp.exp(f_c[..., -1] + m_st - m_new)
        wk = jnp.exp(ae - m_new[..., None])
        c_st = dec[..., None, None] * c_st + jnp.einsum("bhl,bhld,bhle->bhde", wk, k_c, v_c)
        n_st = dec[..., None] * n_st + jnp.einsum("bhl,bhld->bhd", wk, k_c)
        return (c_st, n_st, m_new), h

    init = (jnp.zeros((b_, h_, dqk, dv), jnp.float32), jnp.zeros((b_, h_, dqk), jnp.float32),
            jnp.zeros((b_, h_), jnp.float32))
    _, h = lax.scan(step, init, (qc, kc, vc, fc, d_mat, d_max, qk, a_end, a_max))
    return from_chunks(h)


def hybrid_mixer(u, w_in, conv_w, a_log, dt_bias, gdn_norm, ml_b_i, ml_b_f, ml_norm, w_out):
    b_, s_, _ = u.shape
    f32 = jnp.float32
    z = u @ w_in
    parts = []
    off = 0
    for width in IN_SPLITS:
        parts.append(z[..., off:off + width])
        off += width
    g_q, g_k, g_v, g_a, g_b, g_z, m_q, m_k, m_v, m_i, m_f, m_o = parts

    qkv = jax.nn.silu(causal_dwconv(jnp.concatenate([g_q, g_k, g_v], axis=-1), conv_w)).astype(f32)
    q = l2norm(qkv[..., :GDN_QK].reshape(b_, s_, GDN_HEADS, GDN_DK)) * (GDN_DK ** -0.5)
    k = l2norm(qkv[..., GDN_QK:2 * GDN_QK].reshape(b_, s_, GDN_HEADS, GDN_DK))
    v = qkv[..., 2 * GDN_QK:].reshape(b_, s_, GDN_HEADS, GDN_DV)
    beta = jax.nn.sigmoid(g_b.astype(f32))
    g = -jnp.exp(a_log.astype(f32)) * jax.nn.softplus(g_a.astype(f32) + dt_bias.astype(f32))
    o_gdn = gated_delta_chunked(q, k, v, g, beta)
    o_gdn = rmsnorm(o_gdn, gdn_norm) * jax.nn.silu(g_z.astype(f32).reshape(b_, s_, GDN_HEADS, GDN_DV))

    mq = m_q.astype(f32).reshape(b_, s_, MLSTM_HEADS, MLSTM_DQK)
    mk = m_k.astype(f32).reshape(b_, s_, MLSTM_HEADS, MLSTM_DQK) * (MLSTM_DQK ** -0.5)
    mv = m_v.astype(f32).reshape(b_, s_, MLSTM_HEADS, MLSTM_DV)
    i_pre = softcap(m_i.astype(f32) + ml_b_i.astype(f32), GATE_SOFTCAP)
    log_f = jax.nn.log_sigmoid(softcap(m_f.astype(f32) + ml_b_f.astype(f32), GATE_SOFTCAP))
    h_ml = mlstm_chunked(mq, mk, mv, i_pre, log_f)
    h_ml = rmsnorm(h_ml, ml_norm) * jax.nn.sigmoid(m_o.astype(f32).reshape(b_, s_, MLSTM_HEADS, MLSTM_DV))

    y = jnp.concatenate([o_gdn.reshape(b_, s_, GDN_V), h_ml.reshape(b_, s_, ML_V)], axis=-1)
    return y.astype(u.dtype) @ w_out


def moe_ffn(u, w_router, b_router, w_gu, b_gu, w_down, b_down):
    b_, s_, d = u.shape
    n_tok = b_ * s_
    xt = u.reshape(n_tok, d)
    logits = (xt @ w_router).astype(jnp.float32) + b_router.astype(jnp.float32)
    top_v, top_i = lax.top_k(logits, TOP_K)
    gates = jax.nn.softmax(top_v, axis=-1)
    n_asg = n_tok * TOP_K
    flat_e = top_i.reshape(n_asg).astype(jnp.int32)
    flat_tok = jnp.arange(n_asg, dtype=jnp.int32) // TOP_K
    flat_g = gates.reshape(n_asg)
    order = jnp.argsort(flat_e)
    se, stok, sg = flat_e[order], flat_tok[order], flat_g[order]
    counts = jnp.zeros((N_EXPERTS,), jnp.int32).at[flat_e].add(1)
    padded = (counts + MOE_BLOCK - 1) // MOE_BLOCK * MOE_BLOCK
    start = jnp.cumsum(counts) - counts
    pend = jnp.cumsum(padded)
    pstart = pend - padded
    dest = pstart[se] + (jnp.arange(n_asg, dtype=jnp.int32) - start[se])
    n_blocks = -(-n_asg // MOE_BLOCK) + N_EXPERTS
    n_rows = n_blocks * MOE_BLOCK
    row_tok = jnp.full((n_rows,), n_tok, jnp.int32).at[dest].set(stok)
    row_g = jnp.zeros((n_rows,), jnp.float32).at[dest].set(sg)
    block_start = jnp.arange(n_blocks, dtype=jnp.int32) * MOE_BLOCK
    block_e = jnp.minimum(jnp.searchsorted(pend, block_start, side="right"), N_EXPERTS - 1)
    x_pad = jnp.concatenate([xt, jnp.zeros((1, d), xt.dtype)], axis=0)

    def expert_block(args):
        toks, e = args
        xb = x_pad[toks]
        hgu = xb @ w_gu[e] + b_gu[e]
        gate = jnp.minimum(hgu[:, ::2], SWIGLU_LIMIT)
        up = jnp.clip(hgu[:, 1::2], -SWIGLU_LIMIT, SWIGLU_LIMIT)
        act = gate * jax.nn.sigmoid(gate * SWIGLU_ALPHA) * (up + 1.0)
        return act @ w_down[e] + b_down[e]

    yb = lax.map(expert_block, (row_tok.reshape(n_blocks, MOE_BLOCK), block_e))
    y = jnp.zeros((n_tok + 1, d), jnp.float32).at[row_tok].add(
        yb.reshape(n_rows, d).astype(jnp.float32) * row_g[:, None])
    return y[:n_tok].reshape(b_, s_, d).astype(u.dtype)


def setup_inputs(seed: int = 0) -> dict:
    key = jax.random.key(seed)
    ks = jax.random.split(key, 32)
    f32 = jnp.float32

    def nrm(k, shape, scale):
        return jax.random.normal(k, shape, f32) * scale

    def gain(k, shape):
        return 1.0 + 0.05 * jax.random.normal(k, shape, f32)

    dt = jnp.exp(jax.random.uniform(ks[6], (DEPTH, GDN_HEADS), f32, np.log(1e-3), np.log(1e-1)))
    return {
        "x": nrm(ks[0], (BATCH, SEQ, D_MODEL), 1.0),
        "p": nrm(ks[1], (DEPTH, BATCH, SEQ, PLE_DIM), 1.0),
        "g_mix": gain(ks[2], (DEPTH, D_MODEL)),
        "w_in": nrm(ks[3], (DEPTH, D_MODEL, D_IN), D_MODEL ** -0.5),
        "conv_w": nrm(ks[4], (DEPTH, GDN_CONV, GDN_CONV_CH), GDN_CONV ** -0.5),
        "a_log": jnp.log(jax.random.uniform(ks[5], (DEPTH, GDN_HEADS), f32, 1.0, 16.0)),
        "dt_bias": dt + jnp.log(-jnp.expm1(-dt)),
        "gdn_norm": gain(ks[7], (DEPTH, GDN_DV)),
        "ml_b_i": nrm(ks[8], (DEPTH, MLSTM_HEADS), 0.1),
        "ml_b_f": jnp.linspace(3.0, 6.0, MLSTM_HEADS, dtype=f32)[None, :] + nrm(ks[9], (DEPTH, MLSTM_HEADS), 0.1),
        "ml_norm": gain(ks[10], (DEPTH, MLSTM_HEADS, MLSTM_DV)),
        "w_out": nrm(ks[11], (DEPTH, MIX_WIDTH, D_MODEL), MIX_WIDTH ** -0.5),
        "g_ffn": gain(ks[12], (DEPTH, D_MODEL)),
        "w_router": nrm(ks[13], (DEPTH, D_MODEL, N_EXPERTS), D_MODEL ** -0.5),
        "b_router": nrm(ks[14], (DEPTH, N_EXPERTS), 0.01),
        "w_gu": nrm(ks[15], (DEPTH, N_EXPERTS, D_MODEL, 2 * D_EXPERT), D_MODEL ** -0.5),
        "b_gu": nrm(ks[16], (DEPTH, N_EXPERTS, 2 * D_EXPERT), 0.01),
        "w_down": nrm(ks[17], (DEPTH, N_EXPERTS, D_EXPERT, D_MODEL), D_EXPERT ** -0.5),
        "b_down": nrm(ks[18], (DEPTH, N_EXPERTS, D_MODEL), 0.01),
        "g_ple": gain(ks[19], (DEPTH, D_MODEL)),
        "w_ple_gate": nrm(ks[20], (DEPTH, D_MODEL, D_MODEL), D_MODEL ** -0.5),
        "w_ple_proj": nrm(ks[21], (DEPTH, PLE_DIM, D_MODEL), PLE_DIM ** -0.5),
        "g_ple_post": gain(ks[22], (DEPTH, D_MODEL)),
        "g_final": gain(ks[23], (D_MODEL,)),
    }


def reference(x, p, g_mix, w_in, conv_w, a_log, dt_bias, gdn_norm, ml_b_i, ml_b_f, ml_norm, w_out,
              g_ffn, w_router, b_router, w_gu, b_gu, w_down, b_down, g_ple, w_ple_gate, w_ple_proj,
              g_ple_post, g_final):
    h = x
    for l in range(DEPTH):
        u = rmsnorm(h, g_mix[l])
        h = h + hybrid_mixer(u, w_in[l], conv_w[l], a_log[l], dt_bias[l], gdn_norm[l],
                             ml_b_i[l], ml_b_f[l], ml_norm[l], w_out[l])
        u = rmsnorm(h, g_ffn[l])
        h = h + moe_ffn(u, w_router[l], b_router[l], w_gu[l], b_gu[l], w_down[l], b_down[l])
        gate = jax.nn.sigmoid((rmsnorm(h, g_ple[l]) @ w_ple_gate[l]).astype(jnp.float32))
        pe = rmsnorm(p[l].astype(h.dtype) @ w_ple_proj[l], g_ple_post[l])
        h = h + (gate * pe.astype(jnp.float32)).astype(h.dtype)
    return rmsnorm(h, g_final)
```

```python
import functools

import jax
import jax.numpy as jnp
from jax import lax
from jax.experimental import pallas as pl
from jax.experimental.pallas import tpu as pltpu

F32 = jnp.float32
BF16 = jnp.bfloat16

EPS = 1e-6
CHUNK = 64
GDN_HEADS = 8
GDN_DK = 128
GDN_DV = 128
GDN_CONV = 4
ML_HEADS = 4
ML_DQK = 128
ML_DV = 256
GATE_SOFTCAP = 15.0
N_EXPERTS = 32
TOP_K = 4
SWIGLU_LIMIT = 7.0
SWIGLU_ALPHA = 1.702

LANES = 128
NEG_BIG = -1e30

GDN_QK = GDN_HEADS * GDN_DK
GDN_V = GDN_HEADS * GDN_DV
ML_QK = ML_HEADS * ML_DQK
ML_V = ML_HEADS * ML_DV
OFF_Q = 0
OFF_K = OFF_Q + GDN_QK
OFF_V = OFF_K + GDN_QK
OFF_GZ = OFF_V + GDN_V
OFF_MQ = OFF_GZ + GDN_V
OFF_MK = OFF_MQ + ML_QK
OFF_MV = OFF_MK + ML_QK
OFF_MO = OFF_MV + ML_V
Z_MAIN = OFF_MO + ML_V
GATE_A = 0
GATE_B = GATE_A + GDN_HEADS
GATE_I = GATE_B + GDN_HEADS
GATE_F = GATE_I + ML_HEADS

TIME_BLOCK = 256
VMEM_LIMIT = 56 * 1024 * 1024


def _cparams(sem):
    return pltpu.CompilerParams(dimension_semantics=sem, vmem_limit_bytes=VMEM_LIMIT)


def _bdot(a, b):
    return jnp.dot(a.astype(BF16), b.astype(BF16), preferred_element_type=F32)


def _bdot_nt(a, b):
    return lax.dot_general(a.astype(BF16), b.astype(BF16), (((1,), (1,)), ((), ())),
                           preferred_element_type=F32)


def _bdot_tn(a, b):
    return lax.dot_general(a.astype(BF16), b.astype(BF16), (((0,), (0,)), ((), ())),
                           preferred_element_type=F32)


def _split3(x):
    hi = x.astype(BF16)
    r1 = x - hi.astype(F32)
    mid = r1.astype(BF16)
    lo = (r1 - mid.astype(F32)).astype(BF16)
    return hi, mid, lo


def _dot_sel(sel_bf16, x):
    hi, mid, lo = _split3(x)
    d = lambda t: jnp.dot(sel_bf16, t, preferred_element_type=F32)
    return d(hi) + (d(mid) + d(lo))


def _sigmoid(x):
    return 1.0 / (1.0 + jnp.exp(-x))


def _softplus(x):
    return jnp.maximum(x, 0.0) + jnp.log1p(jnp.exp(-jnp.abs(x)))


def _lane_pick(zg, idx):
    lane = lax.broadcasted_iota(jnp.int32, zg.shape, 1)
    return jnp.sum(jnp.where(lane == idx, zg, 0.0), axis=1, keepdims=True)


def _chunk_masks(n):
    r = lax.broadcasted_iota(jnp.int32, (n, n), 0)
    c = lax.broadcasted_iota(jnp.int32, (n, n), 1)
    same = lambda s: (r >> s) == (c >> s)
    return r, c, same


def _chunk_last(gc):
    n = gc.shape[0] // CHUNK
    parts = [jnp.broadcast_to(gc[CHUNK * (i + 1) - 1:CHUNK * (i + 1), :], (CHUNK, gc.shape[1]))
             for i in range(n)]
    return jnp.concatenate(parts, axis=0)


def _inproj_kernel(x_ref, g_ref, w_ref, wg_ref, z_ref, zg_ref, u_scr):
    @pl.when(pl.program_id(1) == 0)
    def _():
        x = x_ref[...]
        ms = jnp.mean(x * x, axis=-1, keepdims=True)
        u = (x * lax.rsqrt(ms + EPS) * g_ref[...]).astype(BF16)
        u_scr[...] = u
        zg_ref[...] = jnp.dot(u, wg_ref[...], preferred_element_type=F32)

    z_ref[...] = jnp.dot(u_scr[...], w_ref[...], preferred_element_type=F32).astype(z_ref.dtype)


def _in_proj(x2, g_mix, w_main, w_gate, tm=1024, tn=1024):
    t, d = x2.shape
    n = w_main.shape[1]
    return pl.pallas_call(
        _inproj_kernel,
        grid=(t // tm, n // tn),
        in_specs=[
            pl.BlockSpec((tm, d), lambda i, j: (i, 0)),
            pl.BlockSpec((1, d), lambda i, j: (0, 0)),
            pl.BlockSpec((d, tn), lambda i, j: (0, j)),
            pl.BlockSpec((d, LANES), lambda i, j: (0, 0)),
        ],
        out_specs=[
            pl.BlockSpec((tm, tn), lambda i, j: (i, j)),
            pl.BlockSpec((tm, LANES), lambda i, j: (i, 0)),
        ],
        out_shape=[jax.ShapeDtypeStruct((t, n), BF16), jax.ShapeDtypeStruct((t, LANES), F32)],
        scratch_shapes=[pltpu.VMEM((tm, d), BF16)],
        compiler_params=_cparams(("parallel", "arbitrary")),
        name="in_proj",
    )(x2, g_mix.reshape(1, d), w_main, w_gate)


def _tri_inverse_minus_eye(m, same):
    m16 = same(4)
    m32 = same(5)
    a = jnp.where(m16, -m, 0.0)
    acc = a
    for _ in range(3):
        a = _bdot(a, a)
        acc = acc + a + _bdot(acc, a)
    n32 = jnp.where(jnp.logical_and(m32, jnp.logical_not(m16)), m, 0.0)
    n64 = jnp.where(m32, 0.0, m)
    for nmat in (n32, n64):
        y = nmat + _bdot(acc, nmat)
        acc = acc - (y + _bdot(y, acc))
    return acc


def _gdn_kernel(alog_ref, dtb_ref, zq_ref, zk_ref, zv_ref, gz_ref, zg_ref, cwq_ref, cwk_ref, cwv_ref,
                gn_ref, o_ref, s_scr, xq_scr, xk_scr, xv_scr):
    h = pl.program_id(1)
    tb = zq_ref.shape[1]
    nchunk = tb // CHUNK

    @pl.when(pl.program_id(2) == 0)
    def _():
        s_scr[...] = jnp.zeros_like(s_scr)
        for scr in (xq_scr, xk_scr, xv_scr):
            scr[0:8, :] = jnp.zeros((8, LANES), F32)

    def conv_silu(z_ref, x_scr, cw_ref):
        x = z_ref[0].astype(F32)
        x_scr[8:8 + tb, :] = x
        w = cw_ref[...]
        acc = x * w[GDN_CONV - 1:GDN_CONV, :]
        for s in range(1, GDN_CONV):
            acc = acc + x_scr[pl.ds(8 - s, tb), :] * w[GDN_CONV - 1 - s:GDN_CONV - s, :]
        x_scr[0:8, :] = x[tb - 8:tb, :]
        return acc * _sigmoid(acc)

    q = conv_silu(zq_ref, xq_scr, cwq_ref)
    k = conv_silu(zk_ref, xk_scr, cwk_ref)
    v = conv_silu(zv_ref, xv_scr, cwv_ref)
    q = q * lax.rsqrt(jnp.sum(q * q, axis=-1, keepdims=True) + EPS) * (GDN_DK ** -0.5)
    k = k * lax.rsqrt(jnp.sum(k * k, axis=-1, keepdims=True) + EPS)

    zg = zg_ref[0]
    ga = _lane_pick(zg, GATE_A + h)
    gb = _lane_pick(zg, GATE_B + h)
    beta = _sigmoid(gb)
    a_coef = jnp.exp(jnp.full((1, 1), alog_ref[h], F32))
    g = -a_coef * _softplus(ga + dtb_ref[h])

    r, c, same = _chunk_masks(tb)
    in_chunk = same(6)
    tril = jnp.logical_and(in_chunk, r >= c)
    strict = jnp.logical_and(in_chunk, r > c)

    gcb = _dot_sel(tril.astype(BF16), jnp.broadcast_to(g, (tb, LANES)))
    gc_row = gcb.T[0:1, :]
    gc_col = jnp.concatenate([gcb] * (tb // LANES), axis=1)
    decay = jnp.exp(jnp.where(tril, gc_col - gc_row, NEG_BIG))
    eg = jnp.exp(gcb)
    glb = _chunk_last(gcb)

    kb = k * beta
    vb = v * beta
    m_low = jnp.where(strict, _bdot_nt(kb, k) * decay, 0.0)
    t_m1 = _tri_inverse_minus_eye(m_low, same)
    rhs = jnp.concatenate([vb, kb * eg], axis=1)
    uw = rhs + _bdot(t_m1, rhs)
    u = uw[:, :GDN_DV]
    w = uw[:, GDN_DV:]
    attn = _bdot_nt(q, k) * decay
    qd = q * eg
    kd = k * jnp.exp(glb - gcb)

    state = s_scr[...]
    outs = []
    for i in range(nchunk):
        sl = slice(CHUNK * i, CHUNK * (i + 1))
        res1 = _bdot(jnp.concatenate([w[sl], qd[sl]], axis=0), state)
        v_new = u[sl] - res1[:CHUNK]
        o_intra = _bdot(attn[sl, CHUNK * i:CHUNK * (i + 1)], v_new)
        outs.append(res1[CHUNK:] + o_intra)
        g_last = jnp.exp(gcb[CHUNK * (i + 1) - 1:CHUNK * (i + 1), :])
        state = state * g_last[:, 0:1] + _bdot_tn(kd[sl], v_new)
    s_scr[...] = state

    o = jnp.concatenate(outs, axis=0)
    o = o * lax.rsqrt(jnp.mean(o * o, axis=-1, keepdims=True) + EPS) * gn_ref[...]
    gz = gz_ref[0].astype(F32)
    o_ref[0] = (o * (gz * _sigmoid(gz))).astype(o_ref.dtype)


def _gdn(z_main, z_gate, conv_w, a_log, dt_bias, gdn_norm, tb=TIME_BLOCK):
    b, s, _ = z_main.shape
    hq, hk, hv, hz = OFF_Q // LANES, OFF_K // LANES, OFF_V // LANES, OFF_GZ // LANES
    zspec = lambda off: pl.BlockSpec((1, tb, LANES), lambda bi, hi, ti: (bi, ti, off + hi))
    cspec = lambda off: pl.BlockSpec((GDN_CONV, LANES), lambda bi, hi, ti: (0, off + hi))
    smem = pl.BlockSpec(memory_space=pltpu.SMEM)
    return pl.pallas_call(
        _gdn_kernel,
        grid=(b, GDN_HEADS, s // tb),
        in_specs=[
            smem, smem,
            zspec(hq), zspec(hk), zspec(hv), zspec(hz),
            pl.BlockSpec((1, tb, LANES), lambda bi, hi, ti: (bi, ti, 0)),
            cspec(hq), cspec(hk), cspec(hv),
            pl.BlockSpec((1, GDN_DV), lambda bi, hi, ti: (0, 0)),
        ],
        out_specs=pl.BlockSpec((1, tb, GDN_DV), lambda bi, hi, ti: (bi, ti, hi)),
        out_shape=jax.ShapeDtypeStruct((b, s, GDN_V), BF16),
        scratch_shapes=[
            pltpu.VMEM((GDN_DK, GDN_DV), F32),
            pltpu.VMEM((8 + tb, LANES), F32),
            pltpu.VMEM((8 + tb, LANES), F32),
            pltpu.VMEM((8 + tb, LANES), F32),
        ],
        compiler_params=_cparams(("parallel", "parallel", "arbitrary")),
        name="gdn",
    )(a_log, dt_bias, z_main, z_main, z_main, z_main, z_gate, conv_w, conv_w, conv_w,
      gdn_norm.reshape(1, GDN_DV))


def _mlstm_kernel(bi_ref, bf_ref, q_ref, k_ref, v_ref, og_ref, zg_ref, nrm_ref, o_ref, c_scr, m_scr):
    h = pl.program_id(1)
    tb = q_ref.shape[1]
    nchunk = tb // CHUNK

    @pl.when(pl.program_id(2) == 0)
    def _():
        c_scr[...] = jnp.zeros_like(c_scr)
        m_scr[...] = jnp.zeros_like(m_scr)

    q = q_ref[0].astype(F32)
    k = k_ref[0].astype(F32) * (ML_DQK ** -0.5)
    v = v_ref[0].astype(F32)
    zg = zg_ref[0]
    cap = GATE_SOFTCAP
    i_pre = cap * jnp.tanh((_lane_pick(zg, GATE_I + h) + bi_ref[h]) / cap)
    log_f = -_softplus(-(cap * jnp.tanh((_lane_pick(zg, GATE_F + h) + bf_ref[h]) / cap)))

    r, c, same = _chunk_masks(tb)
    tril = jnp.logical_and(same(6), r >= c)

    fcb = _dot_sel(tril.astype(BF16), jnp.broadcast_to(log_f, (tb, LANES)))
    fmi = fcb - i_pre
    fmi_row = fmi.T[0:1, :]
    fc_col = jnp.concatenate([fcb] * (tb // LANES), axis=1)
    d_mat = jnp.where(tril, fc_col - fmi_row, NEG_BIG)
    d_max = jnp.max(d_mat, axis=-1, keepdims=True)
    a_end = _chunk_last(fcb) - fmi

    m_st = m_scr[...]
    m_rows, decs, wks = [], [], []
    for i in range(nchunk):
        sl = slice(CHUNK * i, CHUNK * (i + 1))
        f_last = fcb[CHUNK * (i + 1) - 1:CHUNK * (i + 1), :]
        a_max = jnp.max(a_end[sl], axis=0, keepdims=True)
        m_new = jnp.maximum(f_last + m_st, a_max)
        m_rows.append(jnp.broadcast_to(m_st, (CHUNK, LANES)))
        decs.append(jnp.exp(f_last + m_st - m_new))
        wks.append(jnp.exp(a_end[sl] - m_new))
        m_st = m_new
    m_scr[...] = m_st

    inter = fcb + jnp.concatenate(m_rows, axis=0)
    m_i = jnp.maximum(d_max, inter)
    s_inter = jnp.exp(inter - m_i)
    wts = jnp.exp(d_mat - m_i[:, 0:1]) * _bdot_nt(q, k)
    v_aug = jnp.concatenate([v, jnp.ones((tb, LANES), F32)], axis=1)
    intra = _bdot(wts, v_aug)

    cst = c_scr[...]
    nums = []
    for i in range(nchunk):
        sl = slice(CHUNK * i, CHUNK * (i + 1))
        nums.append(s_inter[sl, 0:1] * _bdot(q[sl], cst) + intra[sl])
        cst = decs[i][:, 0:1] * cst + _bdot_tn(wks[i] * k[sl], v_aug[sl])
    c_scr[...] = cst

    num_aug = jnp.concatenate(nums, axis=0)
    num = num_aug[:, :ML_DV]
    den = num_aug[:, ML_DV:ML_DV + 1]
    hout = num / jnp.maximum(jnp.abs(den), jnp.exp(-m_i[:, 0:1]))
    hout = hout * lax.rsqrt(jnp.mean(hout * hout, axis=-1, keepdims=True) + EPS) * nrm_ref[0]
    o_ref[0] = (hout * _sigmoid(og_ref[0].astype(F32))).astype(o_ref.dtype)


def _mlstm(z_main, z_gate, ml_b_i, ml_b_f, ml_norm, tb=TIME_BLOCK):
    b, s, _ = z_main.shape
    hq, hk = OFF_MQ // ML_DQK, OFF_MK // ML_DQK
    hv, ho = OFF_MV // ML_DV, OFF_MO // ML_DV
    qspec = lambda off: pl.BlockSpec((1, tb, ML_DQK), lambda bi, hi, ti: (bi, ti, off + hi))
    vspec = lambda off: pl.BlockSpec((1, tb, ML_DV), lambda bi, hi, ti: (bi, ti, off + hi))
    smem = pl.BlockSpec(memory_space=pltpu.SMEM)
    return pl.pallas_call(
        _mlstm_kernel,
        grid=(b, ML_HEADS, s // tb),
        in_specs=[
            smem, smem,
            qspec(hq), qspec(hk), vspec(hv), vspec(ho),
            pl.BlockSpec((1, tb, LANES), lambda bi, hi, ti: (bi, ti, 0)),
            pl.BlockSpec((1, 1, ML_DV), lambda bi, hi, ti: (hi, 0, 0)),
        ],
        out_specs=pl.BlockSpec((1, tb, ML_DV), lambda bi, hi, ti: (bi, ti, hi)),
        out_shape=jax.ShapeDtypeStruct((b, s, ML_V), BF16),
        scratch_shapes=[
            pltpu.VMEM((ML_DQK, ML_DV + LANES), F32),
            pltpu.VMEM((1, LANES), F32),
        ],
        compiler_params=_cparams(("parallel", "parallel", "arbitrary")),
        name="mlstm",
    )(ml_b_i, ml_b_f, z_main, z_main, z_main, z_main, z_gate, ml_norm.reshape(ML_HEADS, 1, ML_DV))


def _prep_in_weights(w_in):
    splits = (GDN_QK, GDN_QK, GDN_V, GDN_HEADS, GDN_HEADS, GDN_V, ML_QK, ML_QK, ML_V, ML_HEADS, ML_HEADS, ML_V)
    offs = [0]
    for wd in splits:
        offs.append(offs[-1] + wd)
    part = lambda i: w_in[:, offs[i]:offs[i + 1]]
    w_main = jnp.concatenate([part(i) for i in (0, 1, 2, 5, 6, 7, 8, 11)], axis=1).astype(BF16)
    gates = jnp.concatenate([part(i) for i in (3, 4, 9, 10)], axis=1)
    w_gate = jnp.pad(gates, ((0, 0), (0, LANES - gates.shape[1]))).astype(BF16)
    return w_main, w_gate


def _rms(x, g):
    return x * lax.rsqrt(jnp.mean(x * x, axis=-1, keepdims=True) + EPS) * g


def _outproj_kernel(x_ref, yg_ref, ym_ref, wo1_ref, wo2_ref, gf_ref, wr_ref, br_ref,
                    h_ref, u_ref, ti_ref, tg_ref):
    h1 = (x_ref[...] + jnp.dot(yg_ref[...], wo1_ref[...], preferred_element_type=F32)
          + jnp.dot(ym_ref[...], wo2_ref[...], preferred_element_type=F32))
    h_ref[...] = h1
    u = _rms(h1, gf_ref[...])
    u_ref[...] = u

    u_hi = u.astype(BF16)
    u_lo = (u - u_hi.astype(F32)).astype(BF16)
    wr = wr_ref[...]
    w_hi = wr.astype(BF16)
    w_lo = (wr - w_hi.astype(F32)).astype(BF16)
    d = lambda a, b: jnp.dot(a, b, preferred_element_type=F32)
    logits = d(u_hi, w_hi) + (d(u_hi, w_lo) + d(u_lo, w_hi)) + br_ref[...]

    lane = lax.broadcasted_iota(jnp.int32, logits.shape, 1)
    lg = jnp.where(lane < N_EXPERTS, logits, NEG_BIG)
    vals, idxs = [], []
    for _ in range(TOP_K):
        m = jnp.max(lg, axis=1, keepdims=True)
        idx = jnp.min(jnp.where(lg == m, lane, LANES), axis=1, keepdims=True)
        vals.append(m)
        idxs.append(idx)
        lg = jnp.where(lane == idx, NEG_BIG, lg)
    es = [jnp.exp(vv - vals[0]) for vv in vals]
    tot = es[0] + es[1] + es[2] + es[3]
    ti = jnp.zeros(logits.shape, jnp.int32)
    tg = jnp.zeros(logits.shape, F32)
    for kk in range(TOP_K):
        ti = jnp.where(lane == kk, idxs[kk], ti)
        tg = jnp.where(lane == kk, es[kk] / tot, tg)
    ti_ref[...] = ti
    tg_ref[...] = tg


def _out_proj(x2, y_gdn, y_ml, w_out_bf, g_ffn, w_router_pad, b_router_pad, tm=512):
    t, d = x2.shape
    row = lambda w: pl.BlockSpec((tm, w), lambda i: (i, 0))
    const = lambda shp: pl.BlockSpec(shp, lambda i: (0, 0))
    return pl.pallas_call(
        _outproj_kernel,
        grid=(t // tm,),
        in_specs=[
            row(d), row(GDN_V), row(ML_V),
            pl.BlockSpec((GDN_V, d), lambda i: (0, 0)),
            pl.BlockSpec((ML_V, d), lambda i: (GDN_V // ML_V, 0)),
            const((1, d)), const((d, LANES)), const((1, LANES)),
        ],
        out_specs=[row(d), row(d), row(LANES), row(LANES)],
        out_shape=[jax.ShapeDtypeStruct((t, d), F32), jax.ShapeDtypeStruct((t, d), F32),
                   jax.ShapeDtypeStruct((t, LANES), jnp.int32), jax.ShapeDtypeStruct((t, LANES), F32)],
        compiler_params=_cparams(("parallel",)),
        name="out_proj_router",
    )(x2, y_gdn, y_ml, w_out_bf, w_out_bf, g_ffn.reshape(1, d), w_router_pad, b_router_pad)


FFN_ROWS = 512
FILL_ROWS = 8


def _dispatch_kernel(pend_ref, pad_ref, pos_ref, u_hbm, xs_hbm, zero_scr, sem):
    step = pl.program_id(0)
    n_asg = pos_ref.shape[0]
    tok0 = step * (n_asg // TOP_K)

    def fill_copy(row0):
        return pltpu.make_async_copy(zero_scr, xs_hbm.at[pl.ds(row0, FILL_ROWS), :], sem)

    @pl.when(step == 0)
    def _():
        zero_scr[...] = jnp.zeros_like(zero_scr)

        def per_expert(e, carry):
            n_fill = (pad_ref[e] + FILL_ROWS - 1) // FILL_ROWS

            def start(j, c):
                fill_copy(pl.multiple_of(pend_ref[e] - (j + 1) * FILL_ROWS, FILL_ROWS)).start()
                return c

            def wait(j, c):
                fill_copy(0).wait()
                return c

            lax.fori_loop(0, n_fill, start, 0)
            lax.fori_loop(0, n_fill, wait, 0)
            return carry

        lax.fori_loop(0, N_EXPERTS, per_expert, 0)

    def row_copy(tok, dst):
        return pltpu.make_async_copy(u_hbm.at[pl.ds(tok, 1), :], xs_hbm.at[pl.ds(dst, 1), :], sem)

    def start(a, c):
        row_copy(tok0 + a // TOP_K, pos_ref[a]).start()
        return c

    def wait(a, c):
        row_copy(0, 0).wait()
        return c

    lax.fori_loop(0, n_asg, start, 0)
    lax.fori_loop(0, n_asg, wait, 0)


def _dispatch(u2, pos_flat, pend, pad, n_rows, tt=512):
    t, d = u2.shape
    return pl.pallas_call(
        _dispatch_kernel,
        grid_spec=pltpu.PrefetchScalarGridSpec(
            num_scalar_prefetch=2,
            grid=(t // tt,),
            in_specs=[
                pl.BlockSpec((tt * TOP_K,), lambda i, *_: (i,), memory_space=pltpu.SMEM),
                pl.BlockSpec(memory_space=pl.ANY),
            ],
            out_specs=pl.BlockSpec(memory_space=pl.ANY),
            scratch_shapes=[pltpu.VMEM((FILL_ROWS, d), F32), pltpu.SemaphoreType.DMA(())],
        ),
        out_shape=jax.ShapeDtypeStruct((n_rows, d), F32),
        compiler_params=_cparams(("arbitrary",)),
        name="dispatch",
    )(pend, pad, pos_flat, u2)


def _ffn_kernel(be_ref, nv_ref, x_ref, wg_ref, wu_ref, wd_ref, bg_ref, bu_ref, bd_ref, o_ref, xb_scr):
    i = pl.program_id(0)
    j = pl.program_id(1)

    @pl.when(i < nv_ref[0])
    def _():
        @pl.when(j == 0)
        def _():
            xb_scr[...] = x_ref[...].astype(BF16)

        xb = xb_scr[...]
        g = jnp.dot(xb, wg_ref[0], preferred_element_type=F32) + bg_ref[0]
        u = jnp.dot(xb, wu_ref[0], preferred_element_type=F32) + bu_ref[0]
        g = jnp.minimum(g, SWIGLU_LIMIT)
        u = jnp.clip(u, -SWIGLU_LIMIT, SWIGLU_LIMIT)
        act = g * _sigmoid(g * SWIGLU_ALPHA) * (u + 1.0)
        y = jnp.dot(act.astype(BF16), wd_ref[0], preferred_element_type=F32)

        @pl.when(j == 0)
        def _():
            o_ref[...] = y + bd_ref[0]

        @pl.when(j > 0)
        def _():
            o_ref[...] += y


def _ffn(xs, block_e, n_valid, w_g, w_u, w_d, b_g, b_u, b_d, tm=FFN_ROWS, tf=512):
    n_rows, d = xs.shape
    n_e, _, f = w_g.shape
    nf = f // tf
    n_blocks = n_rows // tm
    ic = lambda i, nv: jnp.minimum(i, nv[0] - 1)
    jc = lambda i, j, nv: jnp.where(i < nv[0], j, nf - 1)
    return pl.pallas_call(
        _ffn_kernel,
        grid_spec=pltpu.PrefetchScalarGridSpec(
            num_scalar_prefetch=2,
            grid=(n_blocks, nf),
            in_specs=[
                pl.BlockSpec((tm, d), lambda i, j, be, nv: (ic(i, nv), 0)),
                pl.BlockSpec((1, d, tf), lambda i, j, be, nv: (be[ic(i, nv)], 0, jc(i, j, nv))),
                pl.BlockSpec((1, d, tf), lambda i, j, be, nv: (be[ic(i, nv)], 0, jc(i, j, nv))),
                pl.BlockSpec((1, tf, d), lambda i, j, be, nv: (be[ic(i, nv)], jc(i, j, nv), 0)),
                pl.BlockSpec((1, 1, tf), lambda i, j, be, nv: (be[ic(i, nv)], 0, jc(i, j, nv))),
                pl.BlockSpec((1, 1, tf), lambda i, j, be, nv: (be[ic(i, nv)], 0, jc(i, j, nv))),
                pl.BlockSpec((1, 1, d), lambda i, j, be, nv: (be[ic(i, nv)], 0, 0)),
            ],
            out_specs=pl.BlockSpec((tm, d), lambda i, j, be, nv: (ic(i, nv), 0)),
            scratch_shapes=[pltpu.VMEM((tm, d), BF16)],
        ),
        out_shape=jax.ShapeDtypeStruct((n_rows, d), F32),
        compiler_params=_cparams(("arbitrary", "arbitrary")),
        name="expert_ffn",
    )(block_e, n_valid, xs, w_g, w_u, w_d, b_g.reshape(n_e, 1, f), b_u.reshape(n_e, 1, f),
      b_d.reshape(n_e, 1, d))


def _combine_kernel(pos_ref, yb_hbm, tg_ref, h1_ref, p_ref, wpg_ref, wpp_ref, gp_ref, gpp_ref, gfin_ref,
                    o_ref, gbuf, sem):
    n_asg = pos_ref.shape[0]

    def row_copy(a):
        return pltpu.make_async_copy(yb_hbm.at[pl.ds(pos_ref[a], 1), :],
                                     gbuf.at[a % TOP_K, pl.ds(a // TOP_K, 1), :], sem)

    def start(a, c):
        row_copy(a).start()
        return c

    def wait(a, c):
        row_copy(a).wait()
        return c

    lax.fori_loop(0, n_asg, start, 0)
    lax.fori_loop(0, n_asg, wait, 0)

    tg = tg_ref[...]
    moe = tg[:, 0:1] * gbuf[0]
    for kk in range(1, TOP_K):
        moe = moe + tg[:, kk:kk + 1] * gbuf[kk]
    h2 = h1_ref[...] + moe
    gate = _sigmoid(_bdot(_rms(h2, gp_ref[...]), wpg_ref[...]))
    pe = _rms(_bdot(p_ref[...], wpp_ref[...]), gpp_ref[...])
    h3 = h2 + gate * pe
    o_ref[...] = _rms(h3, gfin_ref[...])


def _combine(pos_flat, yb, tg, h1, p2, w_pg, w_pp, g_ple, g_ple_post, g_final, tc=128):
    t, d = h1.shape
    pd = p2.shape[1]
    row = lambda w: pl.BlockSpec((tc, w), lambda i: (i, 0))
    const = lambda shp: pl.BlockSpec(shp, lambda i: (0, 0))
    return pl.pallas_call(
        _combine_kernel,
        grid=(t // tc,),
        in_specs=[
            pl.BlockSpec((tc * TOP_K,), lambda i: (i,), memory_space=pltpu.SMEM),
            pl.BlockSpec(memory_space=pl.ANY),
            row(LANES), row(d), row(pd),
            const((d, d)), const((pd, d)), const((1, d)), const((1, d)), const((1, d)),
        ],
        out_specs=row(d),
        out_shape=jax.ShapeDtypeStruct((t, d), F32),
        scratch_shapes=[pltpu.VMEM((TOP_K, tc, d), F32), pltpu.SemaphoreType.DMA(())],
        compiler_params=_cparams(("arbitrary",)),
        name="combine_ple",
    )(pos_flat, yb, tg, h1, p2, w_pg, w_pp, g_ple.reshape(1, d), g_ple_post.reshape(1, d),
      g_final.reshape(1, d))


def _routing_tables(top_i, n_tok, tm=FFN_ROWS):
    e_flat = top_i.reshape(n_tok * TOP_K)
    onehot = (e_flat[:, None] == jnp.arange(N_EXPERTS, dtype=jnp.int32)[None, :]).astype(jnp.int32)
    csum = jnp.cumsum(onehot, axis=0)
    counts = csum[-1]
    padded = (counts + tm - 1) // tm * tm
    pend = jnp.cumsum(padded)
    pstart = pend - padded
    pos = jnp.sum(onehot * (csum - 1 + pstart[None, :]), axis=1).astype(jnp.int32)
    n_blocks = (n_tok * TOP_K) // tm + N_EXPERTS
    block_start = jnp.arange(n_blocks, dtype=jnp.int32) * tm
    block_e = jnp.minimum(jnp.searchsorted(pend, block_start, side="right"), N_EXPERTS - 1).astype(jnp.int32)
    n_valid = (pend[-1] // tm).astype(jnp.int32).reshape(1)
    return pos, pend.astype(jnp.int32), (padded - counts).astype(jnp.int32), block_e, n_valid, n_blocks * tm


def kernel(x, p, g_mix, w_in, conv_w, a_log, dt_bias, gdn_norm, ml_b_i, ml_b_f, ml_norm, w_out, g_ffn, w_router, b_router, w_gu, b_gu, w_down, b_down, g_ple, w_ple_gate, w_ple_proj, g_ple_post, g_final):
    b, s, d = x.shape
    n_tok = b * s
    assert w_in.shape[0] == 1, "single-layer block: the final norm is fused into the layer's last kernel"
    l = 0
    x2 = x.reshape(n_tok, d)
    w_main, w_gate = _prep_in_weights(w_in[l])
    z_main, z_gate = _in_proj(x2, g_mix[l], w_main, w_gate)
    z_main = z_main.reshape(b, s, Z_MAIN)
    z_gate = z_gate.reshape(b, s, LANES)
    y_gdn = _gdn(z_main, z_gate, conv_w[l], a_log[l], dt_bias[l], gdn_norm[l])
    y_ml = _mlstm(z_main, z_gate, ml_b_i[l], ml_b_f[l], ml_norm[l])

    w_r = jnp.pad(w_router[l], ((0, 0), (0, LANES - N_EXPERTS)))
    b_r = jnp.pad(b_router[l], (0, LANES - N_EXPERTS)).reshape(1, LANES)
    h1, u2, top_i, top_g = _out_proj(x2, y_gdn.reshape(n_tok, GDN_V), y_ml.reshape(n_tok, ML_V),
                                     w_out[l].astype(BF16), g_ffn[l], w_r, b_r)

    pos, pend, pad, block_e, n_valid, n_rows = _routing_tables(top_i[:, :TOP_K], n_tok)
    xs = _dispatch(u2, pos, pend, pad, n_rows)
    w_g = w_gu[l][:, :, 0::2].astype(BF16)
    w_u = w_gu[l][:, :, 1::2].astype(BF16)
    yb = _ffn(xs, block_e, n_valid, w_g, w_u, w_down[l].astype(BF16),
              b_gu[l][:, 0::2], b_gu[l][:, 1::2], b_down[l])
    out = _combine(pos, yb, top_g, h1, p[l].reshape(n_tok, -1), w_ple_gate[l].astype(BF16),
                   w_ple_proj[l].astype(BF16), g_ple[l], g_ple_post[l], g_final)
    return out.reshape(b, s, d)
```

```python
import functools

import jax
import jax.numpy as jnp
from jax import lax
from jax.experimental import pallas as pl
from jax.experimental.pallas import tpu as pltpu

F32 = jnp.float32
BF16 = jnp.bfloat16

EPS = 1e-6
CHUNK = 64
GDN_HEADS = 8
GDN_DK = 128
GDN_DV = 128
GDN_CONV = 4
ML_HEADS = 4
ML_DQK = 128
ML_DV = 256
GATE_SOFTCAP = 15.0
N_EXPERTS = 32
TOP_K = 4
SWIGLU_LIMIT = 7.0
SWIGLU_ALPHA = 1.702

LANES = 128
NEG_BIG = -1e30

GDN_QK = GDN_HEADS * GDN_DK
GDN_V = GDN_HEADS * GDN_DV
ML_QK = ML_HEADS * ML_DQK
ML_V = ML_HEADS * ML_DV
OFF_Q = 0
OFF_K = OFF_Q + GDN_QK
OFF_V = OFF_K + GDN_QK
OFF_GZ = OFF_V + GDN_V
OFF_MQ = OFF_GZ + GDN_V
OFF_MK = OFF_MQ + ML_QK
OFF_MV = OFF_MK + ML_QK
OFF_MO = OFF_MV + ML_V
Z_MAIN = OFF_MO + ML_V
GATE_A = 0
GATE_B = GATE_A + GDN_HEADS
GATE_I = GATE_B + GDN_HEADS
GATE_F = GATE_I + ML_HEADS

TIME_BLOCK = 256
VMEM_LIMIT = 56 * 1024 * 1024


def _cparams(sem):
    return pltpu.CompilerParams(dimension_semantics=sem, vmem_limit_bytes=VMEM_LIMIT)


def _bdot(a, b):
    return jnp.dot(a.astype(BF16), b.astype(BF16), preferred_element_type=F32)


def _bdot_nt(a, b):
    return lax.dot_general(a.astype(BF16), b.astype(BF16), (((1,), (1,)), ((), ())),
                           preferred_element_type=F32)


def _bdot_tn(a, b):
    return lax.dot_general(a.astype(BF16), b.astype(BF16), (((0,), (0,)), ((), ())),
                           preferred_element_type=F32)


def _split3(x):
    hi = x.astype(BF16)
    r1 = x - hi.astype(F32)
    mid = r1.astype(BF16)
    lo = (r1 - mid.astype(F32)).astype(BF16)
    return hi, mid, lo


def _dot_sel(sel_bf16, x):
    hi, mid, lo = _split3(x)
    d = lambda t: jnp.dot(sel_bf16, t, preferred_element_type=F32)
    return d(hi) + (d(mid) + d(lo))


def _sigmoid(x):
    return 1.0 / (1.0 + jnp.exp(-x))


def _softplus(x):
    return jnp.maximum(x, 0.0) + jnp.log1p(jnp.exp(-jnp.abs(x)))


def _lane_pick(zg, idx):
    lane = lax.broadcasted_iota(jnp.int32, zg.shape, 1)
    return jnp.sum(jnp.where(lane == idx, zg, 0.0), axis=1, keepdims=True)


def _chunk_masks(n):
    r = lax.broadcasted_iota(jnp.int32, (n, n), 0)
    c = lax.broadcasted_iota(jnp.int32, (n, n), 1)
    same = lambda s: (r >> s) == (c >> s)
    return r, c, same


def _chunk_last(gc):
    n = gc.shape[0] // CHUNK
    parts = [jnp.broadcast_to(gc[CHUNK * (i + 1) - 1:CHUNK * (i + 1), :], (CHUNK, gc.shape[1]))
             for i in range(n)]
    return jnp.concatenate(parts, axis=0)


def _inproj_kernel(x_ref, g_ref, w_ref, wg_ref, z_ref, zg_ref, u_scr):
    @pl.when(pl.program_id(1) == 0)
    def _():
        x = x_ref[...]
        ms = jnp.mean(x * x, axis=-1, keepdims=True)
        u = (x * lax.rsqrt(ms + EPS) * g_ref[...]).astype(BF16)
        u_scr[...] = u
        zg_ref[...] = jnp.dot(u, wg_ref[...], preferred_element_type=F32)

    z_ref[...] = jnp.dot(u_scr[...], w_ref[...], preferred_element_type=F32).astype(z_ref.dtype)


def _in_proj(x2, g_mix, w_main, w_gate, tm=1024, tn=1024):
    t, d = x2.shape
    n = w_main.shape[1]
    return pl.pallas_call(
        _inproj_kernel,
        grid=(t // tm, n // tn),
        in_specs=[
            pl.BlockSpec((tm, d), lambda i, j: (i, 0)),
            pl.BlockSpec((1, d), lambda i, j: (0, 0)),
            pl.BlockSpec((d, tn), lambda i, j: (0, j)),
            pl.BlockSpec((d, LANES), lambda i, j: (0, 0)),
        ],
        out_specs=[
            pl.BlockSpec((tm, tn), lambda i, j: (i, j)),
            pl.BlockSpec((tm, LANES), lambda i, j: (i, 0)),
        ],
        out_shape=[jax.ShapeDtypeStruct((t, n), BF16), jax.ShapeDtypeStruct((t, LANES), F32)],
        scratch_shapes=[pltpu.VMEM((tm, d), BF16)],
        compiler_params=_cparams(("parallel", "arbitrary")),
        name="in_proj",
    )(x2, g_mix.reshape(1, d), w_main, w_gate)


def _tri_inverse_minus_eye(m, same):
    m16 = same(4)
    m32 = same(5)
    a = jnp.where(m16, -m, 0.0)
    acc = a
    for _ in range(3):
        a = _bdot(a, a)
        acc = acc + a + _bdot(acc, a)
    n32 = jnp.where(jnp.logical_and(m32, jnp.logical_not(m16)), m, 0.0)
    n64 = jnp.where(m32, 0.0, m)
    for nmat in (n32, n64):
        y = nmat + _bdot(acc, nmat)
        acc = acc - (y + _bdot(y, acc))
    return acc


def _gdn_kernel(alog_ref, dtb_ref, zq_ref, zk_ref, zv_ref, gz_ref, zg_ref, cwq_ref, cwk_ref, cwv_ref,
                gn_ref, o_ref, s_scr, xq_scr, xk_scr, xv_scr):
    h = pl.program_id(1)
    tb = zq_ref.shape[1]
    nchunk = tb // CHUNK

    @pl.when(pl.program_id(2) == 0)
    def _():
        s_scr[...] = jnp.zeros_like(s_scr)
        for scr in (xq_scr, xk_scr, xv_scr):
            scr[0:8, :] = jnp.zeros((8, LANES), F32)

    def conv_silu(z_ref, x_scr, cw_ref):
        x = z_ref[0].astype(F32)
        x_scr[8:8 + tb, :] = x
        w = cw_ref[...]
        acc = x * w[GDN_CONV - 1:GDN_CONV, :]
        for s in range(1, GDN_CONV):
            acc = acc + x_scr[pl.ds(8 - s, tb), :] * w[GDN_CONV - 1 - s:GDN_CONV - s, :]
        x_scr[0:8, :] = x[tb - 8:tb, :]
        return acc * _sigmoid(acc)

    q = conv_silu(zq_ref, xq_scr, cwq_ref)
    k = conv_silu(zk_ref, xk_scr, cwk_ref)
    v = conv_silu(zv_ref, xv_scr, cwv_ref)
    q = q * lax.rsqrt(jnp.sum(q * q, axis=-1, keepdims=True) + EPS) * (GDN_DK ** -0.5)
    k = k * lax.rsqrt(jnp.sum(k * k, axis=-1, keepdims=True) + EPS)

    zg = zg_ref[0]
    ga = _lane_pick(zg, GATE_A + h)
    gb = _lane_pick(zg, GATE_B + h)
    beta = _sigmoid(gb)
    a_coef = jnp.exp(jnp.full((1, 1), alog_ref[h], F32))
    g = -a_coef * _softplus(ga + dtb_ref[h])

    r, c, same = _chunk_masks(tb)
    in_chunk = same(6)
    tril = jnp.logical_and(in_chunk, r >= c)
    strict = jnp.logical_and(in_chunk, r > c)

    gcb = _dot_sel(tril.astype(BF16), jnp.broadcast_to(g, (tb, LANES)))
    gc_row = gcb.T[0:1, :]
    gc_col = jnp.concatenate([gcb] * (tb // LANES), axis=1)
    decay = jnp.exp(jnp.where(tril, gc_col - gc_row, NEG_BIG))
    eg = jnp.exp(gcb)
    glb = _chunk_last(gcb)

    kb = k * beta
    vb = v * beta
    m_low = jnp.where(strict, _bdot_nt(kb, k) * decay, 0.0)
    t_m1 = _tri_inverse_minus_eye(m_low, same)
    rhs = jnp.concatenate([vb, kb * eg], axis=1)
    uw = rhs + _bdot(t_m1, rhs)
    u = uw[:, :GDN_DV]
    w = uw[:, GDN_DV:]
    attn = _bdot_nt(q, k) * decay
    qd = q * eg
    kd = k * jnp.exp(glb - gcb)

    state = s_scr[...]
    outs = []
    for i in range(nchunk):
        sl = slice(CHUNK * i, CHUNK * (i + 1))
        res1 = _bdot(jnp.concatenate([w[sl], qd[sl]], axis=0), state)
        v_new = u[sl] - res1[:CHUNK]
        o_intra = _bdot(attn[sl, CHUNK * i:CHUNK * (i + 1)], v_new)
        outs.append(res1[CHUNK:] + o_intra)
        g_last = jnp.exp(gcb[CHUNK * (i + 1) - 1:CHUNK * (i + 1), :])
        state = state * g_last[:, 0:1] + _bdot_tn(kd[sl], v_new)
    s_scr[...] = state

    o = jnp.concatenate(outs, axis=0)
    o = o * lax.rsqrt(jnp.mean(o * o, axis=-1, keepdims=True) + EPS) * gn_ref[...]
    gz = gz_ref[0].astype(F32)
    o_ref[0] = (o * (gz * _sigmoid(gz))).astype(o_ref.dtype)


def _gdn(z_main, z_gate, conv_w, a_log, dt_bias, gdn_norm, tb=TIME_BLOCK):
    b, s, _ = z_main.shape
    hq, hk, hv, hz = OFF_Q // LANES, OFF_K // LANES, OFF_V // LANES, OFF_GZ // LANES
    zspec = lambda off: pl.BlockSpec((1, tb, LANES), lambda bi, hi, ti: (bi, ti, off + hi))
    cspec = lambda off: pl.BlockSpec((GDN_CONV, LANES), lambda bi, hi, ti: (0, off + hi))
    smem = pl.BlockSpec(memory_space=pltpu.SMEM)
    return pl.pallas_call(
        _gdn_kernel,
        grid=(b, GDN_HEADS, s // tb),
        in_specs=[
            smem, smem,
            zspec(hq), zspec(hk), zspec(hv), zspec(hz),
            pl.BlockSpec((1, tb, LANES), lambda bi, hi, ti: (bi, ti, 0)),
            cspec(hq), cspec(hk), cspec(hv),
            pl.BlockSpec((1, GDN_DV), lambda bi, hi, ti: (0, 0)),
        ],
        out_specs=pl.BlockSpec((1, tb, GDN_DV), lambda bi, hi, ti: (bi, ti, hi)),
        out_shape=jax.ShapeDtypeStruct((b, s, GDN_V), BF16),
        scratch_shapes=[
            pltpu.VMEM((GDN_DK, GDN_DV), F32),
            pltpu.VMEM((8 + tb, LANES), F32),
            pltpu.VMEM((8 + tb, LANES), F32),
            pltpu.VMEM((8 + tb, LANES), F32),
        ],
        compiler_params=_cparams(("parallel", "parallel", "arbitrary")),
        name="gdn",
    )(a_log, dt_bias, z_main, z_main, z_main, z_main, z_gate, conv_w, conv_w, conv_w,
      gdn_norm.reshape(1, GDN_DV))


def _mlstm_kernel(bi_ref, bf_ref, q_ref, k_ref, v_ref, og_ref, zg_ref, nrm_ref, o_ref, c_scr, m_scr):
    h = pl.program_id(1)
    tb = q_ref.shape[1]
    nchunk = tb // CHUNK

    @pl.when(pl.program_id(2) == 0)
    def _():
        c_scr[...] = jnp.zeros_like(c_scr)
        m_scr[...] = jnp.zeros_like(m_scr)

    q = q_ref[0].astype(F32)
    k = k_ref[0].astype(F32) * (ML_DQK ** -0.5)
    v = v_ref[0].astype(F32)
    zg = zg_ref[0]
    cap = GATE_SOFTCAP
    i_pre = cap * jnp.tanh((_lane_pick(zg, GATE_I + h) + bi_ref[h]) / cap)
    log_f = -_softplus(-(cap * jnp.tanh((_lane_pick(zg, GATE_F + h) + bf_ref[h]) / cap)))

    r, c, same = _chunk_masks(tb)
    tril = jnp.logical_and(same(6), r >= c)

    fcb = _dot_sel(tril.astype(BF16), jnp.broadcast_to(log_f, (tb, LANES)))
    fmi = fcb - i_pre
    fmi_row = fmi.T[0:1, :]
    fc_col = jnp.concatenate([fcb] * (tb // LANES), axis=1)
    d_mat = jnp.where(tril, fc_col - fmi_row, NEG_BIG)
    d_max = jnp.max(d_mat, axis=-1, keepdims=True)
    a_end = _chunk_last(fcb) - fmi

    m_st = m_scr[...]
    m_rows, decs, wks = [], [], []
    for i in range(nchunk):
        sl = slice(CHUNK * i, CHUNK * (i + 1))
        f_last = fcb[CHUNK * (i + 1) - 1:CHUNK * (i + 1), :]
        a_max = jnp.max(a_end[sl], axis=0, keepdims=True)
        m_new = jnp.maximum(f_last + m_st, a_max)
        m_rows.append(jnp.broadcast_to(m_st, (CHUNK, LANES)))
        decs.append(jnp.exp(f_last + m_st - m_new))
        wks.append(jnp.exp(a_end[sl] - m_new))
        m_st = m_new
    m_scr[...] = m_st

    inter = fcb + jnp.concatenate(m_rows, axis=0)
    m_i = jnp.maximum(d_max, inter)
    s_inter = jnp.exp(inter - m_i)
    wts = jnp.exp(d_mat - m_i[:, 0:1]) * _bdot_nt(q, k)
    v_aug = jnp.concatenate([v, jnp.ones((tb, LANES), F32)], axis=1)
    intra = _bdot(wts, v_aug)

    cst = c_scr[...]
    nums = []
    for i in range(nchunk):
        sl = slice(CHUNK * i, CHUNK * (i + 1))
        nums.append(s_inter[sl, 0:1] * _bdot(q[sl], cst) + intra[sl])
        cst = decs[i][:, 0:1] * cst + _bdot_tn(wks[i] * k[sl], v_aug[sl])
    c_scr[...] = cst

    num_aug = jnp.concatenate(nums, axis=0)
    num = num_aug[:, :ML_DV]
    den = num_aug[:, ML_DV:ML_DV + 1]
    hout = num / jnp.maximum(jnp.abs(den), jnp.exp(-m_i[:, 0:1]))
    hout = hout * lax.rsqrt(jnp.mean(hout * hout, axis=-1, keepdims=True) + EPS) * nrm_ref[0]
    o_ref[0] = (hout * _sigmoid(og_ref[0].astype(F32))).astype(o_ref.dtype)


def _mlstm(z_main, z_gate, ml_b_i, ml_b_f, ml_norm, tb=TIME_BLOCK):
    b, s, _ = z_main.shape
    hq, hk = OFF_MQ // ML_DQK, OFF_MK // ML_DQK
    hv, ho = OFF_MV // ML_DV, OFF_MO // ML_DV
    qspec = lambda off: pl.BlockSpec((1, tb, ML_DQK), lambda bi, hi, ti: (bi, ti, off + hi))
    vspec = lambda off: pl.BlockSpec((1, tb, ML_DV), lambda bi, hi, ti: (bi, ti, off + hi))
    smem = pl.BlockSpec(memory_space=pltpu.SMEM)
    return pl.pallas_call(
        _mlstm_kernel,
        grid=(b, ML_HEADS, s // tb),
        in_specs=[
            smem, smem,
            qspec(hq), qspec(hk), vspec(hv), vspec(ho),
            pl.BlockSpec((1, tb, LANES), lambda bi, hi, ti: (bi, ti, 0)),
            pl.BlockSpec((1, 1, ML_DV), lambda bi, hi, ti: (hi, 0, 0)),
        ],
        out_specs=pl.BlockSpec((1, tb, ML_DV), lambda bi, hi, ti: (bi, ti, hi)),
        out_shape=jax.ShapeDtypeStruct((b, s, ML_V), BF16),
        scratch_shapes=[
            pltpu.VMEM((ML_DQK, ML_DV + LANES), F32),
            pltpu.VMEM((1, LANES), F32),
        ],
        compiler_params=_cparams(("parallel", "parallel", "arbitrary")),
        name="mlstm",
    )(ml_b_i, ml_b_f, z_main, z_main, z_main, z_main, z_gate, ml_norm.reshape(ML_HEADS, 1, ML_DV))


def _prep_in_weights(w_in):
    splits = (GDN_QK, GDN_QK, GDN_V, GDN_HEADS, GDN_HEADS, GDN_V, ML_QK, ML_QK, ML_V, ML_HEADS, ML_HEADS, ML_V)
    offs = [0]
    for wd in splits:
        offs.append(offs[-1] + wd)
    part = lambda i: w_in[:, offs[i]:offs[i + 1]]
    w_main = jnp.concatenate([part(i) for i in (0, 1, 2, 5, 6, 7, 8, 11)], axis=1).astype(BF16)
    gates = jnp.concatenate([part(i) for i in (3, 4, 9, 10)], axis=1)
    w_gate = jnp.pad(gates, ((0, 0), (0, LANES - gates.shape[1]))).astype(BF16)
    return w_main, w_gate


PACK_SUB = 8


def _pack_rows(x, o_ref):
    n, d = x.shape
    half = d // 2
    for j in range(PACK_SUB):
        lo = x[:, LANES * j:LANES * (j + 1)].astype(BF16).astype(F32)
        hi = x[:, half + LANES * j:half + LANES * (j + 1)].astype(BF16).astype(F32)
        word = (lax.bitcast_convert_type(lo, jnp.uint32) >> 16) | lax.bitcast_convert_type(hi, jnp.uint32)
        o_ref[pl.ds(j, n, stride=PACK_SUB), :] = word


def _unpack_chunk(word):
    lo = lax.bitcast_convert_type(word << 16, F32)
    hi = lax.bitcast_convert_type(word & jnp.uint32(0xFFFF0000), F32)
    return lo, hi


def _rms(x, g):
    return x * lax.rsqrt(jnp.mean(x * x, axis=-1, keepdims=True) + EPS) * g


def _outproj_kernel(x_ref, yg_ref, ym_ref, wo1_ref, wo2_ref, gf_ref, wr_ref, br_ref,
                    h_ref, u_ref, ti_ref, tg_ref):
    h1 = (x_ref[...] + jnp.dot(yg_ref[...], wo1_ref[...], preferred_element_type=F32)
          + jnp.dot(ym_ref[...], wo2_ref[...], preferred_element_type=F32))
    h_ref[...] = h1
    u = _rms(h1, gf_ref[...])
    _pack_rows(u, u_ref)

    u_hi = u.astype(BF16)
    u_lo = (u - u_hi.astype(F32)).astype(BF16)
    wr = wr_ref[...]
    w_hi = wr.astype(BF16)
    w_lo = (wr - w_hi.astype(F32)).astype(BF16)
    d = lambda a, b: jnp.dot(a, b, preferred_element_type=F32)
    logits = d(u_hi, w_hi) + (d(u_hi, w_lo) + d(u_lo, w_hi)) + br_ref[...]

    lane = lax.broadcasted_iota(jnp.int32, logits.shape, 1)
    lg = jnp.where(lane < N_EXPERTS, logits, NEG_BIG)
    vals, idxs = [], []
    for _ in range(TOP_K):
        m = jnp.max(lg, axis=1, keepdims=True)
        idx = jnp.min(jnp.where(lg == m, lane, LANES), axis=1, keepdims=True)
        vals.append(m)
        idxs.append(idx)
        lg = jnp.where(lane == idx, NEG_BIG, lg)
    es = [jnp.exp(vv - vals[0]) for vv in vals]
    tot = es[0] + es[1] + es[2] + es[3]
    ti = jnp.zeros(logits.shape, jnp.int32)
    tg = jnp.zeros(logits.shape, F32)
    for kk in range(TOP_K):
        ti = jnp.where(lane == kk, idxs[kk], ti)
        tg = jnp.where(lane == kk, es[kk] / tot, tg)
    ti_ref[...] = ti
    tg_ref[...] = tg


def _out_proj(x2, y_gdn, y_ml, w_out_bf, g_ffn, w_router_pad, b_router_pad, tm=512):
    t, d = x2.shape
    row = lambda w: pl.BlockSpec((tm, w), lambda i: (i, 0))
    const = lambda shp: pl.BlockSpec(shp, lambda i: (0, 0))
    return pl.pallas_call(
        _outproj_kernel,
        grid=(t // tm,),
        in_specs=[
            row(d), row(GDN_V), row(ML_V),
            pl.BlockSpec((GDN_V, d), lambda i: (0, 0)),
            pl.BlockSpec((ML_V, d), lambda i: (GDN_V // ML_V, 0)),
            const((1, d)), const((d, LANES)), const((1, LANES)),
        ],
        out_specs=[row(d), pl.BlockSpec((tm * PACK_SUB, LANES), lambda i: (i, 0)), row(LANES), row(LANES)],
        out_shape=[jax.ShapeDtypeStruct((t, d), F32), jax.ShapeDtypeStruct((t * PACK_SUB, LANES), jnp.uint32),
                   jax.ShapeDtypeStruct((t, LANES), jnp.int32), jax.ShapeDtypeStruct((t, LANES), F32)],
        compiler_params=_cparams(("parallel",)),
        name="out_proj_router",
    )(x2, y_gdn, y_ml, w_out_bf, w_out_bf, g_ffn.reshape(1, d), w_router_pad, b_router_pad)


FFN_ROWS = 512
FILL_ROWS = 8


def _row_tile(ref, row, n=1):
    start = row * PACK_SUB
    if not isinstance(start, int):
        start = pl.multiple_of(start, PACK_SUB)
    return ref.at[pl.ds(start, n * PACK_SUB), :]


def _dispatch_kernel(pend_ref, pad_ref, pos_ref, u_ref, xs_hbm, zero_scr, sem):
    step = pl.program_id(0)
    n_asg = pos_ref.shape[0]

    def fill_copy(row0):
        return pltpu.make_async_copy(zero_scr, _row_tile(xs_hbm, row0, FILL_ROWS), sem)

    @pl.when(step == 0)
    def _():
        zero_scr[...] = jnp.zeros_like(zero_scr)

        def per_expert(e, carry):
            n_fill = (pad_ref[e] + FILL_ROWS - 1) // FILL_ROWS

            def start(j, c):
                fill_copy(pend_ref[e] - (j + 1) * FILL_ROWS).start()
                return c

            def wait(j, c):
                fill_copy(0).wait()
                return c

            lax.fori_loop(0, n_fill, start, 0)
            lax.fori_loop(0, n_fill, wait, 0)
            return carry

        lax.fori_loop(0, N_EXPERTS, per_expert, 0)

    def row_copy(a, dst):
        return pltpu.make_async_copy(_row_tile(u_ref, a // TOP_K), _row_tile(xs_hbm, dst), sem)

    def start(a, c):
        row_copy(a, pos_ref[a]).start()
        return c

    def wait(a, c):
        row_copy(0, 0).wait()
        return c

    lax.fori_loop(0, n_asg, start, 0)
    lax.fori_loop(0, n_asg, wait, 0)


def _dispatch(u_packed, pos_flat, pend, pad, n_rows, tt=512):
    t = u_packed.shape[0] // PACK_SUB
    return pl.pallas_call(
        _dispatch_kernel,
        grid_spec=pltpu.PrefetchScalarGridSpec(
            num_scalar_prefetch=2,
            grid=(t // tt,),
            in_specs=[
                pl.BlockSpec((tt * TOP_K,), lambda i, *_: (i,), memory_space=pltpu.SMEM),
                pl.BlockSpec((tt * PACK_SUB, LANES), lambda i, *_: (i, 0)),
            ],
            out_specs=pl.BlockSpec(memory_space=pl.ANY),
            scratch_shapes=[pltpu.VMEM((FILL_ROWS * PACK_SUB, LANES), jnp.uint32), pltpu.SemaphoreType.DMA(())],
        ),
        out_shape=jax.ShapeDtypeStruct((n_rows * PACK_SUB, LANES), jnp.uint32),
        compiler_params=_cparams(("arbitrary",)),
        name="dispatch",
    )(pend, pad, pos_flat, u_packed)


DEINT_GROUP = 2 * LANES


def _ffn_kernel(be_ref, nv_ref, x_ref, wgu_ref, wd_ref, bgu_ref, bd_ref, o_ref, xb_scr, acc_scr):
    i = pl.program_id(0)
    j = pl.program_id(1)
    tm, d = xb_scr.shape
    half = d // 2

    @pl.when(i < nv_ref[0])
    def _():
        @pl.when(j == 0)
        def _():
            for c in range(PACK_SUB):
                lo, hi = _unpack_chunk(x_ref[pl.ds(c, tm, stride=PACK_SUB), :])
                xb_scr[:, LANES * c:LANES * (c + 1)] = lo.astype(BF16)
                xb_scr[:, half + LANES * c:half + LANES * (c + 1)] = hi.astype(BF16)

        hgu = jnp.dot(xb_scr[...], wgu_ref[0], preferred_element_type=F32) + bgu_ref[0]
        width = hgu.shape[1]
        gate = jnp.minimum(hgu, SWIGLU_LIMIT)
        up = jnp.clip(pltpu.roll(hgu, width - 1, axis=1), -SWIGLU_LIMIT, SWIGLU_LIMIT)
        act = (gate * _sigmoid(gate * SWIGLU_ALPHA) * (up + 1.0)).astype(BF16)
        r = lax.broadcasted_iota(jnp.int32, (DEINT_GROUP, LANES), 0)
        c = lax.broadcasted_iota(jnp.int32, (DEINT_GROUP, LANES), 1)
        sel = (r == 2 * c).astype(BF16)
        parts = [jnp.dot(act[:, DEINT_GROUP * k:DEINT_GROUP * (k + 1)], sel, preferred_element_type=F32)
                 for k in range(width // DEINT_GROUP)]
        act_c = jnp.concatenate(parts, axis=1).astype(BF16)
        y = jnp.dot(act_c, wd_ref[0], preferred_element_type=F32)

        @pl.when(j == 0)
        def _():
            acc_scr[...] = y + bd_ref[0]

        @pl.when(j > 0)
        def _():
            acc_scr[...] += y

        @pl.when(j == pl.num_programs(1) - 1)
        def _():
            _pack_rows(acc_scr[...], o_ref)


def _ffn(xs, block_e, n_valid, w_gu, w_d, b_gu, b_d, tm=FFN_ROWS, tf=512):
    n_rows = xs.shape[0] // PACK_SUB
    n_e, f, d = w_d.shape
    nf = f // tf
    n_blocks = n_rows // tm
    ic = lambda i, nv: jnp.minimum(i, nv[0] - 1)
    jc = lambda i, j, nv: jnp.where(i < nv[0], j, nf - 1)
    return pl.pallas_call(
        _ffn_kernel,
        grid_spec=pltpu.PrefetchScalarGridSpec(
            num_scalar_prefetch=2,
            grid=(n_blocks, nf),
            in_specs=[
                pl.BlockSpec((tm * PACK_SUB, LANES), lambda i, j, be, nv: (ic(i, nv), 0)),
                pl.BlockSpec((1, d, 2 * tf), lambda i, j, be, nv: (be[ic(i, nv)], 0, jc(i, j, nv))),
                pl.BlockSpec((1, tf, d), lambda i, j, be, nv: (be[ic(i, nv)], jc(i, j, nv), 0)),
                pl.BlockSpec((1, 1, 2 * tf), lambda i, j, be, nv: (be[ic(i, nv)], 0, jc(i, j, nv))),
                pl.BlockSpec((1, 1, d), lambda i, j, be, nv: (be[ic(i, nv)], 0, 0)),
            ],
            out_specs=pl.BlockSpec((tm * PACK_SUB, LANES), lambda i, j, be, nv: (ic(i, nv), 0)),
            scratch_shapes=[pltpu.VMEM((tm, d), BF16), pltpu.VMEM((tm, d), F32)],
        ),
        out_shape=jax.ShapeDtypeStruct((n_rows * PACK_SUB, LANES), jnp.uint32),
        compiler_params=_cparams(("arbitrary", "arbitrary")),
        name="expert_ffn",
    )(block_e, n_valid, xs, w_gu, w_d, b_gu.reshape(n_e, 1, 2 * f), b_d.reshape(n_e, 1, d))


def _combine_kernel(pos_ref, yb_hbm, tg_ref, h1_ref, p_ref, wpg_ref, wpp_ref, gp_ref, gpp_ref, gfin_ref,
                    o_ref, gbuf, sem):
    n_asg = pos_ref.shape[0]

    tc = tg_ref.shape[0]

    def row_copy(a):
        return pltpu.make_async_copy(_row_tile(yb_hbm, pos_ref[a]), _row_tile(gbuf.at[a % TOP_K], a // TOP_K), sem)

    def start(a, c):
        row_copy(a).start()
        return c

    def wait(a, c):
        row_copy(a).wait()
        return c

    lax.fori_loop(0, n_asg, start, 0)
    lax.fori_loop(0, n_asg, wait, 0)

    tg = tg_ref[...]
    lo_parts, hi_parts = [], []
    for c in range(PACK_SUB):
        lo_sum = hi_sum = None
        for kk in range(TOP_K):
            lo, hi = _unpack_chunk(gbuf[kk, pl.ds(c, tc, stride=PACK_SUB), :])
            gk = tg[:, kk:kk + 1]
            lo_sum = gk * lo if lo_sum is None else lo_sum + gk * lo
            hi_sum = gk * hi if hi_sum is None else hi_sum + gk * hi
        lo_parts.append(lo_sum)
        hi_parts.append(hi_sum)
    moe = jnp.concatenate(lo_parts + hi_parts, axis=1)
    h2 = h1_ref[...] + moe
    gate = _sigmoid(_bdot(_rms(h2, gp_ref[...]), wpg_ref[...]))
    pe = _rms(_bdot(p_ref[...], wpp_ref[...]), gpp_ref[...])
    h3 = h2 + gate * pe
    o_ref[...] = _rms(h3, gfin_ref[...])


def _combine(pos_flat, yb, tg, h1, p2, w_pg, w_pp, g_ple, g_ple_post, g_final, tc=128):
    t, d = h1.shape
    pd = p2.shape[1]
    row = lambda w: pl.BlockSpec((tc, w), lambda i: (i, 0))
    const = lambda shp: pl.BlockSpec(shp, lambda i: (0, 0))
    return pl.pallas_call(
        _combine_kernel,
        grid=(t // tc,),
        in_specs=[
            pl.BlockSpec((tc * TOP_K,), lambda i: (i,), memory_space=pltpu.SMEM),
            pl.BlockSpec(memory_space=pl.ANY),
            row(LANES), row(d), row(pd),
            const((d, d)), const((pd, d)), const((1, d)), const((1, d)), const((1, d)),
        ],
        out_specs=row(d),
        out_shape=jax.ShapeDtypeStruct((t, d), F32),
        scratch_shapes=[pltpu.VMEM((TOP_K, tc * PACK_SUB, LANES), jnp.uint32), pltpu.SemaphoreType.DMA(())],
        compiler_params=_cparams(("arbitrary",)),
        name="combine_ple",
    )(pos_flat, yb, tg, h1, p2, w_pg, w_pp, g_ple.reshape(1, d), g_ple_post.reshape(1, d),
      g_final.reshape(1, d))


def _routing_tables(top_i, n_tok, tm=FFN_ROWS):
    e_flat = top_i.reshape(n_tok * TOP_K)
    onehot = (e_flat[:, None] == jnp.arange(N_EXPERTS, dtype=jnp.int32)[None, :]).astype(jnp.int32)
    csum = jnp.cumsum(onehot, axis=0)
    counts = csum[-1]
    padded = (counts + tm - 1) // tm * tm
    pend = jnp.cumsum(padded)
    pstart = pend - padded
    pos = jnp.sum(onehot * (csum - 1 + pstart[None, :]), axis=1).astype(jnp.int32)
    n_blocks = (n_tok * TOP_K) // tm + N_EXPERTS
    block_start = jnp.arange(n_blocks, dtype=jnp.int32) * tm
    block_e = jnp.minimum(jnp.searchsorted(pend, block_start, side="right"), N_EXPERTS - 1).astype(jnp.int32)
    n_valid = (pend[-1] // tm).astype(jnp.int32).reshape(1)
    return pos, pend.astype(jnp.int32), (padded - counts).astype(jnp.int32), block_e, n_valid, n_blocks * tm


def kernel(x, p, g_mix, w_in, conv_w, a_log, dt_bias, gdn_norm, ml_b_i, ml_b_f, ml_norm, w_out, g_ffn, w_router, b_router, w_gu, b_gu, w_down, b_down, g_ple, w_ple_gate, w_ple_proj, g_ple_post, g_final):
    b, s, d = x.shape
    n_tok = b * s
    assert w_in.shape[0] == 1, "single-layer block: the final norm is fused into the layer's last kernel"
    l = 0
    x2 = x.reshape(n_tok, d)
    w_main, w_gate = _prep_in_weights(w_in[l])
    z_main, z_gate = _in_proj(x2, g_mix[l], w_main, w_gate)
    z_main = z_main.reshape(b, s, Z_MAIN)
    z_gate = z_gate.reshape(b, s, LANES)
    y_gdn = _gdn(z_main, z_gate, conv_w[l], a_log[l], dt_bias[l], gdn_norm[l])
    y_ml = _mlstm(z_main, z_gate, ml_b_i[l], ml_b_f[l], ml_norm[l])

    w_r = jnp.pad(w_router[l], ((0, 0), (0, LANES - N_EXPERTS)))
    b_r = jnp.pad(b_router[l], (0, LANES - N_EXPERTS)).reshape(1, LANES)
    h1, u2, top_i, top_g = _out_proj(x2, y_gdn.reshape(n_tok, GDN_V), y_ml.reshape(n_tok, ML_V),
                                     w_out[l].astype(BF16), g_ffn[l], w_r, b_r)

    pos, pend, pad, block_e, n_valid, n_rows = _routing_tables(top_i[:, :TOP_K], n_tok)
    xs = _dispatch(u2, pos, pend, pad, n_rows)
    yb = _ffn(xs, block_e, n_valid, w_gu[l].astype(BF16), w_down[l].astype(BF16), b_gu[l], b_down[l])
    out = _combine(pos, yb, top_g, h1, p[l].reshape(n_tok, -1), w_ple_gate[l].astype(BF16),
                   w_ple_proj[l].astype(BF16), g_ple[l], g_ple_post[l], g_final)
    return out.reshape(b, s, d)
```

```python
import jax
import jax.numpy as jnp
from jax import lax
from jax.experimental import pallas as pl
from jax.experimental.pallas import tpu as pltpu

F32 = jnp.float32
BF16 = jnp.bfloat16

EPS = 1e-6
CHUNK = 64
GDN_HEADS = 8
GDN_DK = 128
GDN_DV = 128
GDN_CONV = 4
ML_HEADS = 4
ML_DQK = 128
ML_DV = 256
GATE_SOFTCAP = 15.0
N_EXPERTS = 32
TOP_K = 4
SWIGLU_LIMIT = 7.0
SWIGLU_ALPHA = 1.702

LANES = 128
NEG_BIG = -1e30

GDN_QK = GDN_HEADS * GDN_DK
GDN_V = GDN_HEADS * GDN_DV
ML_QK = ML_HEADS * ML_DQK
ML_V = ML_HEADS * ML_DV
OFF_Q = 0
OFF_K = OFF_Q + GDN_QK
OFF_V = OFF_K + GDN_QK
OFF_GZ = OFF_V + GDN_V
OFF_MQ = OFF_GZ + GDN_V
OFF_MK = OFF_MQ + ML_QK
OFF_MV = OFF_MK + ML_QK
OFF_MO = OFF_MV + ML_V
Z_MAIN = OFF_MO + ML_V
GATE_A = 0
GATE_B = GATE_A + GDN_HEADS
GATE_I = GATE_B + GDN_HEADS
GATE_F = GATE_I + ML_HEADS

TIME_BLOCK = 256
GDN_HEADS_PER_STEP = 4
ML_HEADS_PER_STEP = 4
VMEM_LIMIT = 56 * 1024 * 1024


def _cparams(sem):
    return pltpu.CompilerParams(dimension_semantics=sem, vmem_limit_bytes=VMEM_LIMIT)


def _bdot(a, b):
    return jnp.dot(a.astype(BF16), b.astype(BF16), preferred_element_type=F32)


def _bdot_nt(a, b):
    return lax.dot_general(a.astype(BF16), b.astype(BF16), (((1,), (1,)), ((), ())),
                           preferred_element_type=F32)


def _bdot_tn(a, b):
    return lax.dot_general(a.astype(BF16), b.astype(BF16), (((0,), (0,)), ((), ())),
                           preferred_element_type=F32)


def _split3(x):
    hi = x.astype(BF16)
    r1 = x - hi.astype(F32)
    mid = r1.astype(BF16)
    lo = (r1 - mid.astype(F32)).astype(BF16)
    return hi, mid, lo


def _dot_sel(sel_bf16, x):
    hi, mid, lo = _split3(x)
    d = lambda t: jnp.dot(sel_bf16, t, preferred_element_type=F32)
    return d(hi) + (d(mid) + d(lo))


def _sigmoid(x):
    return 1.0 / (1.0 + jnp.exp(-x))


def _softplus(x):
    return jnp.maximum(x, 0.0) + jnp.log1p(jnp.exp(-jnp.abs(x)))


def _rms(x, g):
    return x * lax.rsqrt(jnp.mean(x * x, axis=-1, keepdims=True) + EPS) * g


def _lane_pick(zg, idx):
    lane = lax.broadcasted_iota(jnp.int32, zg.shape, 1)
    return jnp.sum(jnp.where(lane == idx, zg, 0.0), axis=1, keepdims=True)


def _chunk_masks(n):
    r = lax.broadcasted_iota(jnp.int32, (n, n), 0)
    c = lax.broadcasted_iota(jnp.int32, (n, n), 1)
    same = lambda s: (r >> s) == (c >> s)
    return r, c, same


def _chunk_last(gc):
    n = gc.shape[0] // CHUNK
    parts = [jnp.broadcast_to(gc[CHUNK * (i + 1) - 1:CHUNK * (i + 1), :], (CHUNK, gc.shape[1]))
             for i in range(n)]
    return jnp.concatenate(parts, axis=0)


def _inproj_kernel(x_ref, g_ref, w_ref, wg_ref, z_ref, zg_ref, u_scr):
    @pl.when(pl.program_id(1) == 0)
    def _():
        u = _rms(x_ref[...], g_ref[...]).astype(BF16)
        u_scr[...] = u
        zg_ref[...] = jnp.dot(u, wg_ref[...], preferred_element_type=F32)

    z_ref[...] = jnp.dot(u_scr[...], w_ref[...], preferred_element_type=F32).astype(z_ref.dtype)


def _in_proj(x2, g_mix, w_main, w_gate, tm=1024, tn=1024):
    t, d = x2.shape
    n = w_main.shape[1]
    return pl.pallas_call(
        _inproj_kernel,
        grid=(t // tm, n // tn),
        in_specs=[
            pl.BlockSpec((tm, d), lambda i, j: (i, 0)),
            pl.BlockSpec((1, d), lambda i, j: (0, 0)),
            pl.BlockSpec((d, tn), lambda i, j: (0, j)),
            pl.BlockSpec((d, LANES), lambda i, j: (0, 0)),
        ],
        out_specs=[
            pl.BlockSpec((tm, tn), lambda i, j: (i, j)),
            pl.BlockSpec((tm, LANES), lambda i, j: (i, 0)),
        ],
        out_shape=[jax.ShapeDtypeStruct((t, n), BF16), jax.ShapeDtypeStruct((t, LANES), F32)],
        scratch_shapes=[pltpu.VMEM((tm, d), BF16)],
        compiler_params=_cparams(("parallel", "arbitrary")),
        name="in_proj",
    )(x2, g_mix.reshape(1, d), w_main, w_gate)


def _prep_in_weights(w_in):
    splits = (GDN_QK, GDN_QK, GDN_V, GDN_HEADS, GDN_HEADS, GDN_V, ML_QK, ML_QK, ML_V, ML_HEADS, ML_HEADS, ML_V)
    offs = [0]
    for wd in splits:
        offs.append(offs[-1] + wd)
    part = lambda i: w_in[:, offs[i]:offs[i + 1]]
    w_main = jnp.concatenate([part(i) for i in (0, 1, 2, 5, 6, 7, 8, 11)], axis=1).astype(BF16)
    gates = jnp.concatenate([part(i) for i in (3, 4, 9, 10)], axis=1)
    w_gate = jnp.pad(gates, ((0, 0), (0, LANES - gates.shape[1]))).astype(BF16)
    return w_main, w_gate


def _tri_inverse_minus_eye(ms, same):
    m16 = same(4)
    m32 = same(5)
    n32_mask = jnp.logical_and(m32, jnp.logical_not(m16))
    a = [jnp.where(m16, -m, 0.0) for m in ms]
    acc = list(a)
    for _ in range(3):
        a = [_bdot(x, x) for x in a]
        acc = [p + x + _bdot(p, x) for p, x in zip(acc, a)]
    for level in range(2):
        ns = [jnp.where(n32_mask, m, 0.0) if level == 0 else jnp.where(m32, 0.0, m) for m in ms]
        ys = [n + _bdot(p, n) for p, n in zip(acc, ns)]
        acc = [p - (y + _bdot(y, p)) for p, y in zip(acc, ys)]
    return acc


def _gdn_kernel(alog_ref, dtb_ref, zq_ref, zk_ref, zv_ref, gz_ref, zg_ref, cwq_ref, cwk_ref, cwv_ref,
                gn_ref, o_ref, s_scr, xq_scr, xk_scr, xv_scr):
    hb = s_scr.shape[0]
    h0 = pl.program_id(1) * hb
    tb = zq_ref.shape[1]
    nchunk = tb // CHUNK
    heads = range(hb)
    lanes = lambda hh: slice(LANES * hh, LANES * (hh + 1))

    @pl.when(pl.program_id(2) == 0)
    def _():
        s_scr[...] = jnp.zeros_like(s_scr)
        for scr in (xq_scr, xk_scr, xv_scr):
            scr[0:8, :] = jnp.zeros((8, scr.shape[1]), F32)

    def conv_silu(z_ref, x_scr, cw_ref):
        x = z_ref[0].astype(F32)
        x_scr[8:8 + tb, :] = x
        w = cw_ref[...]
        acc = x * w[GDN_CONV - 1:GDN_CONV, :]
        for s in range(1, GDN_CONV):
            acc = acc + x_scr[pl.ds(8 - s, tb), :] * w[GDN_CONV - 1 - s:GDN_CONV - s, :]
        x_scr[0:8, :] = x[tb - 8:tb, :]
        return acc * _sigmoid(acc)

    q_all = conv_silu(zq_ref, xq_scr, cwq_ref)
    k_all = conv_silu(zk_ref, xk_scr, cwk_ref)
    v_all = conv_silu(zv_ref, xv_scr, cwv_ref)
    gz_all = gz_ref[0].astype(F32)
    zg = zg_ref[0]

    r, c, same = _chunk_masks(tb)
    in_chunk = same(6)
    tril = jnp.logical_and(in_chunk, r >= c)
    strict = jnp.logical_and(in_chunk, r > c)
    tril_bf = tril.astype(BF16)

    def head_prep(hh):
        q = q_all[:, lanes(hh)]
        k = k_all[:, lanes(hh)]
        v = v_all[:, lanes(hh)]
        q = q * lax.rsqrt(jnp.sum(q * q, axis=-1, keepdims=True) + EPS) * (GDN_DK ** -0.5)
        k = k * lax.rsqrt(jnp.sum(k * k, axis=-1, keepdims=True) + EPS)
        beta = _sigmoid(_lane_pick(zg, GATE_B + h0 + hh))
        a_coef = jnp.exp(jnp.full((1, 1), alog_ref[h0 + hh], F32))
        g = -a_coef * _softplus(_lane_pick(zg, GATE_A + h0 + hh) + dtb_ref[h0 + hh])
        gcb = _dot_sel(tril_bf, jnp.broadcast_to(g, (tb, LANES)))
        gc_row = gcb.T[0:1, :]
        gc_col = jnp.concatenate([gcb] * (tb // LANES), axis=1)
        decay = jnp.exp(jnp.where(tril, gc_col - gc_row, NEG_BIG))
        eg = jnp.exp(gcb)
        kb = k * beta
        m_low = jnp.where(strict, _bdot_nt(kb, k) * decay, 0.0)
        rhs = jnp.concatenate([v * beta, kb * eg], axis=1)
        attn = _bdot_nt(q, k) * decay
        kd = k * jnp.exp(_chunk_last(gcb) - gcb)
        return dict(m_low=m_low, rhs=rhs, attn=attn, qd=q * eg, kd=kd, gcb=gcb)

    hp = [head_prep(hh) for hh in heads]
    t_m1 = _tri_inverse_minus_eye([p["m_low"] for p in hp], same)
    uws = [p["rhs"] + _bdot(t, p["rhs"]) for p, t in zip(hp, t_m1)]

    states = [s_scr[hh] for hh in heads]
    outs = [[] for _ in heads]
    for i in range(nchunk):
        sl = slice(CHUNK * i, CHUNK * (i + 1))
        for hh in heads:
            p, uw = hp[hh], uws[hh]
            res1 = _bdot(jnp.concatenate([uw[sl, GDN_DV:], p["qd"][sl]], axis=0), states[hh])
            v_new = uw[sl, :GDN_DV] - res1[:CHUNK]
            o_intra = _bdot(p["attn"][sl, CHUNK * i:CHUNK * (i + 1)], v_new)
            outs[hh].append(res1[CHUNK:] + o_intra)
            g_last = jnp.exp(p["gcb"][CHUNK * (i + 1) - 1:CHUNK * (i + 1), :])
            states[hh] = states[hh] * g_last[:, 0:1] + _bdot_tn(p["kd"][sl], v_new)

    for hh in heads:
        s_scr[hh] = states[hh]
        o = _rms(jnp.concatenate(outs[hh], axis=0), gn_ref[...])
        gz = gz_all[:, lanes(hh)]
        o_ref[0, :, lanes(hh)] = (o * (gz * _sigmoid(gz))).astype(o_ref.dtype)


def _gdn(z_main, z_gate, conv_w, a_log, dt_bias, gdn_norm, tb=TIME_BLOCK, hb=GDN_HEADS_PER_STEP):
    b, s, _ = z_main.shape
    wid = hb * LANES
    hq, hk, hv, hz = OFF_Q // wid, OFF_K // wid, OFF_V // wid, OFF_GZ // wid
    zspec = lambda off: pl.BlockSpec((1, tb, wid), lambda bi, hi, ti: (bi, ti, off + hi))
    cspec = lambda off: pl.BlockSpec((GDN_CONV, wid), lambda bi, hi, ti: (0, off + hi))
    smem = pl.BlockSpec(memory_space=pltpu.SMEM)
    return pl.pallas_call(
        _gdn_kernel,
        grid=(b, GDN_HEADS // hb, s // tb),
        in_specs=[
            smem, smem,
            zspec(hq), zspec(hk), zspec(hv), zspec(hz),
            pl.BlockSpec((1, tb, LANES), lambda bi, hi, ti: (bi, ti, 0)),
            cspec(hq), cspec(hk), cspec(hv),
            pl.BlockSpec((1, GDN_DV), lambda bi, hi, ti: (0, 0)),
        ],
        out_specs=pl.BlockSpec((1, tb, wid), lambda bi, hi, ti: (bi, ti, hi)),
        out_shape=jax.ShapeDtypeStruct((b, s, GDN_V), BF16),
        scratch_shapes=[
            pltpu.VMEM((hb, GDN_DK, GDN_DV), F32),
            pltpu.VMEM((8 + tb, wid), F32),
            pltpu.VMEM((8 + tb, wid), F32),
            pltpu.VMEM((8 + tb, wid), F32),
        ],
        compiler_params=_cparams(("parallel", "parallel", "arbitrary")),
        name="gdn",
    )(a_log, dt_bias, z_main, z_main, z_main, z_main, z_gate, conv_w, conv_w, conv_w,
      gdn_norm.reshape(1, GDN_DV))


def _mlstm_kernel(bi_ref, bf_ref, q_ref, k_ref, v_ref, og_ref, zg_ref, nrm_ref, o_ref, c_scr, m_scr):
    hb = c_scr.shape[0]
    h0 = pl.program_id(1) * hb
    tb = q_ref.shape[1]
    nchunk = tb // CHUNK
    cap = GATE_SOFTCAP

    @pl.when(pl.program_id(2) == 0)
    def _():
        c_scr[...] = jnp.zeros_like(c_scr)
        m_scr[...] = jnp.zeros_like(m_scr)

    zg = zg_ref[0]
    r, c, same = _chunk_masks(tb)
    tril = jnp.logical_and(same(6), r >= c)
    tril_bf = tril.astype(BF16)
    ones = jnp.ones((tb, LANES), F32)

    for hh in range(hb):
        q = q_ref[0, :, ML_DQK * hh:ML_DQK * (hh + 1)].astype(F32)
        k = k_ref[0, :, ML_DQK * hh:ML_DQK * (hh + 1)].astype(F32) * (ML_DQK ** -0.5)
        v = v_ref[0, :, ML_DV * hh:ML_DV * (hh + 1)].astype(F32)
        i_pre = cap * jnp.tanh((_lane_pick(zg, GATE_I + h0 + hh) + bi_ref[h0 + hh]) / cap)
        log_f = -_softplus(-(cap * jnp.tanh((_lane_pick(zg, GATE_F + h0 + hh) + bf_ref[h0 + hh]) / cap)))

        fcb = _dot_sel(tril_bf, jnp.broadcast_to(log_f, (tb, LANES)))
        fmi = fcb - i_pre
        fmi_row = fmi.T[0:1, :]
        fc_col = jnp.concatenate([fcb] * (tb // LANES), axis=1)
        d_mat = jnp.where(tril, fc_col - fmi_row, NEG_BIG)
        d_max = jnp.max(d_mat, axis=-1, keepdims=True)
        a_end = _chunk_last(fcb) - fmi

        m_st = m_scr[hh]
        m_rows, decs, wks = [], [], []
        for i in range(nchunk):
            sl = slice(CHUNK * i, CHUNK * (i + 1))
            f_last = fcb[CHUNK * (i + 1) - 1:CHUNK * (i + 1), :]
            a_max = jnp.max(a_end[sl], axis=0, keepdims=True)
            m_new = jnp.maximum(f_last + m_st, a_max)
            m_rows.append(jnp.broadcast_to(m_st, (CHUNK, LANES)))
            decs.append(jnp.exp(f_last + m_st - m_new))
            wks.append(jnp.exp(a_end[sl] - m_new))
            m_st = m_new
        m_scr[hh] = m_st

        inter = fcb + jnp.concatenate(m_rows, axis=0)
        m_i = jnp.maximum(d_max, inter)
        s_inter = jnp.exp(inter - m_i)
        wts = jnp.exp(d_mat - m_i[:, 0:1]) * _bdot_nt(q, k)
        v_aug = jnp.concatenate([v, ones], axis=1)
        intra = _bdot(wts, v_aug)

        cst = c_scr[hh]
        nums = []
        for i in range(nchunk):
            sl = slice(CHUNK * i, CHUNK * (i + 1))
            nums.append(s_inter[sl, 0:1] * _bdot(q[sl], cst) + intra[sl])
            cst = decs[i][:, 0:1] * cst + _bdot_tn(wks[i] * k[sl], v_aug[sl])
        c_scr[hh] = cst

        num_aug = jnp.concatenate(nums, axis=0)
        den = num_aug[:, ML_DV:ML_DV + 1]
        hout = num_aug[:, :ML_DV] / jnp.maximum(jnp.abs(den), jnp.exp(-m_i[:, 0:1]))
        hout = _rms(hout, nrm_ref[hh])
        og = og_ref[0, :, ML_DV * hh:ML_DV * (hh + 1)].astype(F32)
        o_ref[0, :, ML_DV * hh:ML_DV * (hh + 1)] = (hout * _sigmoid(og)).astype(o_ref.dtype)


def _mlstm(z_main, z_gate, ml_b_i, ml_b_f, ml_norm, tb=TIME_BLOCK, hb=ML_HEADS_PER_STEP):
    b, s, _ = z_main.shape
    qw, vw = hb * ML_DQK, hb * ML_DV
    hq, hk = OFF_MQ // qw, OFF_MK // qw
    hv, ho = OFF_MV // vw, OFF_MO // vw
    qspec = lambda off: pl.BlockSpec((1, tb, qw), lambda bi, hi, ti: (bi, ti, off + hi))
    vspec = lambda off: pl.BlockSpec((1, tb, vw), lambda bi, hi, ti: (bi, ti, off + hi))
    smem = pl.BlockSpec(memory_space=pltpu.SMEM)
    return pl.pallas_call(
        _mlstm_kernel,
        grid=(b, ML_HEADS // hb, s // tb),
        in_specs=[
            smem, smem,
            qspec(hq), qspec(hk), vspec(hv), vspec(ho),
            pl.BlockSpec((1, tb, LANES), lambda bi, hi, ti: (bi, ti, 0)),
            pl.BlockSpec((hb, 1, ML_DV), lambda bi, hi, ti: (hi, 0, 0)),
        ],
        out_specs=pl.BlockSpec((1, tb, vw), lambda bi, hi, ti: (bi, ti, hi)),
        out_shape=jax.ShapeDtypeStruct((b, s, ML_V), BF16),
        scratch_shapes=[
            pltpu.VMEM((hb, ML_DQK, ML_DV + LANES), F32),
            pltpu.VMEM((hb, 1, LANES), F32),
        ],
        compiler_params=_cparams(("parallel", "parallel", "arbitrary")),
        name="mlstm",
    )(ml_b_i, ml_b_f, z_main, z_main, z_main, z_main, z_gate, ml_norm.reshape(ML_HEADS, 1, ML_DV))


PACK_SUB = 8


def _pack_rows(x, o_ref):
    n, d = x.shape
    half = d // 2
    for j in range(PACK_SUB):
        lo = x[:, LANES * j:LANES * (j + 1)].astype(BF16).astype(F32)
        hi = x[:, half + LANES * j:half + LANES * (j + 1)].astype(BF16).astype(F32)
        word = (lax.bitcast_convert_type(lo, jnp.uint32) >> 16) | lax.bitcast_convert_type(hi, jnp.uint32)
        o_ref[pl.ds(j, n, stride=PACK_SUB), :] = word


def _unpack_chunk(word):
    lo = lax.bitcast_convert_type(word << 16, F32)
    hi = lax.bitcast_convert_type(word & jnp.uint32(0xFFFF0000), F32)
    return lo, hi


def _row_tile(ref, row, n=1):
    start = row * PACK_SUB
    if not isinstance(start, int):
        start = pl.multiple_of(start, PACK_SUB)
    return ref.at[pl.ds(start, n * PACK_SUB), :]


def _outproj_kernel(x_ref, yg_ref, ym_ref, wo1_ref, wo2_ref, gf_ref, wr_ref, br_ref,
                    h_ref, u_ref, ti_ref, tg_ref):
    h1 = (x_ref[...] + jnp.dot(yg_ref[...], wo1_ref[...], preferred_element_type=F32)
          + jnp.dot(ym_ref[...], wo2_ref[...], preferred_element_type=F32))
    h_ref[...] = h1
    u = _rms(h1, gf_ref[...])
    _pack_rows(u, u_ref)

    u_hi = u.astype(BF16)
    u_lo = (u - u_hi.astype(F32)).astype(BF16)
    wr = wr_ref[...]
    w_hi = wr.astype(BF16)
    w_lo = (wr - w_hi.astype(F32)).astype(BF16)
    d = lambda a, b: jnp.dot(a, b, preferred_element_type=F32)
    logits = d(u_hi, w_hi) + (d(u_hi, w_lo) + d(u_lo, w_hi)) + br_ref[...]

    lane = lax.broadcasted_iota(jnp.int32, logits.shape, 1)
    lg = jnp.where(lane < N_EXPERTS, logits, NEG_BIG)
    vals, idxs = [], []
    for _ in range(TOP_K):
        m = jnp.max(lg, axis=1, keepdims=True)
        idx = jnp.min(jnp.where(lg == m, lane, LANES), axis=1, keepdims=True)
        vals.append(m)
        idxs.append(idx)
        lg = jnp.where(lane == idx, NEG_BIG, lg)
    es = [jnp.exp(vv - vals[0]) for vv in vals]
    tot = es[0] + es[1] + es[2] + es[3]
    ti = jnp.zeros(logits.shape, jnp.int32)
    tg = jnp.zeros(logits.shape, F32)
    for kk in range(TOP_K):
        ti = jnp.where(lane == kk, idxs[kk], ti)
        tg = jnp.where(lane == kk, es[kk] / tot, tg)
    ti_ref[...] = ti
    tg_ref[...] = tg


def _out_proj(x2, y_gdn, y_ml, w_out_bf, g_ffn, w_router_pad, b_router_pad, tm=512):
    t, d = x2.shape
    row = lambda w: pl.BlockSpec((tm, w), lambda i: (i, 0))
    const = lambda shp: pl.BlockSpec(shp, lambda i: (0, 0))
    return pl.pallas_call(
        _outproj_kernel,
        grid=(t // tm,),
        in_specs=[
            row(d), row(GDN_V), row(ML_V),
            pl.BlockSpec((GDN_V, d), lambda i: (0, 0)),
            pl.BlockSpec((ML_V, d), lambda i: (GDN_V // ML_V, 0)),
            const((1, d)), const((d, LANES)), const((1, LANES)),
        ],
        out_specs=[row(d), pl.BlockSpec((tm * PACK_SUB, LANES), lambda i: (i, 0)), row(LANES), row(LANES)],
        out_shape=[jax.ShapeDtypeStruct((t, d), F32), jax.ShapeDtypeStruct((t * PACK_SUB, LANES), jnp.uint32),
                   jax.ShapeDtypeStruct((t, LANES), jnp.int32), jax.ShapeDtypeStruct((t, LANES), F32)],
        compiler_params=_cparams(("parallel",)),
        name="out_proj_router",
    )(x2, y_gdn, y_ml, w_out_bf, w_out_bf, g_ffn.reshape(1, d), w_router_pad, b_router_pad)


FFN_ROWS = 512
FILL_ROWS = 8
ISSUE_UNROLL = 8


def _dispatch_kernel(pend_ref, pad_ref, pos_ref, u_ref, xs_hbm, zero_scr, sem):
    step = pl.program_id(0)
    n_asg = pos_ref.shape[0]

    def fill_copy(row0):
        return pltpu.make_async_copy(zero_scr, _row_tile(xs_hbm, row0, FILL_ROWS), sem)

    @pl.when(step == 0)
    def _():
        zero_scr[...] = jnp.zeros_like(zero_scr)

        def per_expert(e, carry):
            n_fill = (pad_ref[e] + FILL_ROWS - 1) // FILL_ROWS

            def start(j, c):
                fill_copy(pend_ref[e] - (j + 1) * FILL_ROWS).start()
                return c

            def wait(j, c):
                fill_copy(0).wait()
                return c

            lax.fori_loop(0, n_fill, start, 0)
            lax.fori_loop(0, n_fill, wait, 0)
            return carry

        lax.fori_loop(0, N_EXPERTS, per_expert, 0)

    def start(a, c):
        pltpu.make_async_copy(_row_tile(u_ref, a // TOP_K), _row_tile(xs_hbm, pos_ref[a]), sem).start()
        return c

    lax.fori_loop(0, n_asg, start, 0, unroll=ISSUE_UNROLL)
    for _ in range(TOP_K):
        pltpu.make_async_copy(u_ref, _row_tile(xs_hbm, 0, n_asg // TOP_K), sem).wait()


def _dispatch(u_packed, pos_flat, pend, pad, n_rows, tt=512):
    t = u_packed.shape[0] // PACK_SUB
    return pl.pallas_call(
        _dispatch_kernel,
        grid_spec=pltpu.PrefetchScalarGridSpec(
            num_scalar_prefetch=2,
            grid=(t // tt,),
            in_specs=[
                pl.BlockSpec((tt * TOP_K,), lambda i, *_: (i,), memory_space=pltpu.SMEM),
                pl.BlockSpec((tt * PACK_SUB, LANES), lambda i, *_: (i, 0)),
            ],
            out_specs=pl.BlockSpec(memory_space=pl.ANY),
            scratch_shapes=[pltpu.VMEM((FILL_ROWS * PACK_SUB, LANES), jnp.uint32), pltpu.SemaphoreType.DMA(())],
        ),
        out_shape=jax.ShapeDtypeStruct((n_rows * PACK_SUB, LANES), jnp.uint32),
        compiler_params=_cparams(("arbitrary",)),
        name="dispatch",
    )(pend, pad, pos_flat, u_packed)


DEINT_GROUP = 2 * LANES


def _ffn_kernel(be_ref, nv_ref, x_ref, wgu_ref, wd_ref, bgu_ref, bd_ref, o_ref, xb_scr, acc_scr):
    i = pl.program_id(0)
    j = pl.program_id(1)
    tm, d = xb_scr.shape
    half = d // 2

    @pl.when(i < nv_ref[0])
    def _():
        @pl.when(j == 0)
        def _():
            for c in range(PACK_SUB):
                lo, hi = _unpack_chunk(x_ref[pl.ds(c, tm, stride=PACK_SUB), :])
                xb_scr[:, LANES * c:LANES * (c + 1)] = lo.astype(BF16)
                xb_scr[:, half + LANES * c:half + LANES * (c + 1)] = hi.astype(BF16)

        hgu = jnp.dot(xb_scr[...], wgu_ref[0], preferred_element_type=F32) + bgu_ref[0]
        width = hgu.shape[1]
        gate = jnp.minimum(hgu, SWIGLU_LIMIT)
        up = jnp.clip(pltpu.roll(hgu, width - 1, axis=1), -SWIGLU_LIMIT, SWIGLU_LIMIT)
        act = (gate * _sigmoid(gate * SWIGLU_ALPHA) * (up + 1.0)).astype(BF16)
        r = lax.broadcasted_iota(jnp.int32, (DEINT_GROUP, LANES), 0)
        c = lax.broadcasted_iota(jnp.int32, (DEINT_GROUP, LANES), 1)
        sel = (r == 2 * c).astype(BF16)
        parts = [jnp.dot(act[:, DEINT_GROUP * k:DEINT_GROUP * (k + 1)], sel, preferred_element_type=F32)
                 for k in range(width // DEINT_GROUP)]
        act_c = jnp.concatenate(parts, axis=1).astype(BF16)
        y = jnp.dot(act_c, wd_ref[0], preferred_element_type=F32)

        @pl.when(j == 0)
        def _():
            acc_scr[...] = y + bd_ref[0]

        @pl.when(j > 0)
        def _():
            acc_scr[...] += y

        @pl.when(j == pl.num_programs(1) - 1)
        def _():
            _pack_rows(acc_scr[...], o_ref)


def _ffn(xs, block_e, n_valid, w_gu, w_d, b_gu, b_d, tm=FFN_ROWS, tf=1024):
    n_rows = xs.shape[0] // PACK_SUB
    n_e, f, d = w_d.shape
    nf = f // tf
    n_blocks = n_rows // tm
    ic = lambda i, nv: jnp.minimum(i, nv[0] - 1)
    jc = lambda i, j, nv: jnp.where(i < nv[0], j, nf - 1)
    return pl.pallas_call(
        _ffn_kernel,
        grid_spec=pltpu.PrefetchScalarGridSpec(
            num_scalar_prefetch=2,
            grid=(n_blocks, nf),
            in_specs=[
                pl.BlockSpec((tm * PACK_SUB, LANES), lambda i, j, be, nv: (ic(i, nv), 0)),
                pl.BlockSpec((1, d, 2 * tf), lambda i, j, be, nv: (be[ic(i, nv)], 0, jc(i, j, nv))),
                pl.BlockSpec((1, tf, d), lambda i, j, be, nv: (be[ic(i, nv)], jc(i, j, nv), 0)),
                pl.BlockSpec((1, 1, 2 * tf), lambda i, j, be, nv: (be[ic(i, nv)], 0, jc(i, j, nv))),
                pl.BlockSpec((1, 1, d), lambda i, j, be, nv: (be[ic(i, nv)], 0, 0)),
            ],
            out_specs=pl.BlockSpec((tm * PACK_SUB, LANES), lambda i, j, be, nv: (ic(i, nv), 0)),
            scratch_shapes=[pltpu.VMEM((tm, d), BF16), pltpu.VMEM((tm, d), F32)],
        ),
        out_shape=jax.ShapeDtypeStruct((n_rows * PACK_SUB, LANES), jnp.uint32),
        compiler_params=_cparams(("arbitrary", "arbitrary")),
        name="expert_ffn",
    )(block_e, n_valid, xs, w_gu, w_d, b_gu.reshape(n_e, 1, 2 * f), b_d.reshape(n_e, 1, d))


COMBINE_ROWS = 128


def _combine_kernel(pos_ref, posn_ref, yb_hbm, tg_ref, h1_ref, p_ref, wpg_ref, wpp_ref, gp_ref, gpp_ref,
                    gfin_ref, o_ref, gbuf_a, gbuf_b, sem):
    step = pl.program_id(0)
    last = pl.num_programs(0) - 1
    tc = COMBINE_ROWS
    n_asg = tc * TOP_K

    def issue(p_ref, base, gbuf, s):
        for a in range(n_asg):
            pltpu.make_async_copy(_row_tile(yb_hbm, p_ref[base + a]), _row_tile(gbuf.at[a % TOP_K], a // TOP_K),
                                  sem.at[s]).start()

    def wait_all(gbuf, s):
        for kk in range(TOP_K):
            pltpu.make_async_copy(_row_tile(yb_hbm, 0, tc), gbuf.at[kk], sem.at[s]).wait()

    def tail(gbuf, rows):
        tg = tg_ref[rows, :]
        lo_parts, hi_parts = [], []
        for c in range(PACK_SUB):
            lo_sum = hi_sum = None
            for kk in range(TOP_K):
                lo, hi = _unpack_chunk(gbuf[kk, pl.ds(c, tc, stride=PACK_SUB), :])
                gk = tg[:, kk:kk + 1]
                lo_sum = gk * lo if lo_sum is None else lo_sum + gk * lo
                hi_sum = gk * hi if hi_sum is None else hi_sum + gk * hi
            lo_parts.append(lo_sum)
            hi_parts.append(hi_sum)
        moe = jnp.concatenate(lo_parts + hi_parts, axis=1)
        h2 = h1_ref[rows, :] + moe
        gate = _sigmoid(_bdot(_rms(h2, gp_ref[...]), wpg_ref[...]))
        pe = _rms(_bdot(p_ref[rows, :], wpp_ref[...]), gpp_ref[...])
        h3 = h2 + gate * pe
        o_ref[rows, :] = _rms(h3, gfin_ref[...])

    @pl.when(step == 0)
    def _():
        issue(pos_ref, 0, gbuf_a, 0)

    wait_all(gbuf_a, 0)
    issue(pos_ref, n_asg, gbuf_b, 1)
    tail(gbuf_a, slice(0, tc))
    wait_all(gbuf_b, 1)
    issue(posn_ref, 0, gbuf_a, 0)
    tail(gbuf_b, slice(tc, 2 * tc))

    @pl.when(step == last)
    def _():
        wait_all(gbuf_a, 0)


def _combine(pos_flat, yb, tg, h1, p2, w_pg, w_pp, g_ple, g_ple_post, g_final, tc=COMBINE_ROWS):
    t, d = h1.shape
    pd = p2.shape[1]
    ts = 2 * tc
    n_steps = t // ts
    row = lambda w: pl.BlockSpec((ts, w), lambda i: (i, 0))
    const = lambda shp: pl.BlockSpec(shp, lambda i: (0, 0))
    return pl.pallas_call(
        _combine_kernel,
        grid=(n_steps,),
        in_specs=[
            pl.BlockSpec((ts * TOP_K,), lambda i: (i,), memory_space=pltpu.SMEM),
            pl.BlockSpec((ts * TOP_K,), lambda i: (jnp.minimum(i + 1, n_steps - 1),), memory_space=pltpu.SMEM),
            pl.BlockSpec(memory_space=pl.ANY),
            row(LANES), row(d), row(pd),
            const((d, d)), const((pd, d)), const((1, d)), const((1, d)), const((1, d)),
        ],
        out_specs=row(d),
        out_shape=jax.ShapeDtypeStruct((t, d), F32),
        scratch_shapes=[pltpu.VMEM((TOP_K, tc * PACK_SUB, LANES), jnp.uint32),
                        pltpu.VMEM((TOP_K, tc * PACK_SUB, LANES), jnp.uint32),
                        pltpu.SemaphoreType.DMA((2,))],
        compiler_params=_cparams(("arbitrary",)),
        name="combine_ple",
    )(pos_flat, pos_flat, yb, tg, h1, p2, w_pg, w_pp, g_ple.reshape(1, d), g_ple_post.reshape(1, d),
      g_final.reshape(1, d))


def _routing_tables(top_i, n_tok, tm=FFN_ROWS):
    n_asg = n_tok * TOP_K
    e_flat = top_i.reshape(n_asg)
    onehot = (e_flat[:, None] == jnp.arange(N_EXPERTS, dtype=jnp.int32)[None, :]).astype(jnp.int32)
    grp = onehot.reshape(n_asg // LANES, LANES, N_EXPERTS).astype(F32)
    within = jnp.einsum("ij,gjk->gik", jnp.tril(jnp.ones((LANES, LANES), F32)), grp).astype(jnp.int32)
    tot = within[:, -1, :]
    csum = (within + (jnp.cumsum(tot, axis=0) - tot)[:, None, :]).reshape(n_asg, N_EXPERTS)
    counts = csum[-1]
    padded = (counts + tm - 1) // tm * tm
    pend = jnp.cumsum(padded)
    pstart = pend - padded
    pos = jnp.sum(onehot * (csum - 1 + pstart[None, :]), axis=1).astype(jnp.int32)
    n_blocks = (n_tok * TOP_K) // tm + N_EXPERTS
    block_start = jnp.arange(n_blocks, dtype=jnp.int32) * tm
    block_e = jnp.minimum(jnp.searchsorted(pend, block_start, side="right"), N_EXPERTS - 1).astype(jnp.int32)
    n_valid = (pend[-1] // tm).astype(jnp.int32).reshape(1)
    return pos, pend.astype(jnp.int32), (padded - counts).astype(jnp.int32), block_e, n_valid, n_blocks * tm


def kernel(x, p, g_mix, w_in, conv_w, a_log, dt_bias, gdn_norm, ml_b_i, ml_b_f, ml_norm, w_out, g_ffn, w_router, b_router, w_gu, b_gu, w_down, b_down, g_ple, w_ple_gate, w_ple_proj, g_ple_post, g_final):
    b, s, d = x.shape
    n_tok = b * s
    assert w_in.shape[0] == 1, "single-layer block: the final norm is fused into the layer's last kernel"
    l = 0
    x2 = x.reshape(n_tok, d)
    w_main, w_gate = _prep_in_weights(w_in[l])
    z_main, z_gate = _in_proj(x2, g_mix[l], w_main, w_gate)
    z_main = z_main.reshape(b, s, Z_MAIN)
    z_gate = z_gate.reshape(b, s, LANES)
    y_gdn = _gdn(z_main, z_gate, conv_w[l], a_log[l], dt_bias[l], gdn_norm[l])
    y_ml = _mlstm(z_main, z_gate, ml_b_i[l], ml_b_f[l], ml_norm[l])

    w_r = jnp.pad(w_router[l], ((0, 0), (0, LANES - N_EXPERTS)))
    b_r = jnp.pad(b_router[l], (0, LANES - N_EXPERTS)).reshape(1, LANES)
    h1, u2, top_i, top_g = _out_proj(x2, y_gdn.reshape(n_tok, GDN_V), y_ml.reshape(n_tok, ML_V),
                                     w_out[l].astype(BF16), g_ffn[l], w_r, b_r)

    pos, pend, pad, block_e, n_valid, n_rows = _routing_tables(top_i[:, :TOP_K], n_tok)
    xs = _dispatch(u2, pos, pend, pad, n_rows)
    yb = _ffn(xs, block_e, n_valid, w_gu[l].astype(BF16), w_down[l].astype(BF16), b_gu[l], b_down[l])
    out = _combine(pos, yb, top_g, h1, p[l].reshape(n_tok, -1), w_ple_gate[l].astype(BF16),
                   w_ple_proj[l].astype(BF16), g_ple[l], g_ple_post[l], g_final)
    return out.reshape(b, s, d)
```

```python
import jax
import jax.numpy as jnp
from jax import lax
from jax.experimental import pallas as pl
from jax.experimental.pallas import tpu as pltpu

F32 = jnp.float32
BF16 = jnp.bfloat16

EPS = 1e-6
CHUNK = 64
GDN_HEADS = 8
GDN_DK = 128
GDN_DV = 128
GDN_CONV = 4
ML_HEADS = 4
ML_DQK = 128
ML_DV = 256
GATE_SOFTCAP = 15.0
N_EXPERTS = 32
TOP_K = 4
SWIGLU_LIMIT = 7.0
SWIGLU_ALPHA = 1.702

LANES = 128
NEG_BIG = -1e30

GDN_QK = GDN_HEADS * GDN_DK
GDN_V = GDN_HEADS * GDN_DV
ML_QK = ML_HEADS * ML_DQK
ML_V = ML_HEADS * ML_DV
OFF_Q = 0
OFF_K = OFF_Q + GDN_QK
OFF_V = OFF_K + GDN_QK
OFF_GZ = OFF_V + GDN_V
OFF_MQ = OFF_GZ + GDN_V
OFF_MK = OFF_MQ + ML_QK
OFF_MV = OFF_MK + ML_QK
OFF_MO = OFF_MV + ML_V
Z_MAIN = OFF_MO + ML_V
GATE_A = 0
GATE_B = GATE_A + GDN_HEADS
GATE_I = GATE_B + GDN_HEADS
GATE_F = GATE_I + ML_HEADS

TIME_BLOCK = 256
GDN_HEADS_PER_STEP = 4
ML_HEADS_PER_STEP = 4
VMEM_LIMIT = 56 * 1024 * 1024


def _cparams(sem):
    return pltpu.CompilerParams(dimension_semantics=sem, vmem_limit_bytes=VMEM_LIMIT)


def _bdot(a, b):
    return jnp.dot(a.astype(BF16), b.astype(BF16), preferred_element_type=F32)


def _bdot_nt(a, b):
    return lax.dot_general(a.astype(BF16), b.astype(BF16), (((1,), (1,)), ((), ())),
                           preferred_element_type=F32)


def _bdot_tn(a, b):
    return lax.dot_general(a.astype(BF16), b.astype(BF16), (((0,), (0,)), ((), ())),
                           preferred_element_type=F32)


def _split3(x):
    hi = x.astype(BF16)
    r1 = x - hi.astype(F32)
    mid = r1.astype(BF16)
    lo = (r1 - mid.astype(F32)).astype(BF16)
    return hi, mid, lo


def _dot_sel(sel_bf16, x):
    hi, mid, lo = _split3(x)
    d = lambda t: jnp.dot(sel_bf16, t, preferred_element_type=F32)
    return d(hi) + (d(mid) + d(lo))


def _sigmoid(x):
    return 1.0 / (1.0 + jnp.exp(-x))


def _softplus(x):
    return jnp.maximum(x, 0.0) + jnp.log1p(jnp.exp(-jnp.abs(x)))


def _rms(x, g):
    return x * lax.rsqrt(jnp.mean(x * x, axis=-1, keepdims=True) + EPS) * g


def _lane_pick(zg, idx):
    lane = lax.broadcasted_iota(jnp.int32, zg.shape, 1)
    return jnp.sum(jnp.where(lane == idx, zg, 0.0), axis=1, keepdims=True)


def _chunk_masks(n):
    r = lax.broadcasted_iota(jnp.int32, (n, n), 0)
    c = lax.broadcasted_iota(jnp.int32, (n, n), 1)
    same = lambda s: (r >> s) == (c >> s)
    return r, c, same


def _chunk_last(gc):
    n = gc.shape[0] // CHUNK
    parts = [jnp.broadcast_to(gc[CHUNK * (i + 1) - 1:CHUNK * (i + 1), :], (CHUNK, gc.shape[1]))
             for i in range(n)]
    return jnp.concatenate(parts, axis=0)


def _inproj_kernel(x_ref, g_ref, w_ref, wg_ref, z_ref, zg_ref, u_scr):
    @pl.when(pl.program_id(1) == 0)
    def _():
        u = _rms(x_ref[...], g_ref[...]).astype(BF16)
        u_scr[...] = u
        zg_ref[...] = jnp.dot(u, wg_ref[...], preferred_element_type=F32)

    z_ref[...] = jnp.dot(u_scr[...], w_ref[...], preferred_element_type=F32).astype(z_ref.dtype)


def _in_proj(x2, g_mix, w_main, w_gate, tm=1024, tn=1024):
    t, d = x2.shape
    n = w_main.shape[1]
    return pl.pallas_call(
        _inproj_kernel,
        grid=(t // tm, n // tn),
        in_specs=[
            pl.BlockSpec((tm, d), lambda i, j: (i, 0)),
            pl.BlockSpec((1, d), lambda i, j: (0, 0)),
            pl.BlockSpec((d, tn), lambda i, j: (0, j)),
            pl.BlockSpec((d, LANES), lambda i, j: (0, 0)),
        ],
        out_specs=[
            pl.BlockSpec((tm, tn), lambda i, j: (i, j)),
            pl.BlockSpec((tm, LANES), lambda i, j: (i, 0)),
        ],
        out_shape=[jax.ShapeDtypeStruct((t, n), BF16), jax.ShapeDtypeStruct((t, LANES), F32)],
        scratch_shapes=[pltpu.VMEM((tm, d), BF16)],
        compiler_params=_cparams(("parallel", "arbitrary")),
        name="in_proj",
    )(x2, g_mix.reshape(1, d), w_main, w_gate)


def _prep_in_weights(w_in):
    splits = (GDN_QK, GDN_QK, GDN_V, GDN_HEADS, GDN_HEADS, GDN_V, ML_QK, ML_QK, ML_V, ML_HEADS, ML_HEADS, ML_V)
    offs = [0]
    for wd in splits:
        offs.append(offs[-1] + wd)
    part = lambda i: w_in[:, offs[i]:offs[i + 1]]
    w_main = jnp.concatenate([part(i) for i in (0, 1, 2, 5, 6, 7, 8, 11)], axis=1).astype(BF16)
    gates = jnp.concatenate([part(i) for i in (3, 4, 9, 10)], axis=1)
    w_gate = jnp.pad(gates, ((0, 0), (0, LANES - gates.shape[1]))).astype(BF16)
    return w_main, w_gate


def _tri_inverse_minus_eye(ms, same):
    m16 = same(4)
    m32 = same(5)
    n32_mask = jnp.logical_and(m32, jnp.logical_not(m16))
    a = [jnp.where(m16, -m, 0.0) for m in ms]
    acc = list(a)
    for _ in range(3):
        a = [_bdot(x, x) for x in a]
        acc = [p + x + _bdot(p, x) for p, x in zip(acc, a)]
    for level in range(2):
        ns = [jnp.where(n32_mask, m, 0.0) if level == 0 else jnp.where(m32, 0.0, m) for m in ms]
        ys = [n + _bdot(p, n) for p, n in zip(acc, ns)]
        acc = [p - (y + _bdot(y, p)) for p, y in zip(acc, ys)]
    return acc


def _gdn_kernel(alog_ref, dtb_ref, zq_ref, zk_ref, zv_ref, gz_ref, zg_ref, cwq_ref, cwk_ref, cwv_ref,
                gn_ref, o_ref, s_scr, xq_scr, xk_scr, xv_scr):
    hb = s_scr.shape[0]
    h0 = pl.program_id(1) * hb
    tb = zq_ref.shape[1]
    nchunk = tb // CHUNK
    heads = range(hb)
    lanes = lambda hh: slice(LANES * hh, LANES * (hh + 1))

    @pl.when(pl.program_id(2) == 0)
    def _():
        s_scr[...] = jnp.zeros_like(s_scr)
        for scr in (xq_scr, xk_scr, xv_scr):
            scr[0:8, :] = jnp.zeros((8, scr.shape[1]), F32)

    def conv_silu(z_ref, x_scr, cw_ref):
        x = z_ref[0].astype(F32)
        x_scr[8:8 + tb, :] = x
        w = cw_ref[...]
        acc = x * w[GDN_CONV - 1:GDN_CONV, :]
        for s in range(1, GDN_CONV):
            acc = acc + x_scr[pl.ds(8 - s, tb), :] * w[GDN_CONV - 1 - s:GDN_CONV - s, :]
        x_scr[0:8, :] = x[tb - 8:tb, :]
        return acc * _sigmoid(acc)

    q_all = conv_silu(zq_ref, xq_scr, cwq_ref)
    k_all = conv_silu(zk_ref, xk_scr, cwk_ref)
    v_all = conv_silu(zv_ref, xv_scr, cwv_ref)
    gz_all = gz_ref[0].astype(F32)
    zg = zg_ref[0]

    r, c, same = _chunk_masks(tb)
    in_chunk = same(6)
    tril = jnp.logical_and(in_chunk, r >= c)
    strict = jnp.logical_and(in_chunk, r > c)
    tril_bf = tril.astype(BF16)

    def head_prep(hh):
        q = q_all[:, lanes(hh)]
        k = k_all[:, lanes(hh)]
        v = v_all[:, lanes(hh)]
        q = q * lax.rsqrt(jnp.sum(q * q, axis=-1, keepdims=True) + EPS) * (GDN_DK ** -0.5)
        k = k * lax.rsqrt(jnp.sum(k * k, axis=-1, keepdims=True) + EPS)
        beta = _sigmoid(_lane_pick(zg, GATE_B + h0 + hh))
        a_coef = jnp.exp(jnp.full((1, 1), alog_ref[h0 + hh], F32))
        g = -a_coef * _softplus(_lane_pick(zg, GATE_A + h0 + hh) + dtb_ref[h0 + hh])
        gcb = _dot_sel(tril_bf, jnp.broadcast_to(g, (tb, LANES)))
        gc_row = gcb.T[0:1, :]
        gc_col = jnp.concatenate([gcb] * (tb // LANES), axis=1)
        decay = jnp.exp(jnp.where(tril, gc_col - gc_row, NEG_BIG))
        eg = jnp.exp(gcb)
        kb = k * beta
        m_low = jnp.where(strict, _bdot_nt(kb, k) * decay, 0.0)
        rhs = jnp.concatenate([v * beta, kb * eg], axis=1)
        attn = _bdot_nt(q, k) * decay
        kd = k * jnp.exp(_chunk_last(gcb) - gcb)
        return dict(m_low=m_low, rhs=rhs, attn=attn, qd=q * eg, kd=kd, gcb=gcb)

    hp = [head_prep(hh) for hh in heads]
    t_m1 = _tri_inverse_minus_eye([p["m_low"] for p in hp], same)
    uws = [p["rhs"] + _bdot(t, p["rhs"]) for p, t in zip(hp, t_m1)]

    states = [s_scr[hh] for hh in heads]
    outs = [[] for _ in heads]
    for i in range(nchunk):
        sl = slice(CHUNK * i, CHUNK * (i + 1))
        for hh in heads:
            p, uw = hp[hh], uws[hh]
            res1 = _bdot(jnp.concatenate([uw[sl, GDN_DV:], p["qd"][sl]], axis=0), states[hh])
            v_new = uw[sl, :GDN_DV] - res1[:CHUNK]
            o_intra = _bdot(p["attn"][sl, CHUNK * i:CHUNK * (i + 1)], v_new)
            outs[hh].append(res1[CHUNK:] + o_intra)
            g_last = jnp.exp(p["gcb"][CHUNK * (i + 1) - 1:CHUNK * (i + 1), :])
            states[hh] = states[hh] * g_last[:, 0:1] + _bdot_tn(p["kd"][sl], v_new)

    for hh in heads:
        s_scr[hh] = states[hh]
        o = _rms(jnp.concatenate(outs[hh], axis=0), gn_ref[...])
        gz = gz_all[:, lanes(hh)]
        o_ref[0, :, lanes(hh)] = (o * (gz * _sigmoid(gz))).astype(o_ref.dtype)


def _gdn(z_main, z_gate, conv_w, a_log, dt_bias, gdn_norm, tb=TIME_BLOCK, hb=GDN_HEADS_PER_STEP):
    b, s, _ = z_main.shape
    wid = hb * LANES
    hq, hk, hv, hz = OFF_Q // wid, OFF_K // wid, OFF_V // wid, OFF_GZ // wid
    zspec = lambda off: pl.BlockSpec((1, tb, wid), lambda bi, hi, ti: (bi, ti, off + hi))
    cspec = lambda off: pl.BlockSpec((GDN_CONV, wid), lambda bi, hi, ti: (0, off + hi))
    smem = pl.BlockSpec(memory_space=pltpu.SMEM)
    return pl.pallas_call(
        _gdn_kernel,
        grid=(b, GDN_HEADS // hb, s // tb),
        in_specs=[
            smem, smem,
            zspec(hq), zspec(hk), zspec(hv), zspec(hz),
            pl.BlockSpec((1, tb, LANES), lambda bi, hi, ti: (bi, ti, 0)),
            cspec(hq), cspec(hk), cspec(hv),
            pl.BlockSpec((1, GDN_DV), lambda bi, hi, ti: (0, 0)),
        ],
        out_specs=pl.BlockSpec((1, tb, wid), lambda bi, hi, ti: (bi, ti, hi)),
        out_shape=jax.ShapeDtypeStruct((b, s, GDN_V), BF16),
        scratch_shapes=[
            pltpu.VMEM((hb, GDN_DK, GDN_DV), F32),
            pltpu.VMEM((8 + tb, wid), F32),
            pltpu.VMEM((8 + tb, wid), F32),
            pltpu.VMEM((8 + tb, wid), F32),
        ],
        compiler_params=_cparams(("parallel", "parallel", "arbitrary")),
        name="gdn",
    )(a_log, dt_bias, z_main, z_main, z_main, z_main, z_gate, conv_w, conv_w, conv_w,
      gdn_norm.reshape(1, GDN_DV))


def _mlstm_kernel(bi_ref, bf_ref, q_ref, k_ref, v_ref, og_ref, zg_ref, nrm_ref, o_ref, c_scr, m_scr):
    hb = c_scr.shape[0]
    h0 = pl.program_id(1) * hb
    tb = q_ref.shape[1]
    nchunk = tb // CHUNK
    cap = GATE_SOFTCAP

    @pl.when(pl.program_id(2) == 0)
    def _():
        c_scr[...] = jnp.zeros_like(c_scr)
        m_scr[...] = jnp.zeros_like(m_scr)

    zg = zg_ref[0]
    r, c, same = _chunk_masks(tb)
    tril = jnp.logical_and(same(6), r >= c)
    tril_bf = tril.astype(BF16)
    ones = jnp.ones((tb, LANES), F32)

    for hh in range(hb):
        q = q_ref[0, :, ML_DQK * hh:ML_DQK * (hh + 1)].astype(F32)
        k = k_ref[0, :, ML_DQK * hh:ML_DQK * (hh + 1)].astype(F32) * (ML_DQK ** -0.5)
        v = v_ref[0, :, ML_DV * hh:ML_DV * (hh + 1)].astype(F32)
        i_pre = cap * jnp.tanh((_lane_pick(zg, GATE_I + h0 + hh) + bi_ref[h0 + hh]) / cap)
        log_f = -_softplus(-(cap * jnp.tanh((_lane_pick(zg, GATE_F + h0 + hh) + bf_ref[h0 + hh]) / cap)))

        fcb = _dot_sel(tril_bf, jnp.broadcast_to(log_f, (tb, LANES)))
        fmi = fcb - i_pre
        fmi_row = fmi.T[0:1, :]
        fc_col = jnp.concatenate([fcb] * (tb // LANES), axis=1)
        d_mat = jnp.where(tril, fc_col - fmi_row, NEG_BIG)
        d_max = jnp.max(d_mat, axis=-1, keepdims=True)
        a_end = _chunk_last(fcb) - fmi

        m_st = m_scr[hh]
        m_rows, decs, wks = [], [], []
        for i in range(nchunk):
            sl = slice(CHUNK * i, CHUNK * (i + 1))
            f_last = fcb[CHUNK * (i + 1) - 1:CHUNK * (i + 1), :]
            a_max = jnp.max(a_end[sl], axis=0, keepdims=True)
            m_new = jnp.maximum(f_last + m_st, a_max)
            m_rows.append(jnp.broadcast_to(m_st, (CHUNK, LANES)))
            decs.append(jnp.exp(f_last + m_st - m_new))
            wks.append(jnp.exp(a_end[sl] - m_new))
            m_st = m_new
        m_scr[hh] = m_st

        inter = fcb + jnp.concatenate(m_rows, axis=0)
        m_i = jnp.maximum(d_max, inter)
        s_inter = jnp.exp(inter - m_i)
        wts = jnp.exp(d_mat - m_i[:, 0:1]) * _bdot_nt(q, k)
        v_aug = jnp.concatenate([v, ones], axis=1)
        intra = _bdot(wts, v_aug)

        cst = c_scr[hh]
        nums = []
        for i in range(nchunk):
            sl = slice(CHUNK * i, CHUNK * (i + 1))
            nums.append(s_inter[sl, 0:1] * _bdot(q[sl], cst) + intra[sl])
            cst = decs[i][:, 0:1] * cst + _bdot_tn(wks[i] * k[sl], v_aug[sl])
        c_scr[hh] = cst

        num_aug = jnp.concatenate(nums, axis=0)
        den = num_aug[:, ML_DV:ML_DV + 1]
        hout = num_aug[:, :ML_DV] / jnp.maximum(jnp.abs(den), jnp.exp(-m_i[:, 0:1]))
        hout = _rms(hout, nrm_ref[hh])
        og = og_ref[0, :, ML_DV * hh:ML_DV * (hh + 1)].astype(F32)
        o_ref[0, :, ML_DV * hh:ML_DV * (hh + 1)] = (hout * _sigmoid(og)).astype(o_ref.dtype)


def _mlstm(z_main, z_gate, ml_b_i, ml_b_f, ml_norm, tb=TIME_BLOCK, hb=ML_HEADS_PER_STEP):
    b, s, _ = z_main.shape
    qw, vw = hb * ML_DQK, hb * ML_DV
    hq, hk = OFF_MQ // qw, OFF_MK // qw
    hv, ho = OFF_MV // vw, OFF_MO // vw
    qspec = lambda off: pl.BlockSpec((1, tb, qw), lambda bi, hi, ti: (bi, ti, off + hi))
    vspec = lambda off: pl.BlockSpec((1, tb, vw), lambda bi, hi, ti: (bi, ti, off + hi))
    smem = pl.BlockSpec(memory_space=pltpu.SMEM)
    return pl.pallas_call(
        _mlstm_kernel,
        grid=(b, ML_HEADS // hb, s // tb),
        in_specs=[
            smem, smem,
            qspec(hq), qspec(hk), vspec(hv), vspec(ho),
            pl.BlockSpec((1, tb, LANES), lambda bi, hi, ti: (bi, ti, 0)),
            pl.BlockSpec((hb, 1, ML_DV), lambda bi, hi, ti: (hi, 0, 0)),
        ],
        out_specs=pl.BlockSpec((1, tb, vw), lambda bi, hi, ti: (bi, ti, hi)),
        out_shape=jax.ShapeDtypeStruct((b, s, ML_V), BF16),
        scratch_shapes=[
            pltpu.VMEM((hb, ML_DQK, ML_DV + LANES), F32),
            pltpu.VMEM((hb, 1, LANES), F32),
        ],
        compiler_params=_cparams(("parallel", "parallel", "arbitrary")),
        name="mlstm",
    )(ml_b_i, ml_b_f, z_main, z_main, z_main, z_main, z_gate, ml_norm.reshape(ML_HEADS, 1, ML_DV))


PACK_SUB = 8


def _pack_rows(x, o_ref):
    n, d = x.shape
    half = d // 2
    for j in range(PACK_SUB):
        lo = x[:, LANES * j:LANES * (j + 1)].astype(BF16).astype(F32)
        hi = x[:, half + LANES * j:half + LANES * (j + 1)].astype(BF16).astype(F32)
        word = (lax.bitcast_convert_type(lo, jnp.uint32) >> 16) | lax.bitcast_convert_type(hi, jnp.uint32)
        o_ref[pl.ds(j, n, stride=PACK_SUB), :] = word


def _unpack_chunk(word):
    lo = lax.bitcast_convert_type(word << 16, F32)
    hi = lax.bitcast_convert_type(word & jnp.uint32(0xFFFF0000), F32)
    return lo, hi


def _row_tile(ref, row, n=1):
    start = row * PACK_SUB
    if not isinstance(start, int):
        start = pl.multiple_of(start, PACK_SUB)
    return ref.at[pl.ds(start, n * PACK_SUB), :]


def _outproj_kernel(x_ref, yg_ref, ym_ref, wo1_ref, wo2_ref, gf_ref, wr_ref, br_ref,
                    h_ref, u_ref, ti_ref, tg_ref):
    h1 = (x_ref[...] + jnp.dot(yg_ref[...], wo1_ref[...], preferred_element_type=F32)
          + jnp.dot(ym_ref[...], wo2_ref[...], preferred_element_type=F32))
    h_ref[...] = h1
    u = _rms(h1, gf_ref[...])
    _pack_rows(u, u_ref)

    u_hi = u.astype(BF16)
    u_lo = (u - u_hi.astype(F32)).astype(BF16)
    wr = wr_ref[...]
    w_hi = wr.astype(BF16)
    w_lo = (wr - w_hi.astype(F32)).astype(BF16)
    d = lambda a, b: jnp.dot(a, b, preferred_element_type=F32)
    logits = d(u_hi, w_hi) + (d(u_hi, w_lo) + d(u_lo, w_hi)) + br_ref[...]

    lane = lax.broadcasted_iota(jnp.int32, logits.shape, 1)
    lg = jnp.where(lane < N_EXPERTS, logits, NEG_BIG)
    vals, idxs = [], []
    for _ in range(TOP_K):
        m = jnp.max(lg, axis=1, keepdims=True)
        idx = jnp.min(jnp.where(lg == m, lane, LANES), axis=1, keepdims=True)
        vals.append(m)
        idxs.append(idx)
        lg = jnp.where(lane == idx, NEG_BIG, lg)
    es = [jnp.exp(vv - vals[0]) for vv in vals]
    tot = es[0] + es[1] + es[2] + es[3]
    ti = jnp.zeros(logits.shape, jnp.int32)
    tg = jnp.zeros(logits.shape, F32)
    for kk in range(TOP_K):
        ti = jnp.where(lane == kk, idxs[kk], ti)
        tg = jnp.where(lane == kk, es[kk] / tot, tg)
    ti_ref[...] = ti
    tg_ref[...] = tg


def _out_proj(x2, y_gdn, y_ml, w_out_bf, g_ffn, w_router_pad, b_router_pad, tm=512):
    t, d = x2.shape
    row = lambda w: pl.BlockSpec((tm, w), lambda i: (i, 0))
    const = lambda shp: pl.BlockSpec(shp, lambda i: (0, 0))
    return pl.pallas_call(
        _outproj_kernel,
        grid=(t // tm,),
        in_specs=[
            row(d), row(GDN_V), row(ML_V),
            pl.BlockSpec((GDN_V, d), lambda i: (0, 0)),
            pl.BlockSpec((ML_V, d), lambda i: (GDN_V // ML_V, 0)),
            const((1, d)), const((d, LANES)), const((1, LANES)),
        ],
        out_specs=[row(d), pl.BlockSpec((tm * PACK_SUB, LANES), lambda i: (i, 0)), row(LANES), row(LANES)],
        out_shape=[jax.ShapeDtypeStruct((t, d), F32), jax.ShapeDtypeStruct((t * PACK_SUB, LANES), jnp.uint32),
                   jax.ShapeDtypeStruct((t, LANES), jnp.int32), jax.ShapeDtypeStruct((t, LANES), F32)],
        compiler_params=_cparams(("parallel",)),
        name="out_proj_router",
    )(x2, y_gdn, y_ml, w_out_bf, w_out_bf, g_ffn.reshape(1, d), w_router_pad, b_router_pad)


FFN_ROWS = 512
FILL_ROWS = 8
ISSUE_UNROLL = 4


def _dispatch_kernel(pend_ref, pad_ref, pos_ref, u_ref, xs_hbm, zero_scr, sem):
    step = pl.program_id(0)
    n_asg = pos_ref.shape[0]

    def fill_copy(row0):
        return pltpu.make_async_copy(zero_scr, _row_tile(xs_hbm, row0, FILL_ROWS), sem)

    @pl.when(step == 0)
    def _():
        zero_scr[...] = jnp.zeros_like(zero_scr)

        def per_expert(e, carry):
            n_fill = (pad_ref[e] + FILL_ROWS - 1) // FILL_ROWS

            def start(j, c):
                fill_copy(pend_ref[e] - (j + 1) * FILL_ROWS).start()
                return c

            def wait(j, c):
                fill_copy(0).wait()
                return c

            lax.fori_loop(0, n_fill, start, 0)
            lax.fori_loop(0, n_fill, wait, 0)
            return carry

        lax.fori_loop(0, N_EXPERTS, per_expert, 0)

    def start(tok, c):
        for kk in range(TOP_K):
            pltpu.make_async_copy(_row_tile(u_ref, tok), _row_tile(xs_hbm, pos_ref[tok * TOP_K + kk]), sem).start()
        return c

    lax.fori_loop(0, n_asg // TOP_K, start, 0, unroll=ISSUE_UNROLL)
    for _ in range(TOP_K):
        pltpu.make_async_copy(u_ref, _row_tile(xs_hbm, 0, n_asg // TOP_K), sem).wait()


def _dispatch(u_packed, pos_flat, pend, pad, n_rows, tt=512):
    t = u_packed.shape[0] // PACK_SUB
    return pl.pallas_call(
        _dispatch_kernel,
        grid_spec=pltpu.PrefetchScalarGridSpec(
            num_scalar_prefetch=2,
            grid=(t // tt,),
            in_specs=[
                pl.BlockSpec((tt * TOP_K,), lambda i, *_: (i,), memory_space=pltpu.SMEM),
                pl.BlockSpec((tt * PACK_SUB, LANES), lambda i, *_: (i, 0)),
            ],
            out_specs=pl.BlockSpec(memory_space=pl.ANY),
            scratch_shapes=[pltpu.VMEM((FILL_ROWS * PACK_SUB, LANES), jnp.uint32), pltpu.SemaphoreType.DMA(())],
        ),
        out_shape=jax.ShapeDtypeStruct((n_rows * PACK_SUB, LANES), jnp.uint32),
        compiler_params=_cparams(("arbitrary",)),
        name="dispatch",
    )(pend, pad, pos_flat, u_packed)


DEINT_GROUP = 2 * LANES


def _prep_kernel(wgu_ref, wd_ref, wg_o, wu_o, wd_o):
    x = wgu_ref[0].astype(BF16)
    r = lax.broadcasted_iota(jnp.int32, (DEINT_GROUP, LANES), 0)
    c = lax.broadcasted_iota(jnp.int32, (DEINT_GROUP, LANES), 1)
    sel_even = (r == 2 * c).astype(BF16)
    sel_odd = (r == 2 * c + 1).astype(BF16)
    for k in range(x.shape[1] // DEINT_GROUP):
        blk = x[:, DEINT_GROUP * k:DEINT_GROUP * (k + 1)]
        wg_o[0, :, LANES * k:LANES * (k + 1)] = jnp.dot(blk, sel_even, preferred_element_type=F32).astype(BF16)
        wu_o[0, :, LANES * k:LANES * (k + 1)] = jnp.dot(blk, sel_odd, preferred_element_type=F32).astype(BF16)
    wd_o[...] = wd_ref[...].astype(BF16)


def _prep_expert_weights(w_gu, w_d, tn=512):
    n_e, d, f2 = w_gu.shape
    f = f2 // 2
    return pl.pallas_call(
        _prep_kernel,
        grid=(n_e, f // tn),
        in_specs=[
            pl.BlockSpec((1, d, 2 * tn), lambda e, j: (e, 0, j)),
            pl.BlockSpec((1, tn, d), lambda e, j: (e, j, 0)),
        ],
        out_specs=[
            pl.BlockSpec((1, d, tn), lambda e, j: (e, 0, j)),
            pl.BlockSpec((1, d, tn), lambda e, j: (e, 0, j)),
            pl.BlockSpec((1, tn, d), lambda e, j: (e, j, 0)),
        ],
        out_shape=[jax.ShapeDtypeStruct((n_e, d, f), BF16), jax.ShapeDtypeStruct((n_e, d, f), BF16),
                   jax.ShapeDtypeStruct((n_e, f, d), BF16)],
        compiler_params=_cparams(("parallel", "parallel")),
        name="expert_weight_prep",
    )(w_gu, w_d)


def _ffn_kernel(be_ref, rows_ref, nv_ref, x_ref, wg_ref, wu_ref, wd_ref, bg_ref, bu_ref, bd_ref, o_ref,
                xb_scr, acc_scr):
    i = pl.program_id(0)
    j = pl.program_id(1)
    tm, d = xb_scr.shape
    half = d // 2

    def compute(n):
        xb = xb_scr[0:n, :]
        gate = jnp.minimum(jnp.dot(xb, wg_ref[0], preferred_element_type=F32) + bg_ref[0], SWIGLU_LIMIT)
        up = jnp.clip(jnp.dot(xb, wu_ref[0], preferred_element_type=F32) + bu_ref[0], -SWIGLU_LIMIT, SWIGLU_LIMIT)
        act = (gate * _sigmoid(gate * SWIGLU_ALPHA) * (up + 1.0)).astype(BF16)
        y = jnp.dot(act, wd_ref[0], preferred_element_type=F32)

        @pl.when(j == 0)
        def _():
            acc_scr[0:n, :] = y + bd_ref[0]
            if n < tm:
                acc_scr[n:tm, :] = jnp.zeros((tm - n, d), F32)

        @pl.when(j > 0)
        def _():
            acc_scr[0:n, :] += y

    @pl.when(i < nv_ref[0])
    def _():
        @pl.when(j == 0)
        def _():
            for c in range(PACK_SUB):
                lo, hi = _unpack_chunk(x_ref[pl.ds(c, tm, stride=PACK_SUB), :])
                xb_scr[:, LANES * c:LANES * (c + 1)] = lo.astype(BF16)
                xb_scr[:, half + LANES * c:half + LANES * (c + 1)] = hi.astype(BF16)

        full = rows_ref[i] > tm // 2

        @pl.when(full)
        def _():
            compute(tm)

        @pl.when(jnp.logical_not(full))
        def _():
            compute(tm // 2)

        @pl.when(j == pl.num_programs(1) - 1)
        def _():
            _pack_rows(acc_scr[...], o_ref)


def _ffn(xs, block_e, block_rows, n_valid, w_g, w_u, w_d, b_g, b_u, b_d, tm=FFN_ROWS, tf=1024):
    n_rows = xs.shape[0] // PACK_SUB
    n_e, f, d = w_d.shape
    nf = f // tf
    n_blocks = n_rows // tm
    ic = lambda i, nv: jnp.minimum(i, nv[0] - 1)
    jc = lambda i, j, nv: jnp.where(i < nv[0], j, nf - 1)
    col = lambda i, j, be, br, nv: (be[ic(i, nv)], 0, jc(i, j, nv))
    return pl.pallas_call(
        _ffn_kernel,
        grid_spec=pltpu.PrefetchScalarGridSpec(
            num_scalar_prefetch=3,
            grid=(n_blocks, nf),
            in_specs=[
                pl.BlockSpec((tm * PACK_SUB, LANES), lambda i, j, be, br, nv: (ic(i, nv), 0)),
                pl.BlockSpec((1, d, tf), col),
                pl.BlockSpec((1, d, tf), col),
                pl.BlockSpec((1, tf, d), lambda i, j, be, br, nv: (be[ic(i, nv)], jc(i, j, nv), 0)),
                pl.BlockSpec((1, 1, tf), col),
                pl.BlockSpec((1, 1, tf), col),
                pl.BlockSpec((1, 1, d), lambda i, j, be, br, nv: (be[ic(i, nv)], 0, 0)),
            ],
            out_specs=pl.BlockSpec((tm * PACK_SUB, LANES), lambda i, j, be, br, nv: (ic(i, nv), 0)),
            scratch_shapes=[pltpu.VMEM((tm, d), BF16), pltpu.VMEM((tm, d), F32)],
        ),
        out_shape=jax.ShapeDtypeStruct((n_rows * PACK_SUB, LANES), jnp.uint32),
        compiler_params=_cparams(("arbitrary", "arbitrary")),
        name="expert_ffn",
    )(block_e, block_rows, n_valid, xs, w_g, w_u, w_d, b_g.reshape(n_e, 1, f), b_u.reshape(n_e, 1, f),
      b_d.reshape(n_e, 1, d))


COMBINE_ROWS = 128


def _combine_kernel(pos_ref, posn_ref, yb_hbm, tg_ref, h1_ref, p_ref, wpg_ref, wpp_ref, gp_ref, gpp_ref,
                    gfin_ref, o_ref, gbuf_a, gbuf_b, sem):
    step = pl.program_id(0)
    last = pl.num_programs(0) - 1
    tc = COMBINE_ROWS
    n_asg = tc * TOP_K

    def issue(p_ref, base, gbuf, s):
        for a in range(n_asg):
            pltpu.make_async_copy(_row_tile(yb_hbm, p_ref[base + a]), _row_tile(gbuf.at[a % TOP_K], a // TOP_K),
                                  sem.at[s]).start()

    def wait_all(gbuf, s):
        for kk in range(TOP_K):
            pltpu.make_async_copy(_row_tile(yb_hbm, 0, tc), gbuf.at[kk], sem.at[s]).wait()

    def tail(gbuf, rows):
        tg = tg_ref[rows, :]
        lo_parts, hi_parts = [], []
        for c in range(PACK_SUB):
            lo_sum = hi_sum = None
            for kk in range(TOP_K):
                lo, hi = _unpack_chunk(gbuf[kk, pl.ds(c, tc, stride=PACK_SUB), :])
                gk = tg[:, kk:kk + 1]
                lo_sum = gk * lo if lo_sum is None else lo_sum + gk * lo
                hi_sum = gk * hi if hi_sum is None else hi_sum + gk * hi
            lo_parts.append(lo_sum)
            hi_parts.append(hi_sum)
        moe = jnp.concatenate(lo_parts + hi_parts, axis=1)
        h2 = h1_ref[rows, :] + moe
        gate = _sigmoid(_bdot(_rms(h2, gp_ref[...]), wpg_ref[...]))
        pe = _rms(_bdot(p_ref[rows, :], wpp_ref[...]), gpp_ref[...])
        h3 = h2 + gate * pe
        o_ref[rows, :] = _rms(h3, gfin_ref[...])

    @pl.when(step == 0)
    def _():
        issue(pos_ref, 0, gbuf_a, 0)

    wait_all(gbuf_a, 0)
    issue(pos_ref, n_asg, gbuf_b, 1)
    tail(gbuf_a, slice(0, tc))
    wait_all(gbuf_b, 1)
    issue(posn_ref, 0, gbuf_a, 0)
    tail(gbuf_b, slice(tc, 2 * tc))

    @pl.when(step == last)
    def _():
        wait_all(gbuf_a, 0)


def _combine(pos_flat, yb, tg, h1, p2, w_pg, w_pp, g_ple, g_ple_post, g_final, tc=COMBINE_ROWS):
    t, d = h1.shape
    pd = p2.shape[1]
    ts = 2 * tc
    n_steps = t // ts
    row = lambda w: pl.BlockSpec((ts, w), lambda i: (i, 0))
    const = lambda shp: pl.BlockSpec(shp, lambda i: (0, 0))
    return pl.pallas_call(
        _combine_kernel,
        grid=(n_steps,),
        in_specs=[
            pl.BlockSpec((ts * TOP_K,), lambda i: (i,), memory_space=pltpu.SMEM),
            pl.BlockSpec((ts * TOP_K,), lambda i: (jnp.minimum(i + 1, n_steps - 1),), memory_space=pltpu.SMEM),
            pl.BlockSpec(memory_space=pl.ANY),
            row(LANES), row(d), row(pd),
            const((d, d)), const((pd, d)), const((1, d)), const((1, d)), const((1, d)),
        ],
        out_specs=row(d),
        out_shape=jax.ShapeDtypeStruct((t, d), F32),
        scratch_shapes=[pltpu.VMEM((TOP_K, tc * PACK_SUB, LANES), jnp.uint32),
                        pltpu.VMEM((TOP_K, tc * PACK_SUB, LANES), jnp.uint32),
                        pltpu.SemaphoreType.DMA((2,))],
        compiler_params=_cparams(("arbitrary",)),
        name="combine_ple",
    )(pos_flat, pos_flat, yb, tg, h1, p2, w_pg, w_pp, g_ple.reshape(1, d), g_ple_post.reshape(1, d),
      g_final.reshape(1, d))


def _routing_tables(top_i, n_tok, tm=FFN_ROWS):
    n_asg = n_tok * TOP_K
    e_flat = top_i.reshape(n_asg)
    onehot = (e_flat[:, None] == jnp.arange(N_EXPERTS, dtype=jnp.int32)[None, :]).astype(jnp.int32)
    grp = onehot.reshape(n_asg // LANES, LANES, N_EXPERTS).astype(F32)
    within = jnp.einsum("ij,gjk->gik", jnp.tril(jnp.ones((LANES, LANES), F32)), grp).astype(jnp.int32)
    tot = within[:, -1, :]
    csum = (within + (jnp.cumsum(tot, axis=0) - tot)[:, None, :]).reshape(n_asg, N_EXPERTS)
    counts = csum[-1]
    padded = (counts + tm - 1) // tm * tm
    pend = jnp.cumsum(padded)
    pstart = pend - padded
    pos = jnp.sum(onehot * (csum - 1 + pstart[None, :]), axis=1).astype(jnp.int32)
    n_blocks = (n_tok * TOP_K) // tm + N_EXPERTS
    block_start = jnp.arange(n_blocks, dtype=jnp.int32) * tm
    block_e = jnp.minimum(jnp.sum(pend[None, :] <= block_start[:, None], axis=1), N_EXPERTS - 1).astype(jnp.int32)
    block_rows = jnp.clip((pstart + counts)[block_e] - block_start, 0, tm).astype(jnp.int32)
    n_valid = (pend[-1] // tm).astype(jnp.int32).reshape(1)
    return (pos, pend.astype(jnp.int32), (padded - counts).astype(jnp.int32), block_e, block_rows, n_valid,
            n_blocks * tm)


def kernel(x, p, g_mix, w_in, conv_w, a_log, dt_bias, gdn_norm, ml_b_i, ml_b_f, ml_norm, w_out, g_ffn, w_router, b_router, w_gu, b_gu, w_down, b_down, g_ple, w_ple_gate, w_ple_proj, g_ple_post, g_final):
    b, s, d = x.shape
    n_tok = b * s
    assert w_in.shape[0] == 1, "single-layer block: the final norm is fused into the layer's last kernel"
    l = 0
    x2 = x.reshape(n_tok, d)
    w_main, w_gate = _prep_in_weights(w_in[l])
    z_main, z_gate = _in_proj(x2, g_mix[l], w_main, w_gate)
    z_main = z_main.reshape(b, s, Z_MAIN)
    z_gate = z_gate.reshape(b, s, LANES)
    y_gdn = _gdn(z_main, z_gate, conv_w[l], a_log[l], dt_bias[l], gdn_norm[l])
    y_ml = _mlstm(z_main, z_gate, ml_b_i[l], ml_b_f[l], ml_norm[l])

    w_r = jnp.pad(w_router[l], ((0, 0), (0, LANES - N_EXPERTS)))
    b_r = jnp.pad(b_router[l], (0, LANES - N_EXPERTS)).reshape(1, LANES)
    h1, u2, top_i, top_g = _out_proj(x2, y_gdn.reshape(n_tok, GDN_V), y_ml.reshape(n_tok, ML_V),
                                     w_out[l].astype(BF16), g_ffn[l], w_r, b_r)

    pos, pend, pad, block_e, block_rows, n_valid, n_rows = _routing_tables(top_i[:, :TOP_K], n_tok)
    xs = _dispatch(u2, pos, pend, pad, n_rows)
    w_g, w_u, w_d = _prep_expert_weights(w_gu[l], w_down[l])
    yb = _ffn(xs, block_e, block_rows, n_valid, w_g, w_u, w_d, b_gu[l][:, 0::2], b_gu[l][:, 1::2], b_down[l])
    out = _combine(pos, yb, top_g, h1, p[l].reshape(n_tok, -1), w_ple_gate[l].astype(BF16),
                   w_ple_proj[l].astype(BF16), g_ple[l], g_ple_post[l], g_final)
    return out.reshape(b, s, d)
```

```python
import jax
import jax.numpy as jnp
from jax import lax
from jax.experimental import pallas as pl
from jax.experimental.pallas import tpu as pltpu

F32 = jnp.float32
BF16 = jnp.bfloat16

EPS = 1e-6
CHUNK = 64
GDN_HEADS = 8
GDN_DK = 128
GDN_DV = 128
GDN_CONV = 4
ML_HEADS = 4
ML_DQK = 128
ML_DV = 256
GATE_SOFTCAP = 15.0
N_EXPERTS = 32
TOP_K = 4
SWIGLU_LIMIT = 7.0
SWIGLU_ALPHA = 1.702

LANES = 128
NEG_BIG = -1e30

GDN_QK = GDN_HEADS * GDN_DK
GDN_V = GDN_HEADS * GDN_DV
ML_QK = ML_HEADS * ML_DQK
ML_V = ML_HEADS * ML_DV
OFF_Q = 0
OFF_K = OFF_Q + GDN_QK
OFF_V = OFF_K + GDN_QK
OFF_GZ = OFF_V + GDN_V
OFF_MQ = OFF_GZ + GDN_V
OFF_MK = OFF_MQ + ML_QK
OFF_MV = OFF_MK + ML_QK
OFF_MO = OFF_MV + ML_V
Z_MAIN = OFF_MO + ML_V
GATE_A = 0
GATE_B = GATE_A + GDN_HEADS
GATE_I = GATE_B + GDN_HEADS
GATE_F = GATE_I + ML_HEADS

GDN_TIME_BLOCK = 128
ML_TIME_BLOCK = 256
GDN_HEADS_PER_STEP = 8
ML_HEADS_PER_STEP = 4
VMEM_LIMIT = 56 * 1024 * 1024


def _cparams(sem):
    return pltpu.CompilerParams(dimension_semantics=sem, vmem_limit_bytes=VMEM_LIMIT)


def _bdot(a, b):
    return jnp.dot(a.astype(BF16), b.astype(BF16), preferred_element_type=F32)


def _bdot_nt(a, b):
    return lax.dot_general(a.astype(BF16), b.astype(BF16), (((1,), (1,)), ((), ())),
                           preferred_element_type=F32)


def _bdot_tn(a, b):
    return lax.dot_general(a.astype(BF16), b.astype(BF16), (((0,), (0,)), ((), ())),
                           preferred_element_type=F32)


def _split3(x):
    hi = x.astype(BF16)
    r1 = x - hi.astype(F32)
    mid = r1.astype(BF16)
    lo = (r1 - mid.astype(F32)).astype(BF16)
    return hi, mid, lo


def _dot_sel(sel_bf16, x):
    hi, mid, lo = _split3(x)
    d = lambda t: jnp.dot(sel_bf16, t, preferred_element_type=F32)
    return d(hi) + (d(mid) + d(lo))


def _sigmoid(x):
    return 1.0 / (1.0 + jnp.exp(-x))


def _softplus(x):
    return jnp.maximum(x, 0.0) + jnp.log1p(jnp.exp(-jnp.abs(x)))


def _rms(x, g):
    return x * lax.rsqrt(jnp.mean(x * x, axis=-1, keepdims=True) + EPS) * g


def _lane_pick(zg, idx):
    lane = lax.broadcasted_iota(jnp.int32, zg.shape, 1)
    return jnp.sum(jnp.where(lane == idx, zg, 0.0), axis=1, keepdims=True)


def _chunk_masks(n):
    r = lax.broadcasted_iota(jnp.int32, (n, n), 0)
    c = lax.broadcasted_iota(jnp.int32, (n, n), 1)
    same = lambda s: (r >> s) == (c >> s)
    return r, c, same


def _chunk_last(gc):
    n = gc.shape[0] // CHUNK
    parts = [jnp.broadcast_to(gc[CHUNK * (i + 1) - 1:CHUNK * (i + 1), :], (CHUNK, gc.shape[1]))
             for i in range(n)]
    return jnp.concatenate(parts, axis=0)


def _inproj_kernel(x_ref, g_ref, w_ref, wg_ref, z_ref, zg_ref, u_scr):
    @pl.when(pl.program_id(1) == 0)
    def _():
        u = _rms(x_ref[...], g_ref[...]).astype(BF16)
        u_scr[...] = u
        zg_ref[...] = jnp.dot(u, wg_ref[...], preferred_element_type=F32)

    z_ref[...] = jnp.dot(u_scr[...], w_ref[...], preferred_element_type=F32).astype(z_ref.dtype)


def _in_proj(x2, g_mix, w_main, w_gate, tm=1024, tn=1024):
    t, d = x2.shape
    n = w_main.shape[1]
    return pl.pallas_call(
        _inproj_kernel,
        grid=(t // tm, n // tn),
        in_specs=[
            pl.BlockSpec((tm, d), lambda i, j: (i, 0)),
            pl.BlockSpec((1, d), lambda i, j: (0, 0)),
            pl.BlockSpec((d, tn), lambda i, j: (0, j)),
            pl.BlockSpec((d, LANES), lambda i, j: (0, 0)),
        ],
        out_specs=[
            pl.BlockSpec((tm, tn), lambda i, j: (i, j)),
            pl.BlockSpec((tm, LANES), lambda i, j: (i, 0)),
        ],
        out_shape=[jax.ShapeDtypeStruct((t, n), BF16), jax.ShapeDtypeStruct((t, LANES), F32)],
        scratch_shapes=[pltpu.VMEM((tm, d), BF16)],
        compiler_params=_cparams(("parallel", "arbitrary")),
        name="in_proj",
    )(x2, g_mix.reshape(1, d), w_main, w_gate)


def _prep_in_weights(w_in):
    splits = (GDN_QK, GDN_QK, GDN_V, GDN_HEADS, GDN_HEADS, GDN_V, ML_QK, ML_QK, ML_V, ML_HEADS, ML_HEADS, ML_V)
    offs = [0]
    for wd in splits:
        offs.append(offs[-1] + wd)
    part = lambda i: w_in[:, offs[i]:offs[i + 1]]
    w_main = jnp.concatenate([part(i) for i in (0, 1, 2, 5, 6, 7, 8, 11)], axis=1).astype(BF16)
    gates = jnp.concatenate([part(i) for i in (3, 4, 9, 10)], axis=1)
    w_gate = jnp.pad(gates, ((0, 0), (0, LANES - gates.shape[1]))).astype(BF16)
    return w_main, w_gate


def _tri_inverse_minus_eye(ms, same):
    m16 = same(4)
    m32 = same(5)
    n32_mask = jnp.logical_and(m32, jnp.logical_not(m16))
    a = [jnp.where(m16, -m, 0.0) for m in ms]
    acc = list(a)
    for _ in range(3):
        a = [_bdot(x, x) for x in a]
        acc = [p + x + _bdot(p, x) for p, x in zip(acc, a)]
    for level in range(2):
        ns = [jnp.where(n32_mask, m, 0.0) if level == 0 else jnp.where(m32, 0.0, m) for m in ms]
        ys = [n + _bdot(p, n) for p, n in zip(acc, ns)]
        acc = [p - (y + _bdot(y, p)) for p, y in zip(acc, ys)]
    return acc


def _gdn_kernel(alog_ref, dtb_ref, zq_ref, zk_ref, zv_ref, gz_ref, zg_ref, cwq_ref, cwk_ref, cwv_ref,
                gn_ref, o_ref, s_scr, xq_scr, xk_scr, xv_scr):
    hb = s_scr.shape[0]
    h0 = pl.program_id(1) * hb
    tb = zq_ref.shape[1]
    nchunk = tb // CHUNK
    heads = range(hb)
    lanes = lambda hh: slice(LANES * hh, LANES * (hh + 1))

    @pl.when(pl.program_id(2) == 0)
    def _():
        s_scr[...] = jnp.zeros_like(s_scr)
        for scr in (xq_scr, xk_scr, xv_scr):
            scr[0:8, :] = jnp.zeros((8, scr.shape[1]), F32)

    def conv_silu(z_ref, x_scr, cw_ref):
        x = z_ref[0].astype(F32)
        x_scr[8:8 + tb, :] = x
        w = cw_ref[...]
        acc = x * w[GDN_CONV - 1:GDN_CONV, :]
        for s in range(1, GDN_CONV):
            acc = acc + x_scr[pl.ds(8 - s, tb), :] * w[GDN_CONV - 1 - s:GDN_CONV - s, :]
        x_scr[0:8, :] = x[tb - 8:tb, :]
        return acc * _sigmoid(acc)

    q_all = conv_silu(zq_ref, xq_scr, cwq_ref)
    k_all = conv_silu(zk_ref, xk_scr, cwk_ref)
    v_all = conv_silu(zv_ref, xv_scr, cwv_ref)
    gz_all = gz_ref[0].astype(F32)
    zg = zg_ref[0]

    r, c, same = _chunk_masks(tb)
    in_chunk = same(6)
    tril = jnp.logical_and(in_chunk, r >= c)
    strict = jnp.logical_and(in_chunk, r > c)
    tril_bf = tril.astype(BF16)

    def head_prep(hh):
        q = q_all[:, lanes(hh)]
        k = k_all[:, lanes(hh)]
        v = v_all[:, lanes(hh)]
        q = q * lax.rsqrt(jnp.sum(q * q, axis=-1, keepdims=True) + EPS) * (GDN_DK ** -0.5)
        k = k * lax.rsqrt(jnp.sum(k * k, axis=-1, keepdims=True) + EPS)
        beta = _sigmoid(_lane_pick(zg, GATE_B + h0 + hh))
        a_coef = jnp.exp(jnp.full((1, 1), alog_ref[h0 + hh], F32))
        g = -a_coef * _softplus(_lane_pick(zg, GATE_A + h0 + hh) + dtb_ref[h0 + hh])
        gcb = _dot_sel(tril_bf, jnp.broadcast_to(g, (tb, LANES)))
        gc_row = gcb.T[0:1, :]
        gc_col = jnp.concatenate([gcb] * (tb // LANES), axis=1)
        decay = jnp.exp(jnp.where(tril, gc_col - gc_row, NEG_BIG))
        eg = jnp.exp(gcb)
        kb = k * beta
        m_low = jnp.where(strict, _bdot_nt(kb, k) * decay, 0.0)
        rhs = jnp.concatenate([v * beta, kb * eg], axis=1)
        attn = _bdot_nt(q, k) * decay
        kd = k * jnp.exp(_chunk_last(gcb) - gcb)
        return dict(m_low=m_low, rhs=rhs, attn=attn, qd=q * eg, kd=kd, gcb=gcb)

    hp = [head_prep(hh) for hh in heads]
    t_m1 = _tri_inverse_minus_eye([p["m_low"] for p in hp], same)
    uws = [p["rhs"] + _bdot(t, p["rhs"]) for p, t in zip(hp, t_m1)]

    states = [s_scr[hh] for hh in heads]
    outs = [[] for _ in heads]
    for i in range(nchunk):
        sl = slice(CHUNK * i, CHUNK * (i + 1))
        for hh in heads:
            p, uw = hp[hh], uws[hh]
            res1 = _bdot(jnp.concatenate([uw[sl, GDN_DV:], p["qd"][sl]], axis=0), states[hh])
            v_new = uw[sl, :GDN_DV] - res1[:CHUNK]
            o_intra = _bdot(p["attn"][sl, CHUNK * i:CHUNK * (i + 1)], v_new)
            outs[hh].append(res1[CHUNK:] + o_intra)
            g_last = jnp.exp(p["gcb"][CHUNK * (i + 1) - 1:CHUNK * (i + 1), :])
            states[hh] = states[hh] * g_last[:, 0:1] + _bdot_tn(p["kd"][sl], v_new)

    for hh in heads:
        s_scr[hh] = states[hh]
        o = _rms(jnp.concatenate(outs[hh], axis=0), gn_ref[...])
        gz = gz_all[:, lanes(hh)]
        o_ref[0, :, lanes(hh)] = (o * (gz * _sigmoid(gz))).astype(o_ref.dtype)


def _gdn(z_main, z_gate, conv_w, a_log, dt_bias, gdn_norm, tb=GDN_TIME_BLOCK, hb=GDN_HEADS_PER_STEP):
    b, s, _ = z_main.shape
    wid = hb * LANES
    hq, hk, hv, hz = OFF_Q // wid, OFF_K // wid, OFF_V // wid, OFF_GZ // wid
    zspec = lambda off: pl.BlockSpec((1, tb, wid), lambda bi, hi, ti: (bi, ti, off + hi))
    cspec = lambda off: pl.BlockSpec((GDN_CONV, wid), lambda bi, hi, ti: (0, off + hi))
    smem = pl.BlockSpec(memory_space=pltpu.SMEM)
    return pl.pallas_call(
        _gdn_kernel,
        grid=(b, GDN_HEADS // hb, s // tb),
        in_specs=[
            smem, smem,
            zspec(hq), zspec(hk), zspec(hv), zspec(hz),
            pl.BlockSpec((1, tb, LANES), lambda bi, hi, ti: (bi, ti, 0)),
            cspec(hq), cspec(hk), cspec(hv),
            pl.BlockSpec((1, GDN_DV), lambda bi, hi, ti: (0, 0)),
        ],
        out_specs=pl.BlockSpec((1, tb, wid), lambda bi, hi, ti: (bi, ti, hi)),
        out_shape=jax.ShapeDtypeStruct((b, s, GDN_V), BF16),
        scratch_shapes=[
            pltpu.VMEM((hb, GDN_DK, GDN_DV), F32),
            pltpu.VMEM((8 + tb, wid), F32),
            pltpu.VMEM((8 + tb, wid), F32),
            pltpu.VMEM((8 + tb, wid), F32),
        ],
        compiler_params=_cparams(("parallel", "parallel", "arbitrary")),
        name="gdn",
    )(a_log, dt_bias, z_main, z_main, z_main, z_main, z_gate, conv_w, conv_w, conv_w,
      gdn_norm.reshape(1, GDN_DV))


def _mlstm_kernel(bi_ref, bf_ref, q_ref, k_ref, v_ref, og_ref, zg_ref, nrm_ref, o_ref, c_scr, m_scr):
    hb = c_scr.shape[0]
    h0 = pl.program_id(1) * hb
    tb = q_ref.shape[1]
    nchunk = tb // CHUNK
    cap = GATE_SOFTCAP

    @pl.when(pl.program_id(2) == 0)
    def _():
        c_scr[...] = jnp.zeros_like(c_scr)
        m_scr[...] = jnp.zeros_like(m_scr)

    zg = zg_ref[0]
    r, c, same = _chunk_masks(tb)
    tril = jnp.logical_and(same(6), r >= c)
    tril_bf = tril.astype(BF16)
    ones = jnp.ones((tb, LANES), F32)

    for hh in range(hb):
        q = q_ref[0, :, ML_DQK * hh:ML_DQK * (hh + 1)].astype(F32)
        k = k_ref[0, :, ML_DQK * hh:ML_DQK * (hh + 1)].astype(F32) * (ML_DQK ** -0.5)
        v = v_ref[0, :, ML_DV * hh:ML_DV * (hh + 1)].astype(F32)
        i_pre = cap * jnp.tanh((_lane_pick(zg, GATE_I + h0 + hh) + bi_ref[h0 + hh]) / cap)
        log_f = -_softplus(-(cap * jnp.tanh((_lane_pick(zg, GATE_F + h0 + hh) + bf_ref[h0 + hh]) / cap)))

        fcb = _dot_sel(tril_bf, jnp.broadcast_to(log_f, (tb, LANES)))
        fmi = fcb - i_pre
        fmi_row = fmi.T[0:1, :]
        fc_col = jnp.concatenate([fcb] * (tb // LANES), axis=1)
        d_mat = jnp.where(tril, fc_col - fmi_row, NEG_BIG)
        d_max = jnp.max(d_mat, axis=-1, keepdims=True)
        a_end = _chunk_last(fcb) - fmi

        m_st = m_scr[hh]
        m_rows, decs, wks = [], [], []
        for i in range(nchunk):
            sl = slice(CHUNK * i, CHUNK * (i + 1))
            f_last = fcb[CHUNK * (i + 1) - 1:CHUNK * (i + 1), :]
            a_max = jnp.max(a_end[sl], axis=0, keepdims=True)
            m_new = jnp.maximum(f_last + m_st, a_max)
            m_rows.append(jnp.broadcast_to(m_st, (CHUNK, LANES)))
            decs.append(jnp.exp(f_last + m_st - m_new))
            wks.append(jnp.exp(a_end[sl] - m_new))
            m_st = m_new
        m_scr[hh] = m_st

        inter = fcb + jnp.concatenate(m_rows, axis=0)
        m_i = jnp.maximum(d_max, inter)
        s_inter = jnp.exp(inter - m_i)
        wts = jnp.exp(d_mat - m_i[:, 0:1]) * _bdot_nt(q, k)
        v_aug = jnp.concatenate([v, ones], axis=1)
        intra = _bdot(wts, v_aug)

        cst = c_scr[hh]
        nums = []
        for i in range(nchunk):
            sl = slice(CHUNK * i, CHUNK * (i + 1))
            nums.append(s_inter[sl, 0:1] * _bdot(q[sl], cst) + intra[sl])
            cst = decs[i][:, 0:1] * cst + _bdot_tn(wks[i] * k[sl], v_aug[sl])
        c_scr[hh] = cst

        num_aug = jnp.concatenate(nums, axis=0)
        den = num_aug[:, ML_DV:ML_DV + 1]
        hout = num_aug[:, :ML_DV] / jnp.maximum(jnp.abs(den), jnp.exp(-m_i[:, 0:1]))
        hout = _rms(hout, nrm_ref[hh])
        og = og_ref[0, :, ML_DV * hh:ML_DV * (hh + 1)].astype(F32)
        o_ref[0, :, ML_DV * hh:ML_DV * (hh + 1)] = (hout * _sigmoid(og)).astype(o_ref.dtype)


def _mlstm(z_main, z_gate, ml_b_i, ml_b_f, ml_norm, tb=ML_TIME_BLOCK, hb=ML_HEADS_PER_STEP):
    b, s, _ = z_main.shape
    qw, vw = hb * ML_DQK, hb * ML_DV
    hq, hk = OFF_MQ // qw, OFF_MK // qw
    hv, ho = OFF_MV // vw, OFF_MO // vw
    qspec = lambda off: pl.BlockSpec((1, tb, qw), lambda bi, hi, ti: (bi, ti, off + hi))
    vspec = lambda off: pl.BlockSpec((1, tb, vw), lambda bi, hi, ti: (bi, ti, off + hi))
    smem = pl.BlockSpec(memory_space=pltpu.SMEM)
    return pl.pallas_call(
        _mlstm_kernel,
        grid=(b, ML_HEADS // hb, s // tb),
        in_specs=[
            smem, smem,
            qspec(hq), qspec(hk), vspec(hv), vspec(ho),
            pl.BlockSpec((1, tb, LANES), lambda bi, hi, ti: (bi, ti, 0)),
            pl.BlockSpec((hb, 1, ML_DV), lambda bi, hi, ti: (hi, 0, 0)),
        ],
        out_specs=pl.BlockSpec((1, tb, vw), lambda bi, hi, ti: (bi, ti, hi)),
        out_shape=jax.ShapeDtypeStruct((b, s, ML_V), BF16),
        scratch_shapes=[
            pltpu.VMEM((hb, ML_DQK, ML_DV + LANES), F32),
            pltpu.VMEM((hb, 1, LANES), F32),
        ],
        compiler_params=_cparams(("parallel", "parallel", "arbitrary")),
        name="mlstm",
    )(ml_b_i, ml_b_f, z_main, z_main, z_main, z_main, z_gate, ml_norm.reshape(ML_HEADS, 1, ML_DV))


PACK_SUB = 8


def _pack_rows(x, o_ref):
    n, d = x.shape
    half = d // 2
    for j in range(PACK_SUB):
        lo = x[:, LANES * j:LANES * (j + 1)].astype(BF16).astype(F32)
        hi = x[:, half + LANES * j:half + LANES * (j + 1)].astype(BF16).astype(F32)
        word = (lax.bitcast_convert_type(lo, jnp.uint32) >> 16) | lax.bitcast_convert_type(hi, jnp.uint32)
        o_ref[pl.ds(j, n, stride=PACK_SUB), :] = word


def _unpack_chunk(word):
    lo = lax.bitcast_convert_type(word << 16, F32)
    hi = lax.bitcast_convert_type(word & jnp.uint32(0xFFFF0000), F32)
    return lo, hi


def _row_tile(ref, row, n=1):
    start = row * PACK_SUB
    if not isinstance(start, int):
        start = pl.multiple_of(start, PACK_SUB)
    return ref.at[pl.ds(start, n * PACK_SUB), :]


def _outproj_kernel(x_ref, yg_ref, ym_ref, wo1_ref, wo2_ref, gf_ref, wr_ref, br_ref,
                    h_ref, u_ref, ti_ref, tg_ref):
    h1 = (x_ref[...] + jnp.dot(yg_ref[...], wo1_ref[...], preferred_element_type=F32)
          + jnp.dot(ym_ref[...], wo2_ref[...], preferred_element_type=F32))
    h_ref[...] = h1
    u = _rms(h1, gf_ref[...])
    _pack_rows(u, u_ref)

    u_hi = u.astype(BF16)
    u_lo = (u - u_hi.astype(F32)).astype(BF16)
    wr = wr_ref[...]
    w_hi = wr.astype(BF16)
    w_lo = (wr - w_hi.astype(F32)).astype(BF16)
    d = lambda a, b: jnp.dot(a, b, preferred_element_type=F32)
    logits = d(u_hi, w_hi) + (d(u_hi, w_lo) + d(u_lo, w_hi)) + br_ref[...]

    lane = lax.broadcasted_iota(jnp.int32, logits.shape, 1)
    lg = jnp.where(lane < N_EXPERTS, logits, NEG_BIG)
    vals, idxs = [], []
    for _ in range(TOP_K):
        m = jnp.max(lg, axis=1, keepdims=True)
        idx = jnp.min(jnp.where(lg == m, lane, LANES), axis=1, keepdims=True)
        vals.append(m)
        idxs.append(idx)
        lg = jnp.where(lane == idx, NEG_BIG, lg)
    es = [jnp.exp(vv - vals[0]) for vv in vals]
    tot = es[0] + es[1] + es[2] + es[3]
    ti = jnp.zeros(logits.shape, jnp.int32)
    tg = jnp.zeros(logits.shape, F32)
    for kk in range(TOP_K):
        ti = jnp.where(lane == kk, idxs[kk], ti)
        tg = jnp.where(lane == kk, es[kk] / tot, tg)
    ti_ref[...] = ti
    tg_ref[...] = tg


def _out_proj(x2, y_gdn, y_ml, w_out_bf, g_ffn, w_router_pad, b_router_pad, tm=512):
    t, d = x2.shape
    row = lambda w: pl.BlockSpec((tm, w), lambda i: (i, 0))
    const = lambda shp: pl.BlockSpec(shp, lambda i: (0, 0))
    return pl.pallas_call(
        _outproj_kernel,
        grid=(t // tm,),
        in_specs=[
            row(d), row(GDN_V), row(ML_V),
            pl.BlockSpec((GDN_V, d), lambda i: (0, 0)),
            pl.BlockSpec((ML_V, d), lambda i: (GDN_V // ML_V, 0)),
            const((1, d)), const((d, LANES)), const((1, LANES)),
        ],
        out_specs=[row(d), pl.BlockSpec((tm * PACK_SUB, LANES), lambda i: (i, 0)), row(LANES), row(LANES)],
        out_shape=[jax.ShapeDtypeStruct((t, d), F32), jax.ShapeDtypeStruct((t * PACK_SUB, LANES), jnp.uint32),
                   jax.ShapeDtypeStruct((t, LANES), jnp.int32), jax.ShapeDtypeStruct((t, LANES), F32)],
        compiler_params=_cparams(("parallel",)),
        name="out_proj_router",
    )(x2, y_gdn, y_ml, w_out_bf, w_out_bf, g_ffn.reshape(1, d), w_router_pad, b_router_pad)


FFN_ROWS = 512
INV_UNROLL = 8


def _invert_kernel(trips_ref, pos_ref, inv_ref):
    def fill(t, c):
        for k in range(INV_UNROLL):
            inv_ref[t * INV_UNROLL + k] = -1
        return c

    def body(t, c):
        for k in range(INV_UNROLL):
            a = t * INV_UNROLL + k
            inv_ref[FFN_ROWS + pos_ref[a]] = a
        return c

    lax.fori_loop(0, trips_ref[0], fill, 0)
    lax.fori_loop(0, trips_ref[1], body, 0)


def _invert_slots(pos_flat, n_rows):
    smem = pl.BlockSpec(memory_space=pltpu.SMEM)
    n_inv = FFN_ROWS + n_rows
    trips = jnp.array([n_inv // INV_UNROLL, pos_flat.shape[0] // INV_UNROLL], jnp.int32)
    return pl.pallas_call(
        _invert_kernel,
        in_specs=[smem, smem],
        out_specs=smem,
        out_shape=jax.ShapeDtypeStruct((n_inv,), jnp.int32),
        name="invert_slots",
    )(trips, pos_flat)


DEINT_GROUP = 2 * LANES
MXU_TILE = 256


def _prep_kernel(wgu_ref, wd_ref, wg_o, wu_o, wd_o):
    x = wgu_ref[0].astype(BF16)
    r = lax.broadcasted_iota(jnp.int32, (DEINT_GROUP, LANES), 0)
    c = lax.broadcasted_iota(jnp.int32, (DEINT_GROUP, LANES), 1)
    sel_even = (r == 2 * c).astype(BF16)
    sel_odd = (r == 2 * c + 1).astype(BF16)
    for k in range(x.shape[1] // DEINT_GROUP):
        blk = x[:, DEINT_GROUP * k:DEINT_GROUP * (k + 1)]
        wg_o[0, :, LANES * k:LANES * (k + 1)] = jnp.dot(blk, sel_even, preferred_element_type=F32).astype(BF16)
        wu_o[0, :, LANES * k:LANES * (k + 1)] = jnp.dot(blk, sel_odd, preferred_element_type=F32).astype(BF16)
    wd_o[...] = wd_ref[...].astype(BF16)


def _prep_expert_weights(w_gu, w_d, tn=512):
    n_e, d, f2 = w_gu.shape
    f = f2 // 2
    return pl.pallas_call(
        _prep_kernel,
        grid=(n_e, f // tn),
        in_specs=[
            pl.BlockSpec((1, d, 2 * tn), lambda e, j: (e, 0, j)),
            pl.BlockSpec((1, tn, d), lambda e, j: (e, j, 0)),
        ],
        out_specs=[
            pl.BlockSpec((1, d, tn), lambda e, j: (e, 0, j)),
            pl.BlockSpec((1, d, tn), lambda e, j: (e, 0, j)),
            pl.BlockSpec((1, tn, d), lambda e, j: (e, j, 0)),
        ],
        out_shape=[jax.ShapeDtypeStruct((n_e, d, f), BF16), jax.ShapeDtypeStruct((n_e, d, f), BF16),
                   jax.ShapeDtypeStruct((n_e, f, d), BF16)],
        compiler_params=_cparams(("parallel", "parallel")),
        name="expert_weight_prep",
    )(w_gu, w_d)


def _ffn_kernel(be_ref, rows_ref, nv_ref, inv_ref, u_hbm, wg_ref, wu_ref, wd_ref, bg_ref, bu_ref, bd_ref,
                y_hbm, xg_buf, xb_scr, acc_scr, o_buf, sem):
    i = pl.program_id(0)
    j = pl.program_id(1)
    nv = nv_ref[0]
    tm, d = xb_scr.shape
    half = d // 2
    trash0 = y_hbm.shape[0] // PACK_SUB - tm

    tf = wg_ref.shape[2]
    n_chunks = tf // MXU_TILE + d // MXU_TILE
    bounds = [tm * g // n_chunks for g in range(n_chunks + 1)]

    def gather_issue(blk, g=None):
        base = (blk + 1) * tm
        for r in range(tm) if g is None else range(bounds[g], bounds[g + 1]):
            tok = jnp.maximum(inv_ref[base + r], 0) >> 2
            pltpu.make_async_copy(_row_tile(u_hbm, tok), _row_tile(xg_buf, r), sem.at[0]).start()

    def gather_wait():
        pltpu.make_async_copy(_row_tile(u_hbm, 0, tm), xg_buf, sem.at[0]).wait()

    def scatter_issue(blk, g=None):
        base = (blk + 1) * tm
        for r in range(tm) if g is None else range(bounds[g], bounds[g + 1]):
            a = inv_ref[base + r]
            pltpu.make_async_copy(_row_tile(o_buf, r), _row_tile(y_hbm, jnp.where(a < 0, trash0 + r, a)),
                                  sem.at[1]).start()

    def scatter_wait():
        pltpu.make_async_copy(o_buf, _row_tile(y_hbm, 0, tm), sem.at[1]).wait()

    @pl.when(i < nv)
    def _():
        par = i % 2
        full = rows_ref[i] > tm // 2

        def tile(n, first, between):
            xb = xb_scr[0:n, :]
            g = 0
            acts = []
            for k in range(tf // MXU_TILE):
                cs = slice(MXU_TILE * k, MXU_TILE * (k + 1))
                gate = jnp.minimum(jnp.dot(xb, wg_ref[0, :, cs], preferred_element_type=F32) + bg_ref[0, :, cs],
                                   SWIGLU_LIMIT)
                up = jnp.clip(jnp.dot(xb, wu_ref[0, :, cs], preferred_element_type=F32) + bu_ref[0, :, cs],
                              -SWIGLU_LIMIT, SWIGLU_LIMIT)
                acts.append((gate * _sigmoid(gate * SWIGLU_ALPHA) * (up + 1.0)).astype(BF16))
                between(g)
                g += 1
            act = jnp.concatenate(acts, axis=1)
            for k in range(d // MXU_TILE):
                cs = slice(MXU_TILE * k, MXU_TILE * (k + 1))
                y = jnp.dot(act, wd_ref[0, :, cs], preferred_element_type=F32)
                if first:
                    acc_scr[par, 0:n, cs] = y + bd_ref[0, :, cs]
                    if n < tm:
                        acc_scr[par, n:tm, cs] = jnp.zeros((tm - n, MXU_TILE), F32)
                else:
                    acc_scr[par, 0:n, cs] += y
                between(g)
                g += 1

        def either_size(fn):
            @pl.when(full)
            def _():
                fn(tm)

            @pl.when(jnp.logical_not(full))
            def _():
                fn(tm // 2)

        @pl.when(j == 0)
        def _():
            @pl.when(i == 0)
            def _():
                acc_scr[1] = jnp.zeros((tm, d), F32)
                gather_issue(0)

            @pl.when(i > 0)
            def _():
                scatter_wait()

            gather_wait()
            for c in range(PACK_SUB):
                lo, hi = _unpack_chunk(xg_buf[pl.ds(c, tm, stride=PACK_SUB), :])
                xb_scr[:, LANES * c:LANES * (c + 1)] = lo.astype(BF16)
                xb_scr[:, half + LANES * c:half + LANES * (c + 1)] = hi.astype(BF16)
            _pack_rows(acc_scr[1 - par], o_buf)
            nxt = jnp.minimum(i + 1, nv - 1)
            either_size(lambda n: tile(n, True, lambda g: gather_issue(nxt, g)))

        @pl.when(j == 1)
        def _():
            either_size(lambda n: tile(n, False, lambda g: scatter_issue(i - 1, g)))

            @pl.when(i == nv - 1)
            def _():
                scatter_wait()
                _pack_rows(acc_scr[par], o_buf)
                scatter_issue(i)
                scatter_wait()
                gather_wait()


def _ffn(u_packed, inv, block_e, block_rows, n_valid, w_g, w_u, w_d, b_g, b_u, b_d, tm=FFN_ROWS):
    n_e, f, d = w_d.shape
    nf = 2
    tf = f // nf
    n_blocks = inv.shape[0] // tm - 1
    n_asg = (u_packed.shape[0] // PACK_SUB) * TOP_K
    ic = lambda i, nv: jnp.minimum(i, nv[0] - 1)
    jc = lambda i, j, nv: jnp.where(i < nv[0], j, nf - 1)
    col = lambda i, j, be, br, nv, iv: (be[ic(i, nv)], 0, jc(i, j, nv))
    return pl.pallas_call(
        _ffn_kernel,
        grid_spec=pltpu.PrefetchScalarGridSpec(
            num_scalar_prefetch=4,
            grid=(n_blocks, nf),
            in_specs=[
                pl.BlockSpec(memory_space=pl.ANY),
                pl.BlockSpec((1, d, tf), col),
                pl.BlockSpec((1, d, tf), col),
                pl.BlockSpec((1, tf, d), lambda i, j, be, br, nv, iv: (be[ic(i, nv)], jc(i, j, nv), 0)),
                pl.BlockSpec((1, 1, tf), col),
                pl.BlockSpec((1, 1, tf), col),
                pl.BlockSpec((1, 1, d), lambda i, j, be, br, nv, iv: (be[ic(i, nv)], 0, 0)),
            ],
            out_specs=pl.BlockSpec(memory_space=pl.ANY),
            scratch_shapes=[
                pltpu.VMEM((tm * PACK_SUB, LANES), jnp.uint32),
                pltpu.VMEM((tm, d), BF16),
                pltpu.VMEM((2, tm, d), F32),
                pltpu.VMEM((tm * PACK_SUB, LANES), jnp.uint32),
                pltpu.SemaphoreType.DMA((2,)),
            ],
        ),
        out_shape=jax.ShapeDtypeStruct(((n_asg + tm) * PACK_SUB, LANES), jnp.uint32),
        compiler_params=_cparams(("arbitrary", "arbitrary")),
        name="expert_ffn",
    )(block_e, block_rows, n_valid, inv, u_packed, w_g, w_u, w_d, b_g.reshape(n_e, 1, f),
      b_u.reshape(n_e, 1, f), b_d.reshape(n_e, 1, d))


COMBINE_ROWS = 256


def _combine_kernel(y_ref, tg_ref, h1_ref, p_ref, wpg_ref, wpp_ref, gp_ref, gpp_ref, gfin_ref, o_ref):
    tc = tg_ref.shape[0]
    tg = tg_ref[...]
    lo_parts, hi_parts = [], []
    for c in range(PACK_SUB):
        lo_sum = hi_sum = None
        for kk in range(TOP_K):
            lo, hi = _unpack_chunk(y_ref[pl.ds(PACK_SUB * kk + c, tc, stride=PACK_SUB * TOP_K), :])
            gk = tg[:, kk:kk + 1]
            lo_sum = gk * lo if lo_sum is None else lo_sum + gk * lo
            hi_sum = gk * hi if hi_sum is None else hi_sum + gk * hi
        lo_parts.append(lo_sum)
        hi_parts.append(hi_sum)
    moe = jnp.concatenate(lo_parts + hi_parts, axis=1)
    h2 = h1_ref[...] + moe
    gate = _sigmoid(_bdot(_rms(h2, gp_ref[...]), wpg_ref[...]))
    pe = _rms(_bdot(p_ref[...], wpp_ref[...]), gpp_ref[...])
    h3 = h2 + gate * pe
    o_ref[...] = _rms(h3, gfin_ref[...])


def _combine(y_tok, tg, h1, p2, w_pg, w_pp, g_ple, g_ple_post, g_final, tc=COMBINE_ROWS):
    t, d = h1.shape
    pd = p2.shape[1]
    row = lambda w: pl.BlockSpec((tc, w), lambda i: (i, 0))
    const = lambda shp: pl.BlockSpec(shp, lambda i: (0, 0))
    return pl.pallas_call(
        _combine_kernel,
        grid=(t // tc,),
        in_specs=[
            pl.BlockSpec((tc * TOP_K * PACK_SUB, LANES), lambda i: (i, 0)),
            row(LANES), row(d), row(pd),
            const((d, d)), const((pd, d)), const((1, d)), const((1, d)), const((1, d)),
        ],
        out_specs=row(d),
        out_shape=jax.ShapeDtypeStruct((t, d), F32),
        compiler_params=_cparams(("parallel",)),
        name="combine_ple",
    )(y_tok, tg, h1, p2, w_pg, w_pp, g_ple.reshape(1, d), g_ple_post.reshape(1, d), g_final.reshape(1, d))


def _routing_tables(top_i, n_tok, tm=FFN_ROWS):
    n_asg = n_tok * TOP_K
    e_flat = top_i.reshape(n_asg)
    onehot = (e_flat[:, None] == jnp.arange(N_EXPERTS, dtype=jnp.int32)[None, :]).astype(jnp.int32)
    grp = onehot.reshape(n_asg // LANES, LANES, N_EXPERTS).astype(F32)
    within = jnp.einsum("ij,gjk->gik", jnp.tril(jnp.ones((LANES, LANES), F32)), grp).astype(jnp.int32)
    tot = within[:, -1, :]
    csum = (within + (jnp.cumsum(tot, axis=0) - tot)[:, None, :]).reshape(n_asg, N_EXPERTS)
    counts = csum[-1]
    padded = (counts + tm - 1) // tm * tm
    pend = jnp.cumsum(padded)
    pstart = pend - padded
    pos = jnp.sum(onehot * (csum - 1 + pstart[None, :]), axis=1).astype(jnp.int32)
    n_blocks = (n_tok * TOP_K) // tm + N_EXPERTS
    block_start = jnp.arange(n_blocks, dtype=jnp.int32) * tm
    block_e = jnp.minimum(jnp.sum(pend[None, :] <= block_start[:, None], axis=1), N_EXPERTS - 1).astype(jnp.int32)
    block_rows = jnp.clip((pstart + counts)[block_e] - block_start, 0, tm).astype(jnp.int32)
    n_valid = (pend[-1] // tm).astype(jnp.int32).reshape(1)
    return pos, block_e, block_rows, n_valid, n_blocks * tm


def kernel(x, p, g_mix, w_in, conv_w, a_log, dt_bias, gdn_norm, ml_b_i, ml_b_f, ml_norm, w_out, g_ffn, w_router, b_router, w_gu, b_gu, w_down, b_down, g_ple, w_ple_gate, w_ple_proj, g_ple_post, g_final):
    b, s, d = x.shape
    n_tok = b * s
    assert w_in.shape[0] == 1, "single-layer block: the final norm is fused into the layer's last kernel"
    l = 0
    x2 = x.reshape(n_tok, d)
    w_main, w_gate = _prep_in_weights(w_in[l])
    z_main, z_gate = _in_proj(x2, g_mix[l], w_main, w_gate)
    z_main = z_main.reshape(b, s, Z_MAIN)
    z_gate = z_gate.reshape(b, s, LANES)
    y_gdn = _gdn(z_main, z_gate, conv_w[l], a_log[l], dt_bias[l], gdn_norm[l])
    y_ml = _mlstm(z_main, z_gate, ml_b_i[l], ml_b_f[l], ml_norm[l])

    w_r = jnp.pad(w_router[l], ((0, 0), (0, LANES - N_EXPERTS)))
    b_r = jnp.pad(b_router[l], (0, LANES - N_EXPERTS)).reshape(1, LANES)
    h1, u2, top_i, top_g = _out_proj(x2, y_gdn.reshape(n_tok, GDN_V), y_ml.reshape(n_tok, ML_V),
                                     w_out[l].astype(BF16), g_ffn[l], w_r, b_r)

    pos, block_e, block_rows, n_valid, n_rows = _routing_tables(top_i[:, :TOP_K], n_tok)
    inv = _invert_slots(pos, n_rows)
    w_g, w_u, w_d = _prep_expert_weights(w_gu[l], w_down[l])
    y_tok = _ffn(u2, inv, block_e, block_rows, n_valid, w_g, w_u, w_d,
                 b_gu[l][:, 0::2], b_gu[l][:, 1::2], b_down[l])
    out = _combine(y_tok, top_g, h1, p[l].reshape(n_tok, -1), w_ple_gate[l].astype(BF16),
                   w_ple_proj[l].astype(BF16), g_ple[l], g_ple_post[l], g_final)
    return out.reshape(b, s, d)
```

```python
import jax
import jax.numpy as jnp
from jax import lax
from jax.experimental import pallas as pl
from jax.experimental.pallas import tpu as pltpu

F32 = jnp.float32
BF16 = jnp.bfloat16

EPS = 1e-6
CHUNK = 64
GDN_HEADS = 8
GDN_DK = 128
GDN_DV = 128
GDN_CONV = 4
ML_HEADS = 4
ML_DQK = 128
ML_DV = 256
GATE_SOFTCAP = 15.0
N_EXPERTS = 32
TOP_K = 4
SWIGLU_LIMIT = 7.0
SWIGLU_ALPHA = 1.702

LANES = 128
NEG_BIG = -1e30

GDN_QK = GDN_HEADS * GDN_DK
GDN_V = GDN_HEADS * GDN_DV
ML_QK = ML_HEADS * ML_DQK
ML_V = ML_HEADS * ML_DV
OFF_Q = 0
OFF_K = OFF_Q + GDN_QK
OFF_V = OFF_K + GDN_QK
OFF_GZ = OFF_V + GDN_V
OFF_MQ = OFF_GZ + GDN_V
OFF_MK = OFF_MQ + ML_QK
OFF_MV = OFF_MK + ML_QK
OFF_MO = OFF_MV + ML_V
Z_MAIN = OFF_MO + ML_V
GATE_A = 0
GATE_B = GATE_A + GDN_HEADS
GATE_I = GATE_B + GDN_HEADS
GATE_F = GATE_I + ML_HEADS

GDN_TIME_BLOCK = 128
ML_TIME_BLOCK = 256
GDN_HEADS_PER_STEP = 8
ML_HEADS_PER_STEP = 4
VMEM_LIMIT = 56 * 1024 * 1024


def _cparams(sem):
    return pltpu.CompilerParams(dimension_semantics=sem, vmem_limit_bytes=VMEM_LIMIT)


def _bdot(a, b):
    return jnp.dot(a.astype(BF16), b.astype(BF16), preferred_element_type=F32)


def _bdot_nt(a, b):
    return lax.dot_general(a.astype(BF16), b.astype(BF16), (((1,), (1,)), ((), ())),
                           preferred_element_type=F32)


def _bdot_tn(a, b):
    return lax.dot_general(a.astype(BF16), b.astype(BF16), (((0,), (0,)), ((), ())),
                           preferred_element_type=F32)


def _split3(x):
    hi = x.astype(BF16)
    r1 = x - hi.astype(F32)
    mid = r1.astype(BF16)
    lo = (r1 - mid.astype(F32)).astype(BF16)
    return hi, mid, lo


def _dot_sel(sel_bf16, x):
    hi, mid, lo = _split3(x)
    d = lambda t: jnp.dot(sel_bf16, t, preferred_element_type=F32)
    return d(hi) + (d(mid) + d(lo))


def _sigmoid(x):
    return 1.0 / (1.0 + jnp.exp(-x))


def _softplus(x):
    return jnp.maximum(x, 0.0) + jnp.log1p(jnp.exp(-jnp.abs(x)))


def _rms(x, g):
    return x * lax.rsqrt(jnp.mean(x * x, axis=-1, keepdims=True) + EPS) * g


def _lane_pick(zg, idx):
    lane = lax.broadcasted_iota(jnp.int32, zg.shape, 1)
    return jnp.sum(jnp.where(lane == idx, zg, 0.0), axis=1, keepdims=True)


def _chunk_masks(n):
    r = lax.broadcasted_iota(jnp.int32, (n, n), 0)
    c = lax.broadcasted_iota(jnp.int32, (n, n), 1)
    same = lambda s: (r >> s) == (c >> s)
    return r, c, same


def _chunk_last(gc):
    n = gc.shape[0] // CHUNK
    parts = [jnp.broadcast_to(gc[CHUNK * (i + 1) - 1:CHUNK * (i + 1), :], (CHUNK, gc.shape[1]))
             for i in range(n)]
    return jnp.concatenate(parts, axis=0)


def _inproj_kernel(x_ref, g_ref, w_ref, wg_ref, z_ref, zg_ref, u_scr):
    @pl.when(pl.program_id(1) == 0)
    def _():
        u = _rms(x_ref[...], g_ref[...]).astype(BF16)
        u_scr[...] = u
        zg_ref[...] = jnp.dot(u, wg_ref[...], preferred_element_type=F32)

    z_ref[...] = jnp.dot(u_scr[...], w_ref[...], preferred_element_type=F32).astype(z_ref.dtype)


def _in_proj(x2, g_mix, w_main, w_gate, tm=1024, tn=1024):
    t, d = x2.shape
    n = w_main.shape[1]
    return pl.pallas_call(
        _inproj_kernel,
        grid=(t // tm, n // tn),
        in_specs=[
            pl.BlockSpec((tm, d), lambda i, j: (i, 0)),
            pl.BlockSpec((1, d), lambda i, j: (0, 0)),
            pl.BlockSpec((d, tn), lambda i, j: (0, j)),
            pl.BlockSpec((d, LANES), lambda i, j: (0, 0)),
        ],
        out_specs=[
            pl.BlockSpec((tm, tn), lambda i, j: (i, j)),
            pl.BlockSpec((tm, LANES), lambda i, j: (i, 0)),
        ],
        out_shape=[jax.ShapeDtypeStruct((t, n), BF16), jax.ShapeDtypeStruct((t, LANES), F32)],
        scratch_shapes=[pltpu.VMEM((tm, d), BF16)],
        compiler_params=_cparams(("parallel", "arbitrary")),
        name="in_proj",
    )(x2, g_mix.reshape(1, d), w_main, w_gate)


def _prep_in_weights(w_in):
    splits = (GDN_QK, GDN_QK, GDN_V, GDN_HEADS, GDN_HEADS, GDN_V, ML_QK, ML_QK, ML_V, ML_HEADS, ML_HEADS, ML_V)
    offs = [0]
    for wd in splits:
        offs.append(offs[-1] + wd)
    part = lambda i: w_in[:, offs[i]:offs[i + 1]]
    w_main = jnp.concatenate([part(i) for i in (0, 1, 2, 5, 6, 7, 8, 11)], axis=1).astype(BF16)
    gates = jnp.concatenate([part(i) for i in (3, 4, 9, 10)], axis=1)
    w_gate = jnp.pad(gates, ((0, 0), (0, LANES - gates.shape[1]))).astype(BF16)
    return w_main, w_gate


def _tri_inverse_minus_eye(ms, same):
    m16 = same(4)
    m32 = same(5)
    n32_mask = jnp.logical_and(m32, jnp.logical_not(m16))
    a = [jnp.where(m16, -m, 0.0) for m in ms]
    acc = list(a)
    for _ in range(3):
        a = [_bdot(x, x) for x in a]
        acc = [p + x + _bdot(p, x) for p, x in zip(acc, a)]
    for level in range(2):
        ns = [jnp.where(n32_mask, m, 0.0) if level == 0 else jnp.where(m32, 0.0, m) for m in ms]
        ys = [n + _bdot(p, n) for p, n in zip(acc, ns)]
        acc = [p - (y + _bdot(y, p)) for p, y in zip(acc, ys)]
    return acc


def _gdn_kernel(alog_ref, dtb_ref, zq_ref, zk_ref, zv_ref, gz_ref, zg_ref, cwq_ref, cwk_ref, cwv_ref,
                gn_ref, o_ref, s_scr, xq_scr, xk_scr, xv_scr):
    hb = s_scr.shape[0]
    h0 = pl.program_id(1) * hb
    tb = zq_ref.shape[1]
    nchunk = tb // CHUNK
    heads = range(hb)
    lanes = lambda hh: slice(LANES * hh, LANES * (hh + 1))

    @pl.when(pl.program_id(2) == 0)
    def _():
        s_scr[...] = jnp.zeros_like(s_scr)
        for scr in (xq_scr, xk_scr, xv_scr):
            scr[0:8, :] = jnp.zeros((8, scr.shape[1]), F32)

    def conv_silu(z_ref, x_scr, cw_ref):
        x = z_ref[0].astype(F32)
        x_scr[8:8 + tb, :] = x
        w = cw_ref[...]
        acc = x * w[GDN_CONV - 1:GDN_CONV, :]
        for s in range(1, GDN_CONV):
            acc = acc + x_scr[pl.ds(8 - s, tb), :] * w[GDN_CONV - 1 - s:GDN_CONV - s, :]
        x_scr[0:8, :] = x[tb - 8:tb, :]
        return acc * _sigmoid(acc)

    q_all = conv_silu(zq_ref, xq_scr, cwq_ref)
    k_all = conv_silu(zk_ref, xk_scr, cwk_ref)
    v_all = conv_silu(zv_ref, xv_scr, cwv_ref)
    gz_all = gz_ref[0].astype(F32)
    zg = zg_ref[0]

    r, c, same = _chunk_masks(tb)
    in_chunk = same(6)
    tril = jnp.logical_and(in_chunk, r >= c)
    strict = jnp.logical_and(in_chunk, r > c)
    tril_bf = tril.astype(BF16)

    def head_prep(hh):
        q = q_all[:, lanes(hh)]
        k = k_all[:, lanes(hh)]
        v = v_all[:, lanes(hh)]
        q = q * lax.rsqrt(jnp.sum(q * q, axis=-1, keepdims=True) + EPS) * (GDN_DK ** -0.5)
        k = k * lax.rsqrt(jnp.sum(k * k, axis=-1, keepdims=True) + EPS)
        beta = _sigmoid(_lane_pick(zg, GATE_B + h0 + hh))
        a_coef = jnp.exp(jnp.full((1, 1), alog_ref[h0 + hh], F32))
        g = -a_coef * _softplus(_lane_pick(zg, GATE_A + h0 + hh) + dtb_ref[h0 + hh])
        gcb = _dot_sel(tril_bf, jnp.broadcast_to(g, (tb, LANES)))
        gc_row = gcb.T[0:1, :]
        gc_col = jnp.concatenate([gcb] * (tb // LANES), axis=1)
        decay = jnp.exp(jnp.where(tril, gc_col - gc_row, NEG_BIG))
        eg = jnp.exp(gcb)
        kb = k * beta
        m_low = jnp.where(strict, _bdot_nt(kb, k) * decay, 0.0)
        rhs = jnp.concatenate([v * beta, kb * eg], axis=1)
        attn = _bdot_nt(q, k) * decay
        kd = k * jnp.exp(_chunk_last(gcb) - gcb)
        return dict(m_low=m_low, rhs=rhs, attn=attn, qd=q * eg, kd=kd, gcb=gcb)

    hp = [head_prep(hh) for hh in heads]
    t_m1 = _tri_inverse_minus_eye([p["m_low"] for p in hp], same)
    uws = [p["rhs"] + _bdot(t, p["rhs"]) for p, t in zip(hp, t_m1)]

    states = [s_scr[hh] for hh in heads]
    outs = [[] for _ in heads]
    for i in range(nchunk):
        sl = slice(CHUNK * i, CHUNK * (i + 1))
        for hh in heads:
            p, uw = hp[hh], uws[hh]
            res1 = _bdot(jnp.concatenate([uw[sl, GDN_DV:], p["qd"][sl]], axis=0), states[hh])
            v_new = uw[sl, :GDN_DV] - res1[:CHUNK]
            o_intra = _bdot(p["attn"][sl, CHUNK * i:CHUNK * (i + 1)], v_new)
            outs[hh].append(res1[CHUNK:] + o_intra)
            g_last = jnp.exp(p["gcb"][CHUNK * (i + 1) - 1:CHUNK * (i + 1), :])
            states[hh] = states[hh] * g_last[:, 0:1] + _bdot_tn(p["kd"][sl], v_new)

    for hh in heads:
        s_scr[hh] = states[hh]
        o = _rms(jnp.concatenate(outs[hh], axis=0), gn_ref[...])
        gz = gz_all[:, lanes(hh)]
        o_ref[0, :, lanes(hh)] = (o * (gz * _sigmoid(gz))).astype(o_ref.dtype)


def _gdn(z_main, z_gate, conv_w, a_log, dt_bias, gdn_norm, tb=GDN_TIME_BLOCK, hb=GDN_HEADS_PER_STEP):
    b, s, _ = z_main.shape
    wid = hb * LANES
    hq, hk, hv, hz = OFF_Q // wid, OFF_K // wid, OFF_V // wid, OFF_GZ // wid
    zspec = lambda off: pl.BlockSpec((1, tb, wid), lambda bi, hi, ti: (bi, ti, off + hi))
    cspec = lambda off: pl.BlockSpec((GDN_CONV, wid), lambda bi, hi, ti: (0, off + hi))
    smem = pl.BlockSpec(memory_space=pltpu.SMEM)
    return pl.pallas_call(
        _gdn_kernel,
        grid=(b, GDN_HEADS // hb, s // tb),
        in_specs=[
            smem, smem,
            zspec(hq), zspec(hk), zspec(hv), zspec(hz),
            pl.BlockSpec((1, tb, LANES), lambda bi, hi, ti: (bi, ti, 0)),
            cspec(hq), cspec(hk), cspec(hv),
            pl.BlockSpec((1, GDN_DV), lambda bi, hi, ti: (0, 0)),
        ],
        out_specs=pl.BlockSpec((1, tb, wid), lambda bi, hi, ti: (bi, ti, hi)),
        out_shape=jax.ShapeDtypeStruct((b, s, GDN_V), BF16),
        scratch_shapes=[
            pltpu.VMEM((hb, GDN_DK, GDN_DV), F32),
            pltpu.VMEM((8 + tb, wid), F32),
            pltpu.VMEM((8 + tb, wid), F32),
            pltpu.VMEM((8 + tb, wid), F32),
        ],
        compiler_params=_cparams(("parallel", "parallel", "arbitrary")),
        name="gdn",
    )(a_log, dt_bias, z_main, z_main, z_main, z_main, z_gate, conv_w, conv_w, conv_w,
      gdn_norm.reshape(1, GDN_DV))


def _mlstm_kernel(bi_ref, bf_ref, q_ref, k_ref, v_ref, og_ref, zg_ref, nrm_ref, o_ref, c_scr, m_scr):
    hb = c_scr.shape[0]
    h0 = pl.program_id(1) * hb
    tb = q_ref.shape[1]
    nchunk = tb // CHUNK
    cap = GATE_SOFTCAP

    @pl.when(pl.program_id(2) == 0)
    def _():
        c_scr[...] = jnp.zeros_like(c_scr)
        m_scr[...] = jnp.zeros_like(m_scr)

    zg = zg_ref[0]
    r, c, same = _chunk_masks(tb)
    tril = jnp.logical_and(same(6), r >= c)
    tril_bf = tril.astype(BF16)
    ones = jnp.ones((tb, LANES), F32)

    for hh in range(hb):
        q = q_ref[0, :, ML_DQK * hh:ML_DQK * (hh + 1)].astype(F32)
        k = k_ref[0, :, ML_DQK * hh:ML_DQK * (hh + 1)].astype(F32) * (ML_DQK ** -0.5)
        v = v_ref[0, :, ML_DV * hh:ML_DV * (hh + 1)].astype(F32)
        i_pre = cap * jnp.tanh((_lane_pick(zg, GATE_I + h0 + hh) + bi_ref[h0 + hh]) / cap)
        log_f = -_softplus(-(cap * jnp.tanh((_lane_pick(zg, GATE_F + h0 + hh) + bf_ref[h0 + hh]) / cap)))

        fcb = _dot_sel(tril_bf, jnp.broadcast_to(log_f, (tb, LANES)))
        fmi = fcb - i_pre
        fmi_row = fmi.T[0:1, :]
        fc_col = jnp.concatenate([fcb] * (tb // LANES), axis=1)
        d_mat = jnp.where(tril, fc_col - fmi_row, NEG_BIG)
        d_max = jnp.max(d_mat, axis=-1, keepdims=True)
        a_end = _chunk_last(fcb) - fmi

        m_st = m_scr[hh]
        m_rows, decs, wks = [], [], []
        for i in range(nchunk):
            sl = slice(CHUNK * i, CHUNK * (i + 1))
            f_last = fcb[CHUNK * (i + 1) - 1:CHUNK * (i + 1), :]
            a_max = jnp.max(a_end[sl], axis=0, keepdims=True)
            m_new = jnp.maximum(f_last + m_st, a_max)
            m_rows.append(jnp.broadcast_to(m_st, (CHUNK, LANES)))
            decs.append(jnp.exp(f_last + m_st - m_new))
            wks.append(jnp.exp(a_end[sl] - m_new))
            m_st = m_new
        m_scr[hh] = m_st

        inter = fcb + jnp.concatenate(m_rows, axis=0)
        m_i = jnp.maximum(d_max, inter)
        s_inter = jnp.exp(inter - m_i)
        wts = jnp.exp(d_mat - m_i[:, 0:1]) * _bdot_nt(q, k)
        v_aug = jnp.concatenate([v, ones], axis=1)
        intra = _bdot(wts, v_aug)

        cst = c_scr[hh]
        nums = []
        for i in range(nchunk):
            sl = slice(CHUNK * i, CHUNK * (i + 1))
            nums.append(s_inter[sl, 0:1] * _bdot(q[sl], cst) + intra[sl])
            cst = decs[i][:, 0:1] * cst + _bdot_tn(wks[i] * k[sl], v_aug[sl])
        c_scr[hh] = cst

        num_aug = jnp.concatenate(nums, axis=0)
        den = num_aug[:, ML_DV:ML_DV + 1]
        hout = num_aug[:, :ML_DV] / jnp.maximum(jnp.abs(den), jnp.exp(-m_i[:, 0:1]))
        hout = _rms(hout, nrm_ref[hh])
        og = og_ref[0, :, ML_DV * hh:ML_DV * (hh + 1)].astype(F32)
        o_ref[0, :, ML_DV * hh:ML_DV * (hh + 1)] = (hout * _sigmoid(og)).astype(o_ref.dtype)


def _mlstm(z_main, z_gate, ml_b_i, ml_b_f, ml_norm, tb=ML_TIME_BLOCK, hb=ML_HEADS_PER_STEP):
    b, s, _ = z_main.shape
    qw, vw = hb * ML_DQK, hb * ML_DV
    hq, hk = OFF_MQ // qw, OFF_MK // qw
    hv, ho = OFF_MV // vw, OFF_MO // vw
    qspec = lambda off: pl.BlockSpec((1, tb, qw), lambda bi, hi, ti: (bi, ti, off + hi))
    vspec = lambda off: pl.BlockSpec((1, tb, vw), lambda bi, hi, ti: (bi, ti, off + hi))
    smem = pl.BlockSpec(memory_space=pltpu.SMEM)
    return pl.pallas_call(
        _mlstm_kernel,
        grid=(b, ML_HEADS // hb, s // tb),
        in_specs=[
            smem, smem,
            qspec(hq), qspec(hk), vspec(hv), vspec(ho),
            pl.BlockSpec((1, tb, LANES), lambda bi, hi, ti: (bi, ti, 0)),
            pl.BlockSpec((hb, 1, ML_DV), lambda bi, hi, ti: (hi, 0, 0)),
        ],
        out_specs=pl.BlockSpec((1, tb, vw), lambda bi, hi, ti: (bi, ti, hi)),
        out_shape=jax.ShapeDtypeStruct((b, s, ML_V), BF16),
        scratch_shapes=[
            pltpu.VMEM((hb, ML_DQK, ML_DV + LANES), F32),
            pltpu.VMEM((hb, 1, LANES), F32),
        ],
        compiler_params=_cparams(("parallel", "parallel", "arbitrary")),
        name="mlstm",
    )(ml_b_i, ml_b_f, z_main, z_main, z_main, z_main, z_gate, ml_norm.reshape(ML_HEADS, 1, ML_DV))


PACK_SUB = 8


def _pack_rows(x, o_ref):
    n, d = x.shape
    half = d // 2
    for j in range(PACK_SUB):
        lo = x[:, LANES * j:LANES * (j + 1)].astype(BF16).astype(F32)
        hi = x[:, half + LANES * j:half + LANES * (j + 1)].astype(BF16).astype(F32)
        word = (lax.bitcast_convert_type(lo, jnp.uint32) >> 16) | lax.bitcast_convert_type(hi, jnp.uint32)
        o_ref[pl.ds(j, n, stride=PACK_SUB), :] = word


def _unpack_chunk(word):
    lo = lax.bitcast_convert_type(word << 16, F32)
    hi = lax.bitcast_convert_type(word & jnp.uint32(0xFFFF0000), F32)
    return lo, hi


def _row_tile(ref, row, n=1):
    start = row * PACK_SUB
    if not isinstance(start, int):
        start = pl.multiple_of(start, PACK_SUB)
    return ref.at[pl.ds(start, n * PACK_SUB), :]


def _outproj_kernel(x_ref, yg_ref, ym_ref, wo1_ref, wo2_ref, gf_ref, wr_ref, br_ref,
                    h_ref, u_ref, ti_ref, tg_ref):
    h1 = (x_ref[...] + jnp.dot(yg_ref[...], wo1_ref[...], preferred_element_type=F32)
          + jnp.dot(ym_ref[...], wo2_ref[...], preferred_element_type=F32))
    h_ref[...] = h1
    u = _rms(h1, gf_ref[...])
    _pack_rows(u, u_ref)

    u_hi = u.astype(BF16)
    u_lo = (u - u_hi.astype(F32)).astype(BF16)
    wr = wr_ref[...]
    w_hi = wr.astype(BF16)
    w_lo = (wr - w_hi.astype(F32)).astype(BF16)
    d = lambda a, b: jnp.dot(a, b, preferred_element_type=F32)
    logits = d(u_hi, w_hi) + (d(u_hi, w_lo) + d(u_lo, w_hi)) + br_ref[...]

    lane = lax.broadcasted_iota(jnp.int32, logits.shape, 1)
    lg = jnp.where(lane < N_EXPERTS, logits, NEG_BIG)
    vals, idxs = [], []
    for _ in range(TOP_K):
        m = jnp.max(lg, axis=1, keepdims=True)
        idx = jnp.min(jnp.where(lg == m, lane, LANES), axis=1, keepdims=True)
        vals.append(m)
        idxs.append(idx)
        lg = jnp.where(lane == idx, NEG_BIG, lg)
    es = [jnp.exp(vv - vals[0]) for vv in vals]
    tot = es[0] + es[1] + es[2] + es[3]
    ti = jnp.zeros(logits.shape, jnp.int32)
    tg = jnp.zeros(logits.shape, F32)
    for kk in range(TOP_K):
        ti = jnp.where(lane == kk, idxs[kk], ti)
        tg = jnp.where(lane == kk, es[kk] / tot, tg)
    ti_ref[...] = ti
    tg_ref[...] = tg


def _out_proj(x2, y_gdn, y_ml, w_out_bf, g_ffn, w_router_pad, b_router_pad, tm=512):
    t, d = x2.shape
    row = lambda w: pl.BlockSpec((tm, w), lambda i: (i, 0))
    const = lambda shp: pl.BlockSpec(shp, lambda i: (0, 0))
    return pl.pallas_call(
        _outproj_kernel,
        grid=(t // tm,),
        in_specs=[
            row(d), row(GDN_V), row(ML_V),
            pl.BlockSpec((GDN_V, d), lambda i: (0, 0)),
            pl.BlockSpec((ML_V, d), lambda i: (GDN_V // ML_V, 0)),
            const((1, d)), const((d, LANES)), const((1, LANES)),
        ],
        out_specs=[row(d), pl.BlockSpec((tm * PACK_SUB, LANES), lambda i: (i, 0)), row(LANES), row(LANES)],
        out_shape=[jax.ShapeDtypeStruct((t, d), F32), jax.ShapeDtypeStruct((t * PACK_SUB, LANES), jnp.uint32),
                   jax.ShapeDtypeStruct((t, LANES), jnp.int32), jax.ShapeDtypeStruct((t, LANES), F32)],
        compiler_params=_cparams(("parallel",)),
        name="out_proj_router",
    )(x2, y_gdn, y_ml, w_out_bf, w_out_bf, g_ffn.reshape(1, d), w_router_pad, b_router_pad)


FFN_ROWS = 512
INV_UNROLL = 8


def _invert_kernel(lo_ref, hi_ref, trips_ref, pos_ref, inv_ref):
    def fill8(t, c):
        for k in range(INV_UNROLL):
            inv_ref[t * INV_UNROLL + k] = -1
        return c

    def fill1(r, c):
        inv_ref[r] = -1
        return c

    def per_expert(e, c):
        lax.fori_loop(lo_ref[e], hi_ref[e], fill1, 0)
        return c

    def body(t, c):
        for k in range(INV_UNROLL):
            a = t * INV_UNROLL + k
            inv_ref[FFN_ROWS + pos_ref[a]] = a
        return c

    lax.fori_loop(0, FFN_ROWS // INV_UNROLL, fill8, 0)
    lax.fori_loop(trips_ref[0], trips_ref[1], fill8, 0)
    lax.fori_loop(0, lo_ref.shape[0], per_expert, 0)
    lax.fori_loop(0, trips_ref[2], body, 0)


def _invert_slots(pos_flat, pad_lo, pad_hi, rows_used, n_rows):
    smem = pl.BlockSpec(memory_space=pltpu.SMEM)
    n_inv = FFN_ROWS + n_rows
    trips = jnp.stack([(FFN_ROWS + rows_used) // INV_UNROLL, jnp.int32(n_inv // INV_UNROLL),
                       jnp.int32(pos_flat.shape[0] // INV_UNROLL)]).astype(jnp.int32)
    return pl.pallas_call(
        _invert_kernel,
        in_specs=[smem, smem, smem, smem],
        out_specs=smem,
        out_shape=jax.ShapeDtypeStruct((n_inv,), jnp.int32),
        name="invert_slots",
    )(FFN_ROWS + pad_lo, FFN_ROWS + pad_hi, trips, pos_flat)


DEINT_GROUP = 2 * LANES
MXU_TILE = 256


def _prep_kernel(wgu_ref, wd_ref, wg_o, wu_o, wd_o):
    x = wgu_ref[0].astype(BF16)
    r = lax.broadcasted_iota(jnp.int32, (DEINT_GROUP, LANES), 0)
    c = lax.broadcasted_iota(jnp.int32, (DEINT_GROUP, LANES), 1)
    sel_even = (r == 2 * c).astype(BF16)
    sel_odd = (r == 2 * c + 1).astype(BF16)
    for k in range(x.shape[1] // DEINT_GROUP):
        blk = x[:, DEINT_GROUP * k:DEINT_GROUP * (k + 1)]
        wg_o[0, :, LANES * k:LANES * (k + 1)] = jnp.dot(blk, sel_even, preferred_element_type=F32).astype(BF16)
        wu_o[0, :, LANES * k:LANES * (k + 1)] = jnp.dot(blk, sel_odd, preferred_element_type=F32).astype(BF16)
    wd_o[...] = wd_ref[...].astype(BF16)


def _prep_expert_weights(w_gu, w_d, tn=512):
    n_e, d, f2 = w_gu.shape
    f = f2 // 2
    return pl.pallas_call(
        _prep_kernel,
        grid=(n_e, f // tn),
        in_specs=[
            pl.BlockSpec((1, d, 2 * tn), lambda e, j: (e, 0, j)),
            pl.BlockSpec((1, tn, d), lambda e, j: (e, j, 0)),
        ],
        out_specs=[
            pl.BlockSpec((1, d, tn), lambda e, j: (e, 0, j)),
            pl.BlockSpec((1, d, tn), lambda e, j: (e, 0, j)),
            pl.BlockSpec((1, tn, d), lambda e, j: (e, j, 0)),
        ],
        out_shape=[jax.ShapeDtypeStruct((n_e, d, f), BF16), jax.ShapeDtypeStruct((n_e, d, f), BF16),
                   jax.ShapeDtypeStruct((n_e, f, d), BF16)],
        compiler_params=_cparams(("parallel", "parallel")),
        name="expert_weight_prep",
    )(w_gu, w_d)


def _ffn_kernel(be_ref, rows_ref, nv_ref, inv_ref, u_hbm, wg_ref, wu_ref, wd_ref, bg_ref, bu_ref, bd_ref,
                y_hbm, xg_buf, xb_scr, acc_scr, o_buf, sem):
    i = pl.program_id(0)
    j = pl.program_id(1)
    nv = nv_ref[0]
    tm, d = xb_scr.shape
    half = d // 2
    trash0 = y_hbm.shape[0] // PACK_SUB - tm

    tf = wg_ref.shape[2]
    n_chunks = tf // MXU_TILE + d // MXU_TILE
    bounds = [tm * g // n_chunks for g in range(n_chunks + 1)]

    def gather_issue(blk, g=None):
        base = (blk + 1) * tm
        for r in range(tm) if g is None else range(bounds[g], bounds[g + 1]):
            tok = jnp.maximum(inv_ref[base + r], 0) >> 2
            pltpu.make_async_copy(_row_tile(u_hbm, tok), _row_tile(xg_buf, r), sem.at[0]).start()

    def gather_wait():
        pltpu.make_async_copy(_row_tile(u_hbm, 0, tm), xg_buf, sem.at[0]).wait()

    def scatter_issue(blk, g=None):
        base = (blk + 1) * tm
        for r in range(tm) if g is None else range(bounds[g], bounds[g + 1]):
            a = inv_ref[base + r]
            pltpu.make_async_copy(_row_tile(o_buf, r), _row_tile(y_hbm, jnp.where(a < 0, trash0 + r, a)),
                                  sem.at[1]).start(priority=r % 2)

    def scatter_wait():
        pltpu.make_async_copy(o_buf, _row_tile(y_hbm, 0, tm), sem.at[1]).wait()

    @pl.when(i < nv)
    def _():
        par = i % 2
        full = rows_ref[i] > tm // 2

        def tile(n, first, between):
            xb = xb_scr[0:n, :]
            g = 0
            acts = []
            for k in range(tf // MXU_TILE):
                cs = slice(MXU_TILE * k, MXU_TILE * (k + 1))
                gate = jnp.minimum(jnp.dot(xb, wg_ref[0, :, cs], preferred_element_type=F32) + bg_ref[0, :, cs],
                                   SWIGLU_LIMIT)
                up = jnp.clip(jnp.dot(xb, wu_ref[0, :, cs], preferred_element_type=F32) + bu_ref[0, :, cs],
                              -SWIGLU_LIMIT, SWIGLU_LIMIT)
                acts.append((gate * _sigmoid(gate * SWIGLU_ALPHA) * (up + 1.0)).astype(BF16))
                between(g)
                g += 1
            act = jnp.concatenate(acts, axis=1)
            for k in range(d // MXU_TILE):
                cs = slice(MXU_TILE * k, MXU_TILE * (k + 1))
                y = jnp.dot(act, wd_ref[0, :, cs], preferred_element_type=F32)
                if first:
                    acc_scr[par, 0:n, cs] = y + bd_ref[0, :, cs]
                    if n < tm:
                        acc_scr[par, n:tm, cs] = jnp.zeros((tm - n, MXU_TILE), F32)
                else:
                    acc_scr[par, 0:n, cs] += y
                between(g)
                g += 1

        def either_size(fn):
            @pl.when(full)
            def _():
                fn(tm)

            @pl.when(jnp.logical_not(full))
            def _():
                fn(tm // 2)

        @pl.when(j == 0)
        def _():
            @pl.when(i == 0)
            def _():
                acc_scr[1] = jnp.zeros((tm, d), F32)
                gather_issue(0)

            @pl.when(i > 0)
            def _():
                scatter_wait()

            gather_wait()
            for c in range(PACK_SUB):
                lo, hi = _unpack_chunk(xg_buf[pl.ds(c, tm, stride=PACK_SUB), :])
                xb_scr[:, LANES * c:LANES * (c + 1)] = lo.astype(BF16)
                xb_scr[:, half + LANES * c:half + LANES * (c + 1)] = hi.astype(BF16)
            _pack_rows(acc_scr[1 - par], o_buf)
            nxt = jnp.minimum(i + 1, nv - 1)
            either_size(lambda n: tile(n, True, lambda g: gather_issue(nxt, g)))

        @pl.when(j == 1)
        def _():
            either_size(lambda n: tile(n, False, lambda g: scatter_issue(i - 1, g)))

            @pl.when(i == nv - 1)
            def _():
                scatter_wait()
                _pack_rows(acc_scr[par], o_buf)
                scatter_issue(i)
                scatter_wait()
                gather_wait()


def _ffn(u_packed, inv, block_e, block_rows, n_valid, w_g, w_u, w_d, b_g, b_u, b_d, tm=FFN_ROWS):
    n_e, f, d = w_d.shape
    nf = 2
    tf = f // nf
    n_blocks = inv.shape[0] // tm - 1
    n_asg = (u_packed.shape[0] // PACK_SUB) * TOP_K
    ic = lambda i, nv: jnp.minimum(i, nv[0] - 1)
    jc = lambda i, j, nv: jnp.where(i < nv[0], j, nf - 1)
    col = lambda i, j, be, br, nv, iv: (be[ic(i, nv)], 0, jc(i, j, nv))
    return pl.pallas_call(
        _ffn_kernel,
        grid_spec=pltpu.PrefetchScalarGridSpec(
            num_scalar_prefetch=4,
            grid=(n_blocks, nf),
            in_specs=[
                pl.BlockSpec(memory_space=pl.ANY),
                pl.BlockSpec((1, d, tf), col),
                pl.BlockSpec((1, d, tf), col),
                pl.BlockSpec((1, tf, d), lambda i, j, be, br, nv, iv: (be[ic(i, nv)], jc(i, j, nv), 0)),
                pl.BlockSpec((1, 1, tf), col),
                pl.BlockSpec((1, 1, tf), col),
                pl.BlockSpec((1, 1, d), lambda i, j, be, br, nv, iv: (be[ic(i, nv)], 0, 0)),
            ],
            out_specs=pl.BlockSpec(memory_space=pl.ANY),
            scratch_shapes=[
                pltpu.VMEM((tm * PACK_SUB, LANES), jnp.uint32),
                pltpu.VMEM((tm, d), BF16),
                pltpu.VMEM((2, tm, d), F32),
                pltpu.VMEM((tm * PACK_SUB, LANES), jnp.uint32),
                pltpu.SemaphoreType.DMA((2,)),
            ],
        ),
        out_shape=jax.ShapeDtypeStruct(((n_asg + tm) * PACK_SUB, LANES), jnp.uint32),
        compiler_params=_cparams(("arbitrary", "arbitrary")),
        name="expert_ffn",
    )(block_e, block_rows, n_valid, inv, u_packed, w_g, w_u, w_d, b_g.reshape(n_e, 1, f),
      b_u.reshape(n_e, 1, f), b_d.reshape(n_e, 1, d))


COMBINE_ROWS = 256


def _combine_kernel(y_ref, tg_ref, h1_ref, p_ref, wpg_ref, wpp_ref, gp_ref, gpp_ref, gfin_ref, o_ref):
    tc = tg_ref.shape[0]
    tg = tg_ref[...]
    gks = [jnp.broadcast_to(tg[:, kk:kk + 1], (tc, LANES)) for kk in range(TOP_K)]
    lo_parts, hi_parts = [], []
    for c in range(PACK_SUB):
        lo_sum = hi_sum = None
        for kk in range(TOP_K):
            lo, hi = _unpack_chunk(y_ref[pl.ds(PACK_SUB * kk + c, tc, stride=PACK_SUB * TOP_K), :])
            gk = gks[kk]
            lo_sum = gk * lo if lo_sum is None else lo_sum + gk * lo
            hi_sum = gk * hi if hi_sum is None else hi_sum + gk * hi
        lo_parts.append(lo_sum)
        hi_parts.append(hi_sum)
    moe = jnp.concatenate(lo_parts + hi_parts, axis=1)
    h2 = h1_ref[...] + moe
    gate = _sigmoid(_bdot(_rms(h2, gp_ref[...]), wpg_ref[...]))
    pe = _rms(_bdot(p_ref[...], wpp_ref[...]), gpp_ref[...])
    h3 = h2 + gate * pe
    o_ref[...] = _rms(h3, gfin_ref[...])


def _combine(y_tok, tg, h1, p2, w_pg, w_pp, g_ple, g_ple_post, g_final, tc=COMBINE_ROWS):
    t, d = h1.shape
    pd = p2.shape[1]
    row = lambda w: pl.BlockSpec((tc, w), lambda i: (i, 0))
    const = lambda shp: pl.BlockSpec(shp, lambda i: (0, 0))
    return pl.pallas_call(
        _combine_kernel,
        grid=(t // tc,),
        in_specs=[
            pl.BlockSpec((tc * TOP_K * PACK_SUB, LANES), lambda i: (i, 0)),
            row(LANES), row(d), row(pd),
            const((d, d)), const((pd, d)), const((1, d)), const((1, d)), const((1, d)),
        ],
        out_specs=row(d),
        out_shape=jax.ShapeDtypeStruct((t, d), F32),
        compiler_params=_cparams(("parallel",)),
        name="combine_ple",
    )(y_tok, tg, h1, p2, w_pg, w_pp, g_ple.reshape(1, d), g_ple_post.reshape(1, d), g_final.reshape(1, d))


def _routing_tables(top_i, n_tok, tm=FFN_ROWS):
    n_asg = n_tok * TOP_K
    e_flat = top_i.reshape(n_asg)
    onehot = (e_flat[:, None] == jnp.arange(N_EXPERTS, dtype=jnp.int32)[None, :]).astype(jnp.int32)
    grp = onehot.reshape(n_asg // LANES, LANES, N_EXPERTS).astype(F32)
    within = jnp.einsum("ij,gjk->gik", jnp.tril(jnp.ones((LANES, LANES), F32)), grp).astype(jnp.int32)
    tot = within[:, -1, :]
    csum = (within + (jnp.cumsum(tot, axis=0) - tot)[:, None, :]).reshape(n_asg, N_EXPERTS)
    counts = csum[-1]
    padded = (counts + tm - 1) // tm * tm
    pend = jnp.cumsum(padded)
    pstart = pend - padded
    pos = jnp.sum(onehot * (csum - 1 + pstart[None, :]), axis=1).astype(jnp.int32)
    n_blocks = (n_tok * TOP_K) // tm + N_EXPERTS
    block_start = jnp.arange(n_blocks, dtype=jnp.int32) * tm
    block_e = jnp.minimum(jnp.sum(pend[None, :] <= block_start[:, None], axis=1), N_EXPERTS - 1).astype(jnp.int32)
    block_rows = jnp.clip((pstart + counts)[block_e] - block_start, 0, tm).astype(jnp.int32)
    n_valid = (pend[-1] // tm).astype(jnp.int32).reshape(1)
    pad_lo = (pstart + counts).astype(jnp.int32)
    return pos, pad_lo, pend.astype(jnp.int32), block_e, block_rows, n_valid, n_blocks * tm


def kernel(x, p, g_mix, w_in, conv_w, a_log, dt_bias, gdn_norm, ml_b_i, ml_b_f, ml_norm, w_out, g_ffn, w_router, b_router, w_gu, b_gu, w_down, b_down, g_ple, w_ple_gate, w_ple_proj, g_ple_post, g_final):
    b, s, d = x.shape
    n_tok = b * s
    assert w_in.shape[0] == 1, "single-layer block: the final norm is fused into the layer's last kernel"
    l = 0
    x2 = x.reshape(n_tok, d)
    w_main, w_gate = _prep_in_weights(w_in[l])
    z_main, z_gate = _in_proj(x2, g_mix[l], w_main, w_gate)
    z_main = z_main.reshape(b, s, Z_MAIN)
    z_gate = z_gate.reshape(b, s, LANES)
    y_gdn = _gdn(z_main, z_gate, conv_w[l], a_log[l], dt_bias[l], gdn_norm[l])
    y_ml = _mlstm(z_main, z_gate, ml_b_i[l], ml_b_f[l], ml_norm[l])

    w_r = jnp.pad(w_router[l], ((0, 0), (0, LANES - N_EXPERTS)))
    b_r = jnp.pad(b_router[l], (0, LANES - N_EXPERTS)).reshape(1, LANES)
    h1, u2, top_i, top_g = _out_proj(x2, y_gdn.reshape(n_tok, GDN_V), y_ml.reshape(n_tok, ML_V),
                                     w_out[l].astype(BF16), g_ffn[l], w_r, b_r)

    pos, pad_lo, pend, block_e, block_rows, n_valid, n_rows = _routing_tables(top_i[:, :TOP_K], n_tok)
    inv = _invert_slots(pos, pad_lo, pend, pend[-1], n_rows)
    w_g, w_u, w_d = _prep_expert_weights(w_gu[l], w_down[l])
    y_tok = _ffn(u2, inv, block_e, block_rows, n_valid, w_g, w_u, w_d,
                 b_gu[l][:, 0::2], b_gu[l][:, 1::2], b_down[l])
    out = _combine(y_tok, top_g, h1, p[l].reshape(n_tok, -1), w_ple_gate[l].astype(BF16),
                   w_ple_proj[l].astype(BF16), g_ple[l], g_ple_post[l], g_final)
    return out.reshape(b, s, d)
```

```python
import jax
import jax.numpy as jnp
from jax import lax
from jax.experimental import pallas as pl
from jax.experimental.pallas import tpu as pltpu

F32 = jnp.float32
BF16 = jnp.bfloat16

EPS = 1e-6
CHUNK = 64
GDN_HEADS = 8
GDN_DK = 128
GDN_DV = 128
GDN_CONV = 4
ML_HEADS = 4
ML_DQK = 128
ML_DV = 256
GATE_SOFTCAP = 15.0
N_EXPERTS = 32
TOP_K = 4
SWIGLU_LIMIT = 7.0
SWIGLU_ALPHA = 1.702

LANES = 128
NEG_BIG = -1e30

GDN_QK = GDN_HEADS * GDN_DK
GDN_V = GDN_HEADS * GDN_DV
ML_QK = ML_HEADS * ML_DQK
ML_V = ML_HEADS * ML_DV
OFF_Q = 0
OFF_K = OFF_Q + GDN_QK
OFF_V = OFF_K + GDN_QK
OFF_GZ = OFF_V + GDN_V
OFF_MQ = OFF_GZ + GDN_V
OFF_MK = OFF_MQ + ML_QK
OFF_MV = OFF_MK + ML_QK
OFF_MO = OFF_MV + ML_V
Z_MAIN = OFF_MO + ML_V
GATE_A = 0
GATE_B = GATE_A + GDN_HEADS
GATE_I = GATE_B + GDN_HEADS
GATE_F = GATE_I + ML_HEADS

GDN_TIME_BLOCK = 128
ML_TIME_BLOCK = 256
GDN_HEADS_PER_STEP = 8
ML_HEADS_PER_STEP = 4
VMEM_LIMIT = 56 * 1024 * 1024


def _cparams(sem):
    return pltpu.CompilerParams(dimension_semantics=sem, vmem_limit_bytes=VMEM_LIMIT)


def _bdot(a, b):
    return jnp.dot(a.astype(BF16), b.astype(BF16), preferred_element_type=F32)


def _bdot_nt(a, b):
    return lax.dot_general(a.astype(BF16), b.astype(BF16), (((1,), (1,)), ((), ())),
                           preferred_element_type=F32)


def _bdot_tn(a, b):
    return lax.dot_general(a.astype(BF16), b.astype(BF16), (((0,), (0,)), ((), ())),
                           preferred_element_type=F32)


def _split3(x):
    hi = x.astype(BF16)
    r1 = x - hi.astype(F32)
    mid = r1.astype(BF16)
    lo = (r1 - mid.astype(F32)).astype(BF16)
    return hi, mid, lo


def _dot_sel(sel_bf16, x):
    hi, mid, lo = _split3(x)
    d = lambda t: jnp.dot(sel_bf16, t, preferred_element_type=F32)
    return d(hi) + (d(mid) + d(lo))


def _sigmoid(x):
    return 1.0 / (1.0 + jnp.exp(-x))


def _softplus(x):
    return jnp.maximum(x, 0.0) + jnp.log1p(jnp.exp(-jnp.abs(x)))


def _rms(x, g):
    return x * lax.rsqrt(jnp.mean(x * x, axis=-1, keepdims=True) + EPS) * g


def _lane_pick(zg, idx):
    lane = lax.broadcasted_iota(jnp.int32, zg.shape, 1)
    return jnp.sum(jnp.where(lane == idx, zg, 0.0), axis=1, keepdims=True)


def _on_lanes(v, start):
    return jnp.pad(v.astype(F32), (start, LANES - start - v.shape[0])).reshape(1, LANES)


def _chunk_masks(n):
    r = lax.broadcasted_iota(jnp.int32, (n, n), 0)
    c = lax.broadcasted_iota(jnp.int32, (n, n), 1)
    same = lambda s: (r >> s) == (c >> s)
    return r, c, same


def _chunk_last(gc):
    n = gc.shape[0] // CHUNK
    parts = [jnp.broadcast_to(gc[CHUNK * (i + 1) - 1:CHUNK * (i + 1), :], (CHUNK, gc.shape[1]))
             for i in range(n)]
    return jnp.concatenate(parts, axis=0)


def _inproj_kernel(x_ref, g_ref, w_ref, wg_ref, z_ref, zg_ref, u_scr):
    @pl.when(pl.program_id(1) == 0)
    def _():
        u = _rms(x_ref[...], g_ref[...]).astype(BF16)
        u_scr[...] = u
        zg_ref[...] = jnp.dot(u, wg_ref[...], preferred_element_type=F32)

    z_ref[...] = jnp.dot(u_scr[...], w_ref[...], preferred_element_type=F32).astype(z_ref.dtype)


def _in_proj(x2, g_mix, w_main, w_gate, tm=1024, tn=1024):
    t, d = x2.shape
    n = w_main.shape[1]
    return pl.pallas_call(
        _inproj_kernel,
        grid=(t // tm, n // tn),
        in_specs=[
            pl.BlockSpec((tm, d), lambda i, j: (i, 0)),
            pl.BlockSpec((1, d), lambda i, j: (0, 0)),
            pl.BlockSpec((d, tn), lambda i, j: (0, j)),
            pl.BlockSpec((d, LANES), lambda i, j: (0, 0)),
        ],
        out_specs=[
            pl.BlockSpec((tm, tn), lambda i, j: (i, j)),
            pl.BlockSpec((tm, LANES), lambda i, j: (i, 0)),
        ],
        out_shape=[jax.ShapeDtypeStruct((t, n), BF16), jax.ShapeDtypeStruct((t, LANES), F32)],
        scratch_shapes=[pltpu.VMEM((tm, d), BF16)],
        compiler_params=_cparams(("parallel", "arbitrary")),
        name="in_proj",
    )(x2, g_mix.reshape(1, d), w_main, w_gate)


def _prep_in_weights(w_in):
    splits = (GDN_QK, GDN_QK, GDN_V, GDN_HEADS, GDN_HEADS, GDN_V, ML_QK, ML_QK, ML_V, ML_HEADS, ML_HEADS, ML_V)
    offs = [0]
    for wd in splits:
        offs.append(offs[-1] + wd)
    part = lambda i: w_in[:, offs[i]:offs[i + 1]]
    w_main = jnp.concatenate([part(i) for i in (0, 1, 2, 5, 6, 7, 8, 11)], axis=1).astype(BF16)
    gates = jnp.concatenate([part(i) for i in (3, 4, 9, 10)], axis=1)
    w_gate = jnp.pad(gates, ((0, 0), (0, LANES - gates.shape[1]))).astype(BF16)
    return w_main, w_gate


def _tri_inverse_minus_eye(ms, same):
    m16 = same(4)
    m32 = same(5)
    n32_mask = jnp.logical_and(m32, jnp.logical_not(m16))
    a = [jnp.where(m16, -m, 0.0) for m in ms]
    acc = list(a)
    for _ in range(3):
        a = [_bdot(x, x) for x in a]
        acc = [p + x + _bdot(p, x) for p, x in zip(acc, a)]
    for level in range(2):
        ns = [jnp.where(n32_mask, m, 0.0) if level == 0 else jnp.where(m32, 0.0, m) for m in ms]
        ys = [n + _bdot(p, n) for p, n in zip(acc, ns)]
        acc = [p - (y + _bdot(y, p)) for p, y in zip(acc, ys)]
    return acc


def _gdn_kernel(alog_ref, dtb_ref, zq_ref, zk_ref, zv_ref, gz_ref, zg_ref, cwq_ref, cwk_ref, cwv_ref,
                gn_ref, o_ref, s_scr, xq_scr, xk_scr, xv_scr):
    hb = s_scr.shape[0]
    h0 = pl.program_id(1) * hb
    tb = zq_ref.shape[1]
    nchunk = tb // CHUNK
    heads = range(hb)
    lanes = lambda hh: slice(LANES * hh, LANES * (hh + 1))

    @pl.when(pl.program_id(2) == 0)
    def _():
        s_scr[...] = jnp.zeros_like(s_scr)
        for scr in (xq_scr, xk_scr, xv_scr):
            scr[0:8, :] = jnp.zeros((8, scr.shape[1]), F32)

    def conv_silu(z_ref, x_scr, cw_ref):
        x = z_ref[0].astype(F32)
        x_scr[8:8 + tb, :] = x
        w = cw_ref[...]
        acc = x * w[GDN_CONV - 1:GDN_CONV, :]
        for s in range(1, GDN_CONV):
            acc = acc + x_scr[pl.ds(8 - s, tb), :] * w[GDN_CONV - 1 - s:GDN_CONV - s, :]
        x_scr[0:8, :] = x[tb - 8:tb, :]
        return acc * _sigmoid(acc)

    q_all = conv_silu(zq_ref, xq_scr, cwq_ref)
    k_all = conv_silu(zk_ref, xk_scr, cwk_ref)
    v_all = conv_silu(zv_ref, xv_scr, cwv_ref)
    gz_all = gz_ref[0].astype(F32)
    zg = zg_ref[0]
    g_all = -jnp.exp(alog_ref[...]) * _softplus(zg + dtb_ref[...])
    beta_all = _sigmoid(zg)

    r, c, same = _chunk_masks(tb)
    in_chunk = same(6)
    tril = jnp.logical_and(in_chunk, r >= c)
    strict = jnp.logical_and(in_chunk, r > c)
    tril_bf = tril.astype(BF16)

    def head_prep(hh):
        q = q_all[:, lanes(hh)]
        k = k_all[:, lanes(hh)]
        v = v_all[:, lanes(hh)]
        q = q * lax.rsqrt(jnp.sum(q * q, axis=-1, keepdims=True) + EPS) * (GDN_DK ** -0.5)
        k = k * lax.rsqrt(jnp.sum(k * k, axis=-1, keepdims=True) + EPS)
        beta = _lane_pick(beta_all, GATE_B + h0 + hh)
        g = _lane_pick(g_all, GATE_A + h0 + hh)
        gcb = _dot_sel(tril_bf, jnp.broadcast_to(g, (tb, LANES)))
        gc_row = gcb.T[0:1, :]
        gc_col = jnp.concatenate([gcb] * (tb // LANES), axis=1)
        decay = jnp.exp(jnp.where(tril, gc_col - gc_row, NEG_BIG))
        eg = jnp.exp(gcb)
        kb = k * beta
        m_low = jnp.where(strict, _bdot_nt(kb, k) * decay, 0.0)
        rhs = jnp.concatenate([v * beta, kb * eg], axis=1)
        attn = _bdot_nt(q, k) * decay
        kd = k * jnp.exp(_chunk_last(gcb) - gcb)
        return dict(m_low=m_low, rhs=rhs, attn=attn, qd=q * eg, kd=kd, gcb=gcb)

    hp = [head_prep(hh) for hh in heads]
    t_m1 = _tri_inverse_minus_eye([p["m_low"] for p in hp], same)
    uws = [p["rhs"] + _bdot(t, p["rhs"]) for p, t in zip(hp, t_m1)]

    states = [s_scr[hh] for hh in heads]
    outs = [[] for _ in heads]
    for i in range(nchunk):
        sl = slice(CHUNK * i, CHUNK * (i + 1))
        for hh in heads:
            p, uw = hp[hh], uws[hh]
            res1 = _bdot(jnp.concatenate([uw[sl, GDN_DV:], p["qd"][sl]], axis=0), states[hh])
            v_new = uw[sl, :GDN_DV] - res1[:CHUNK]
            o_intra = _bdot(p["attn"][sl, CHUNK * i:CHUNK * (i + 1)], v_new)
            outs[hh].append(res1[CHUNK:] + o_intra)
            g_last = jnp.exp(p["gcb"][CHUNK * (i + 1) - 1:CHUNK * (i + 1), :])
            states[hh] = states[hh] * g_last[:, 0:1] + _bdot_tn(p["kd"][sl], v_new)

    for hh in heads:
        s_scr[hh] = states[hh]
        o = _rms(jnp.concatenate(outs[hh], axis=0), gn_ref[...])
        gz = gz_all[:, lanes(hh)]
        o_ref[0, :, lanes(hh)] = (o * (gz * _sigmoid(gz))).astype(o_ref.dtype)


def _gdn(z_main, z_gate, conv_w, a_log, dt_bias, gdn_norm, tb=GDN_TIME_BLOCK, hb=GDN_HEADS_PER_STEP):
    b, s, _ = z_main.shape
    wid = hb * LANES
    hq, hk, hv, hz = OFF_Q // wid, OFF_K // wid, OFF_V // wid, OFF_GZ // wid
    zspec = lambda off: pl.BlockSpec((1, tb, wid), lambda bi, hi, ti: (bi, ti, off + hi))
    cspec = lambda off: pl.BlockSpec((GDN_CONV, wid), lambda bi, hi, ti: (0, off + hi))
    lane_row = pl.BlockSpec((1, LANES), lambda bi, hi, ti: (0, 0))
    return pl.pallas_call(
        _gdn_kernel,
        grid=(b, GDN_HEADS // hb, s // tb),
        in_specs=[
            lane_row, lane_row,
            zspec(hq), zspec(hk), zspec(hv), zspec(hz),
            pl.BlockSpec((1, tb, LANES), lambda bi, hi, ti: (bi, ti, 0)),
            cspec(hq), cspec(hk), cspec(hv),
            pl.BlockSpec((1, GDN_DV), lambda bi, hi, ti: (0, 0)),
        ],
        out_specs=pl.BlockSpec((1, tb, wid), lambda bi, hi, ti: (bi, ti, hi)),
        out_shape=jax.ShapeDtypeStruct((b, s, GDN_V), BF16),
        scratch_shapes=[
            pltpu.VMEM((hb, GDN_DK, GDN_DV), F32),
            pltpu.VMEM((8 + tb, wid), F32),
            pltpu.VMEM((8 + tb, wid), F32),
            pltpu.VMEM((8 + tb, wid), F32),
        ],
        compiler_params=_cparams(("parallel", "parallel", "arbitrary")),
        name="gdn",
    )(_on_lanes(a_log, GATE_A), _on_lanes(dt_bias, GATE_A), z_main, z_main, z_main, z_main, z_gate,
      conv_w, conv_w, conv_w, gdn_norm.reshape(1, GDN_DV))


def _mlstm_kernel(bias_ref, q_ref, k_ref, v_ref, og_ref, zg_ref, nrm_ref, o_ref, c_scr, m_scr):
    hb = c_scr.shape[0]
    h0 = pl.program_id(1) * hb
    tb = q_ref.shape[1]
    nchunk = tb // CHUNK
    cap = GATE_SOFTCAP

    @pl.when(pl.program_id(2) == 0)
    def _():
        c_scr[...] = jnp.zeros_like(c_scr)
        m_scr[...] = jnp.zeros_like(m_scr)

    capped = cap * jnp.tanh((zg_ref[0] + bias_ref[...]) / cap)
    logf_all = -_softplus(-capped)
    r, c, same = _chunk_masks(tb)
    tril = jnp.logical_and(same(6), r >= c)
    tril_bf = tril.astype(BF16)
    ones = jnp.ones((tb, LANES), F32)

    for hh in range(hb):
        q = q_ref[0, :, ML_DQK * hh:ML_DQK * (hh + 1)].astype(F32)
        k = k_ref[0, :, ML_DQK * hh:ML_DQK * (hh + 1)].astype(F32) * (ML_DQK ** -0.5)
        v = v_ref[0, :, ML_DV * hh:ML_DV * (hh + 1)].astype(F32)
        i_pre = _lane_pick(capped, GATE_I + h0 + hh)
        log_f = _lane_pick(logf_all, GATE_F + h0 + hh)

        fcb = _dot_sel(tril_bf, jnp.broadcast_to(log_f, (tb, LANES)))
        fmi = fcb - i_pre
        fmi_row = fmi.T[0:1, :]
        fc_col = jnp.concatenate([fcb] * (tb // LANES), axis=1)
        d_mat = jnp.where(tril, fc_col - fmi_row, NEG_BIG)
        d_max = jnp.max(d_mat, axis=-1, keepdims=True)
        a_end = _chunk_last(fcb) - fmi

        m_st = m_scr[hh]
        m_rows, decs, wks = [], [], []
        for i in range(nchunk):
            sl = slice(CHUNK * i, CHUNK * (i + 1))
            f_last = fcb[CHUNK * (i + 1) - 1:CHUNK * (i + 1), :]
            a_max = jnp.max(a_end[sl], axis=0, keepdims=True)
            m_new = jnp.maximum(f_last + m_st, a_max)
            m_rows.append(jnp.broadcast_to(m_st, (CHUNK, LANES)))
            decs.append(jnp.exp(f_last + m_st - m_new))
            wks.append(jnp.exp(a_end[sl] - m_new))
            m_st = m_new
        m_scr[hh] = m_st

        inter = fcb + jnp.concatenate(m_rows, axis=0)
        m_i = jnp.maximum(d_max, inter)
        s_inter = jnp.exp(inter - m_i)
        wts = jnp.exp(d_mat - m_i[:, 0:1]) * _bdot_nt(q, k)
        v_aug = jnp.concatenate([v, ones], axis=1)
        intra = _bdot(wts, v_aug)

        cst = c_scr[hh]
        nums = []
        for i in range(nchunk):
            sl = slice(CHUNK * i, CHUNK * (i + 1))
            nums.append(s_inter[sl, 0:1] * _bdot(q[sl], cst) + intra[sl])
            cst = decs[i][:, 0:1] * cst + _bdot_tn(wks[i] * k[sl], v_aug[sl])
        c_scr[hh] = cst

        num_aug = jnp.concatenate(nums, axis=0)
        den = num_aug[:, ML_DV:ML_DV + 1]
        hout = num_aug[:, :ML_DV] / jnp.maximum(jnp.abs(den), jnp.exp(-m_i[:, 0:1]))
        hout = _rms(hout, nrm_ref[hh])
        og = og_ref[0, :, ML_DV * hh:ML_DV * (hh + 1)].astype(F32)
        o_ref[0, :, ML_DV * hh:ML_DV * (hh + 1)] = (hout * _sigmoid(og)).astype(o_ref.dtype)


def _mlstm(z_main, z_gate, ml_b_i, ml_b_f, ml_norm, tb=ML_TIME_BLOCK, hb=ML_HEADS_PER_STEP):
    b, s, _ = z_main.shape
    qw, vw = hb * ML_DQK, hb * ML_DV
    hq, hk = OFF_MQ // qw, OFF_MK // qw
    hv, ho = OFF_MV // vw, OFF_MO // vw
    qspec = lambda off: pl.BlockSpec((1, tb, qw), lambda bi, hi, ti: (bi, ti, off + hi))
    vspec = lambda off: pl.BlockSpec((1, tb, vw), lambda bi, hi, ti: (bi, ti, off + hi))
    gate_bias = _on_lanes(ml_b_i, GATE_I) + _on_lanes(ml_b_f, GATE_F)
    return pl.pallas_call(
        _mlstm_kernel,
        grid=(b, ML_HEADS // hb, s // tb),
        in_specs=[
            pl.BlockSpec((1, LANES), lambda bi, hi, ti: (0, 0)),
            qspec(hq), qspec(hk), vspec(hv), vspec(ho),
            pl.BlockSpec((1, tb, LANES), lambda bi, hi, ti: (bi, ti, 0)),
            pl.BlockSpec((hb, 1, ML_DV), lambda bi, hi, ti: (hi, 0, 0)),
        ],
        out_specs=pl.BlockSpec((1, tb, vw), lambda bi, hi, ti: (bi, ti, hi)),
        out_shape=jax.ShapeDtypeStruct((b, s, ML_V), BF16),
        scratch_shapes=[
            pltpu.VMEM((hb, ML_DQK, ML_DV + LANES), F32),
            pltpu.VMEM((hb, 1, LANES), F32),
        ],
        compiler_params=_cparams(("parallel", "parallel", "arbitrary")),
        name="mlstm",
    )(gate_bias, z_main, z_main, z_main, z_main, z_gate, ml_norm.reshape(ML_HEADS, 1, ML_DV))


PACK_SUB = 8


def _pack_rows(x, o_ref):
    n, d = x.shape
    half = d // 2
    for j in range(PACK_SUB):
        lo = x[:, LANES * j:LANES * (j + 1)].astype(BF16).astype(F32)
        hi = x[:, half + LANES * j:half + LANES * (j + 1)].astype(BF16).astype(F32)
        word = (lax.bitcast_convert_type(lo, jnp.uint32) >> 16) | lax.bitcast_convert_type(hi, jnp.uint32)
        o_ref[pl.ds(j, n, stride=PACK_SUB), :] = word


def _unpack_chunk(word):
    lo = lax.bitcast_convert_type(word << 16, F32)
    hi = lax.bitcast_convert_type(word & jnp.uint32(0xFFFF0000), F32)
    return lo, hi


def _row_tile(ref, row, n=1):
    start = row * PACK_SUB
    if not isinstance(start, int):
        start = pl.multiple_of(start, PACK_SUB)
    return ref.at[pl.ds(start, n * PACK_SUB), :]


def _outproj_kernel(x_ref, yg_ref, ym_ref, wo1_ref, wo2_ref, gf_ref, wr_ref, br_ref,
                    h_ref, u_ref, ti_ref, tg_ref):
    h1 = (x_ref[...] + jnp.dot(yg_ref[...], wo1_ref[...], preferred_element_type=F32)
          + jnp.dot(ym_ref[...], wo2_ref[...], preferred_element_type=F32))
    h_ref[...] = h1
    u = _rms(h1, gf_ref[...])
    _pack_rows(u, u_ref)

    u_hi = u.astype(BF16)
    u_lo = (u - u_hi.astype(F32)).astype(BF16)
    wr = wr_ref[...]
    w_hi = wr.astype(BF16)
    w_lo = (wr - w_hi.astype(F32)).astype(BF16)
    d = lambda a, b: jnp.dot(a, b, preferred_element_type=F32)
    logits = d(u_hi, w_hi) + (d(u_hi, w_lo) + d(u_lo, w_hi)) + br_ref[...]

    lane = lax.broadcasted_iota(jnp.int32, logits.shape, 1)
    lg = jnp.where(lane < N_EXPERTS, logits, NEG_BIG)
    vals, idxs = [], []
    for _ in range(TOP_K):
        m = jnp.max(lg, axis=1, keepdims=True)
        idx = jnp.min(jnp.where(lg == m, lane, LANES), axis=1, keepdims=True)
        vals.append(m)
        idxs.append(idx)
        lg = jnp.where(lane == idx, NEG_BIG, lg)
    es = [jnp.exp(vv - vals[0]) for vv in vals]
    tot = es[0] + es[1] + es[2] + es[3]
    ti = jnp.zeros(logits.shape, jnp.int32)
    tg = jnp.zeros(logits.shape, F32)
    for kk in range(TOP_K):
        ti = jnp.where(lane == kk, idxs[kk], ti)
        tg = jnp.where(lane == kk, es[kk] / tot, tg)
    ti_ref[...] = ti
    tg_ref[...] = tg


def _out_proj(x2, y_gdn, y_ml, w_out_bf, g_ffn, w_router_pad, b_router_pad, tm=512):
    t, d = x2.shape
    row = lambda w: pl.BlockSpec((tm, w), lambda i: (i, 0))
    const = lambda shp: pl.BlockSpec(shp, lambda i: (0, 0))
    return pl.pallas_call(
        _outproj_kernel,
        grid=(t // tm,),
        in_specs=[
            row(d), row(GDN_V), row(ML_V),
            pl.BlockSpec((GDN_V, d), lambda i: (0, 0)),
            pl.BlockSpec((ML_V, d), lambda i: (GDN_V // ML_V, 0)),
            const((1, d)), const((d, LANES)), const((1, LANES)),
        ],
        out_specs=[row(d), pl.BlockSpec((tm * PACK_SUB, LANES), lambda i: (i, 0)), row(LANES), row(LANES)],
        out_shape=[jax.ShapeDtypeStruct((t, d), F32), jax.ShapeDtypeStruct((t * PACK_SUB, LANES), jnp.uint32),
                   jax.ShapeDtypeStruct((t, LANES), jnp.int32), jax.ShapeDtypeStruct((t, LANES), F32)],
        compiler_params=_cparams(("parallel",)),
        name="out_proj_router",
    )(x2, y_gdn, y_ml, w_out_bf, w_out_bf, g_ffn.reshape(1, d), w_router_pad, b_router_pad)


FFN_ROWS = 512
INV_UNROLL = 8


def _invert_kernel(lo_ref, hi_ref, trips_ref, pos_ref, inv_ref):
    def fill8(t, c):
        for k in range(INV_UNROLL):
            inv_ref[t * INV_UNROLL + k] = -1
        return c

    def fill1(r, c):
        inv_ref[r] = -1
        return c

    def per_expert(e, c):
        lax.fori_loop(lo_ref[e], hi_ref[e], fill1, 0)
        return c

    def body(t, c):
        for k in range(INV_UNROLL):
            a = t * INV_UNROLL + k
            inv_ref[FFN_ROWS + pos_ref[a]] = a
        return c

    lax.fori_loop(0, FFN_ROWS // INV_UNROLL, fill8, 0)
    lax.fori_loop(trips_ref[0], trips_ref[1], fill8, 0)
    lax.fori_loop(0, lo_ref.shape[0], per_expert, 0)
    lax.fori_loop(0, trips_ref[2], body, 0)


def _invert_slots(pos_flat, pad_lo, pad_hi, rows_used, n_rows):
    smem = pl.BlockSpec(memory_space=pltpu.SMEM)
    n_inv = FFN_ROWS + n_rows
    trips = jnp.stack([(FFN_ROWS + rows_used) // INV_UNROLL, jnp.int32(n_inv // INV_UNROLL),
                       jnp.int32(pos_flat.shape[0] // INV_UNROLL)]).astype(jnp.int32)
    return pl.pallas_call(
        _invert_kernel,
        in_specs=[smem, smem, smem, smem],
        out_specs=smem,
        out_shape=jax.ShapeDtypeStruct((n_inv,), jnp.int32),
        name="invert_slots",
    )(FFN_ROWS + pad_lo, FFN_ROWS + pad_hi, trips, pos_flat)


DEINT_GROUP = 2 * LANES
MXU_TILE = 256


def _prep_kernel(wgu_ref, wg_o, wu_o):
    x = wgu_ref[0].astype(BF16)
    r = lax.broadcasted_iota(jnp.int32, (DEINT_GROUP, LANES), 0)
    c = lax.broadcasted_iota(jnp.int32, (DEINT_GROUP, LANES), 1)
    sel_even = (r == 2 * c).astype(BF16)
    sel_odd = (r == 2 * c + 1).astype(BF16)
    for k in range(x.shape[1] // DEINT_GROUP):
        blk = x[:, DEINT_GROUP * k:DEINT_GROUP * (k + 1)]
        wg_o[0, :, LANES * k:LANES * (k + 1)] = jnp.dot(blk, sel_even, preferred_element_type=F32).astype(BF16)
        wu_o[0, :, LANES * k:LANES * (k + 1)] = jnp.dot(blk, sel_odd, preferred_element_type=F32).astype(BF16)


def _prep_expert_weights(w_gu, tn=512):
    n_e, d, f2 = w_gu.shape
    f = f2 // 2
    return pl.pallas_call(
        _prep_kernel,
        grid=(n_e, f // tn),
        in_specs=[pl.BlockSpec((1, d, 2 * tn), lambda e, j: (e, 0, j))],
        out_specs=[
            pl.BlockSpec((1, d, tn), lambda e, j: (e, 0, j)),
            pl.BlockSpec((1, d, tn), lambda e, j: (e, 0, j)),
        ],
        out_shape=[jax.ShapeDtypeStruct((n_e, d, f), BF16), jax.ShapeDtypeStruct((n_e, d, f), BF16)],
        compiler_params=_cparams(("parallel", "parallel")),
        name="expert_weight_prep",
    )(w_gu)


def _ffn_kernel(be_ref, rows_ref, nv_ref, inv_ref, u_hbm, wg_ref, wu_ref, wd_ref, bg_ref, bu_ref, bd_ref,
                y_hbm, xg_buf, xb_scr, acc_scr, o_buf, sem):
    i = pl.program_id(0)
    j = pl.program_id(1)
    nv = nv_ref[0]
    tm, d = xb_scr.shape
    half = d // 2
    trash0 = y_hbm.shape[0] // PACK_SUB - tm

    tf = wg_ref.shape[2]
    n_chunks = tf // MXU_TILE + d // MXU_TILE
    bounds = [tm * g // n_chunks for g in range(n_chunks + 1)]

    def gather_issue(blk, g=None):
        base = (blk + 1) * tm
        for r in range(tm) if g is None else range(bounds[g], bounds[g + 1]):
            tok = jnp.maximum(inv_ref[base + r], 0) >> 2
            pltpu.make_async_copy(_row_tile(u_hbm, tok), _row_tile(xg_buf, r), sem.at[0]).start(priority=1)

    def gather_wait():
        pltpu.make_async_copy(_row_tile(u_hbm, 0, tm), xg_buf, sem.at[0]).wait()

    def scatter_issue(blk, g=None):
        base = (blk + 1) * tm
        for r in range(tm) if g is None else range(bounds[g], bounds[g + 1]):
            a = inv_ref[base + r]
            pltpu.make_async_copy(_row_tile(o_buf, r), _row_tile(y_hbm, jnp.where(a < 0, trash0 + r, a)),
                                  sem.at[1]).start(priority=r % 2)

    def scatter_wait():
        pltpu.make_async_copy(o_buf, _row_tile(y_hbm, 0, tm), sem.at[1]).wait()

    @pl.when(i < nv)
    def _():
        par = i % 2
        full = rows_ref[i] > tm // 2

        def tile(n, first, between):
            xb = xb_scr[0:n, :]
            g = 0
            acts = []
            for k in range(tf // MXU_TILE):
                cs = slice(MXU_TILE * k, MXU_TILE * (k + 1))
                gate = jnp.minimum(jnp.dot(xb, wg_ref[0, :, cs], preferred_element_type=F32) + bg_ref[0, :, cs],
                                   SWIGLU_LIMIT)
                up = jnp.clip(jnp.dot(xb, wu_ref[0, :, cs], preferred_element_type=F32) + bu_ref[0, :, cs],
                              -SWIGLU_LIMIT, SWIGLU_LIMIT)
                acts.append((gate * _sigmoid(gate * SWIGLU_ALPHA) * (up + 1.0)).astype(BF16))
                between(g)
                g += 1
            act = jnp.concatenate(acts, axis=1)
            for k in range(d // MXU_TILE):
                cs = slice(MXU_TILE * k, MXU_TILE * (k + 1))
                y = jnp.dot(act, wd_ref[0, :, cs].astype(BF16), preferred_element_type=F32)
                if first:
                    acc_scr[par, 0:n, cs] = y + bd_ref[0, :, cs]
                    if n < tm:
                        acc_scr[par, n:tm, cs] = jnp.zeros((tm - n, MXU_TILE), F32)
                else:
                    acc_scr[par, 0:n, cs] += y
                between(g)
                g += 1

        def either_size(fn):
            @pl.when(full)
            def _():
                fn(tm)

            @pl.when(jnp.logical_not(full))
            def _():
                fn(tm // 2)

        @pl.when(j == 0)
        def _():
            @pl.when(i == 0)
            def _():
                acc_scr[1] = jnp.zeros((tm, d), F32)
                gather_issue(0)

            @pl.when(i > 0)
            def _():
                scatter_wait()

            gather_wait()
            for c in range(PACK_SUB):
                lo, hi = _unpack_chunk(xg_buf[pl.ds(c, tm, stride=PACK_SUB), :])
                xb_scr[:, LANES * c:LANES * (c + 1)] = lo.astype(BF16)
                xb_scr[:, half + LANES * c:half + LANES * (c + 1)] = hi.astype(BF16)
            _pack_rows(acc_scr[1 - par], o_buf)
            nxt = jnp.minimum(i + 1, nv - 1)
            either_size(lambda n: tile(n, True, lambda g: gather_issue(nxt, g)))

        @pl.when(j == 1)
        def _():
            either_size(lambda n: tile(n, False, lambda g: scatter_issue(i - 1, g)))

            @pl.when(i == nv - 1)
            def _():
                scatter_wait()
                _pack_rows(acc_scr[par], o_buf)
                scatter_issue(i)
                scatter_wait()
                gather_wait()


def _ffn(u_packed, inv, block_e, block_rows, n_valid, w_g, w_u, w_d, b_g, b_u, b_d, tm=FFN_ROWS):
    n_e, f, d = w_d.shape
    nf = 2
    tf = f // nf
    n_blocks = inv.shape[0] // tm - 1
    n_asg = (u_packed.shape[0] // PACK_SUB) * TOP_K
    ic = lambda i, nv: jnp.minimum(i, nv[0] - 1)
    jc = lambda i, j, nv: jnp.where(i < nv[0], j, nf - 1)
    col = lambda i, j, be, br, nv, iv: (be[ic(i, nv)], 0, jc(i, j, nv))
    return pl.pallas_call(
        _ffn_kernel,
        grid_spec=pltpu.PrefetchScalarGridSpec(
            num_scalar_prefetch=4,
            grid=(n_blocks, nf),
            in_specs=[
                pl.BlockSpec(memory_space=pl.ANY),
                pl.BlockSpec((1, d, tf), col),
                pl.BlockSpec((1, d, tf), col),
                pl.BlockSpec((1, tf, d), lambda i, j, be, br, nv, iv: (be[ic(i, nv)], jc(i, j, nv), 0)),
                pl.BlockSpec((1, 1, tf), col),
                pl.BlockSpec((1, 1, tf), col),
                pl.BlockSpec((1, 1, d), lambda i, j, be, br, nv, iv: (be[ic(i, nv)], 0, 0)),
            ],
            out_specs=pl.BlockSpec(memory_space=pl.ANY),
            scratch_shapes=[
                pltpu.VMEM((tm * PACK_SUB, LANES), jnp.uint32),
                pltpu.VMEM((tm, d), BF16),
                pltpu.VMEM((2, tm, d), F32),
                pltpu.VMEM((tm * PACK_SUB, LANES), jnp.uint32),
                pltpu.SemaphoreType.DMA((2,)),
            ],
        ),
        out_shape=jax.ShapeDtypeStruct(((n_asg + tm) * PACK_SUB, LANES), jnp.uint32),
        compiler_params=_cparams(("arbitrary", "arbitrary")),
        name="expert_ffn",
    )(block_e, block_rows, n_valid, inv, u_packed, w_g, w_u, w_d, b_g.reshape(n_e, 1, f),
      b_u.reshape(n_e, 1, f), b_d.reshape(n_e, 1, d))


COMBINE_ROWS = 256


def _combine_kernel(y_ref, tg_ref, h1_ref, p_ref, wpg_ref, wpp_ref, gp_ref, gpp_ref, gfin_ref, o_ref):
    tc = tg_ref.shape[0]
    tg = tg_ref[...]
    gks = [jnp.broadcast_to(tg[:, kk:kk + 1], (tc, LANES)) for kk in range(TOP_K)]
    lo_parts, hi_parts = [], []
    for c in range(PACK_SUB):
        lo_sum = hi_sum = None
        for kk in range(TOP_K):
            lo, hi = _unpack_chunk(y_ref[pl.ds(PACK_SUB * kk + c, tc, stride=PACK_SUB * TOP_K), :])
            gk = gks[kk]
            lo_sum = gk * lo if lo_sum is None else lo_sum + gk * lo
            hi_sum = gk * hi if hi_sum is None else hi_sum + gk * hi
        lo_parts.append(lo_sum)
        hi_parts.append(hi_sum)
    moe = jnp.concatenate(lo_parts + hi_parts, axis=1)
    h2 = h1_ref[...] + moe
    gate = _sigmoid(_bdot(_rms(h2, gp_ref[...]), wpg_ref[...]))
    pe = _rms(_bdot(p_ref[...], wpp_ref[...]), gpp_ref[...])
    h3 = h2 + gate * pe
    o_ref[...] = _rms(h3, gfin_ref[...])


def _combine(y_tok, tg, h1, p2, w_pg, w_pp, g_ple, g_ple_post, g_final, tc=COMBINE_ROWS):
    t, d = h1.shape
    pd = p2.shape[1]
    row = lambda w: pl.BlockSpec((tc, w), lambda i: (i, 0))
    const = lambda shp: pl.BlockSpec(shp, lambda i: (0, 0))
    return pl.pallas_call(
        _combine_kernel,
        grid=(t // tc,),
        in_specs=[
            pl.BlockSpec((tc * TOP_K * PACK_SUB, LANES), lambda i: (i, 0)),
            row(LANES), row(d), row(pd),
            const((d, d)), const((pd, d)), const((1, d)), const((1, d)), const((1, d)),
        ],
        out_specs=row(d),
        out_shape=jax.ShapeDtypeStruct((t, d), F32),
        compiler_params=_cparams(("parallel",)),
        name="combine_ple",
    )(y_tok, tg, h1, p2, w_pg, w_pp, g_ple.reshape(1, d), g_ple_post.reshape(1, d), g_final.reshape(1, d))


def _routing_tables(top_i, n_tok, tm=FFN_ROWS):
    n_asg = n_tok * TOP_K
    e_flat = top_i.reshape(n_asg)
    onehot = (e_flat[:, None] == jnp.arange(N_EXPERTS, dtype=jnp.int32)[None, :]).astype(jnp.int32)
    grp = onehot.reshape(n_asg // LANES, LANES, N_EXPERTS).astype(F32)
    within = jnp.einsum("ij,gjk->gik", jnp.tril(jnp.ones((LANES, LANES), F32)), grp).astype(jnp.int32)
    tot = within[:, -1, :]
    csum = (within + (jnp.cumsum(tot, axis=0) - tot)[:, None, :]).reshape(n_asg, N_EXPERTS)
    counts = csum[-1]
    padded = (counts + tm - 1) // tm * tm
    pend = jnp.cumsum(padded)
    pstart = pend - padded
    pos = jnp.sum(onehot * (csum - 1 + pstart[None, :]), axis=1).astype(jnp.int32)
    n_blocks = (n_tok * TOP_K) // tm + N_EXPERTS
    block_start = jnp.arange(n_blocks, dtype=jnp.int32) * tm
    block_e = jnp.minimum(jnp.sum(pend[None, :] <= block_start[:, None], axis=1), N_EXPERTS - 1).astype(jnp.int32)
    block_rows = jnp.clip((pstart + counts)[block_e] - block_start, 0, tm).astype(jnp.int32)
    n_valid = (pend[-1] // tm).astype(jnp.int32).reshape(1)
    pad_lo = (pstart + counts).astype(jnp.int32)
    return pos, pad_lo, pend.astype(jnp.int32), block_e, block_rows, n_valid, n_blocks * tm


def kernel(x, p, g_mix, w_in, conv_w, a_log, dt_bias, gdn_norm, ml_b_i, ml_b_f, ml_norm, w_out, g_ffn, w_router, b_router, w_gu, b_gu, w_down, b_down, g_ple, w_ple_gate, w_ple_proj, g_ple_post, g_final):
    b, s, d = x.shape
    n_tok = b * s
    assert w_in.shape[0] == 1, "single-layer block: the final norm is fused into the layer's last kernel"
    l = 0
    x2 = x.reshape(n_tok, d)
    w_main, w_gate = _prep_in_weights(w_in[l])
    z_main, z_gate = _in_proj(x2, g_mix[l], w_main, w_gate)
    z_main = z_main.reshape(b, s, Z_MAIN)
    z_gate = z_gate.reshape(b, s, LANES)
    y_gdn = _gdn(z_main, z_gate, conv_w[l], a_log[l], dt_bias[l], gdn_norm[l])
    y_ml = _mlstm(z_main, z_gate, ml_b_i[l], ml_b_f[l], ml_norm[l])

    w_r = jnp.pad(w_router[l], ((0, 0), (0, LANES - N_EXPERTS)))
    b_r = jnp.pad(b_router[l], (0, LANES - N_EXPERTS)).reshape(1, LANES)
    h1, u2, top_i, top_g = _out_proj(x2, y_gdn.reshape(n_tok, GDN_V), y_ml.reshape(n_tok, ML_V),
                                     w_out[l].astype(BF16), g_ffn[l], w_r, b_r)

    pos, pad_lo, pend, block_e, block_rows, n_valid, n_rows = _routing_tables(top_i[:, :TOP_K], n_tok)
    inv = _invert_slots(pos, pad_lo, pend, pend[-1], n_rows)
    w_g, w_u = _prep_expert_weights(w_gu[l])
    y_tok = _ffn(u2, inv, block_e, block_rows, n_valid, w_g, w_u, w_down[l],
                 b_gu[l][:, 0::2], b_gu[l][:, 1::2], b_down[l])
    out = _combine(y_tok, top_g, h1, p[l].reshape(n_tok, -1), w_ple_gate[l].astype(BF16),
                   w_ple_proj[l].astype(BF16), g_ple[l], g_ple_post[l], g_final)
    return out.reshape(b, s, d)
```

```python
import jax
import jax.numpy as jnp
from jax import lax
from jax.experimental import pallas as pl
from jax.experimental.pallas import tpu as pltpu

F32 = jnp.float32
BF16 = jnp.bfloat16

EPS = 1e-6
CHUNK = 64
GDN_HEADS = 8
GDN_DK = 128
GDN_DV = 128
GDN_CONV = 4
ML_HEADS = 4
ML_DQK = 128
ML_DV = 256
GATE_SOFTCAP = 15.0
N_EXPERTS = 32
TOP_K = 4
SWIGLU_LIMIT = 7.0
SWIGLU_ALPHA = 1.702

LANES = 128
NEG_BIG = -1e30

GDN_QK = GDN_HEADS * GDN_DK
GDN_V = GDN_HEADS * GDN_DV
ML_QK = ML_HEADS * ML_DQK
ML_V = ML_HEADS * ML_DV
OFF_Q = 0
OFF_K = OFF_Q + GDN_QK
OFF_V = OFF_K + GDN_QK
OFF_GZ = OFF_V + GDN_V
OFF_MQ = OFF_GZ + GDN_V
OFF_MK = OFF_MQ + ML_QK
OFF_MV = OFF_MK + ML_QK
OFF_MO = OFF_MV + ML_V
Z_MAIN = OFF_MO + ML_V
GATE_A = 0
GATE_B = GATE_A + GDN_HEADS
GATE_I = GATE_B + GDN_HEADS
GATE_F = GATE_I + ML_HEADS

GDN_TIME_BLOCK = 128
ML_TIME_BLOCK = 256
GDN_HEADS_PER_STEP = 8
ML_HEADS_PER_STEP = 4
VMEM_LIMIT = 56 * 1024 * 1024


def _cparams(sem):
    return pltpu.CompilerParams(dimension_semantics=sem, vmem_limit_bytes=VMEM_LIMIT)


def _bdot(a, b):
    return jnp.dot(a.astype(BF16), b.astype(BF16), preferred_element_type=F32)


def _bdot_nt(a, b):
    return lax.dot_general(a.astype(BF16), b.astype(BF16), (((1,), (1,)), ((), ())),
                           preferred_element_type=F32)


def _bdot_tn(a, b):
    return lax.dot_general(a.astype(BF16), b.astype(BF16), (((0,), (0,)), ((), ())),
                           preferred_element_type=F32)


def _split3(x):
    hi = x.astype(BF16)
    r1 = x - hi.astype(F32)
    mid = r1.astype(BF16)
    lo = (r1 - mid.astype(F32)).astype(BF16)
    return hi, mid, lo


def _dot_sel(sel_bf16, x):
    hi, mid, lo = _split3(x)
    d = lambda t: jnp.dot(sel_bf16, t, preferred_element_type=F32)
    return d(hi) + (d(mid) + d(lo))


def _sigmoid(x):
    return 1.0 / (1.0 + jnp.exp(-x))


def _softplus(x):
    return jnp.maximum(x, 0.0) + jnp.log1p(jnp.exp(-jnp.abs(x)))


def _rms(x, g):
    return x * lax.rsqrt(jnp.mean(x * x, axis=-1, keepdims=True) + EPS) * g


def _lane_pick(zg, idx):
    lane = lax.broadcasted_iota(jnp.int32, zg.shape, 1)
    return jnp.sum(jnp.where(lane == idx, zg, 0.0), axis=1, keepdims=True)


def _on_lanes(v, start):
    return jnp.pad(v.astype(F32), (start, LANES - start - v.shape[0])).reshape(1, LANES)


def _chunk_masks(n):
    r = lax.broadcasted_iota(jnp.int32, (n, n), 0)
    c = lax.broadcasted_iota(jnp.int32, (n, n), 1)
    same = lambda s: (r >> s) == (c >> s)
    return r, c, same


def _chunk_last(gc):
    n = gc.shape[0] // CHUNK
    parts = [jnp.broadcast_to(gc[CHUNK * (i + 1) - 1:CHUNK * (i + 1), :], (CHUNK, gc.shape[1]))
             for i in range(n)]
    return jnp.concatenate(parts, axis=0)


def _inproj_kernel(x_ref, g_ref, w_ref, wg_ref, z_ref, zg_ref, u_scr):
    @pl.when(pl.program_id(1) == 0)
    def _():
        u = _rms(x_ref[...], g_ref[...]).astype(BF16)
        u_scr[...] = u
        zg_ref[...] = jnp.dot(u, wg_ref[...], preferred_element_type=F32)

    z_ref[...] = jnp.dot(u_scr[...], w_ref[...], preferred_element_type=F32).astype(z_ref.dtype)


def _in_proj(x2, g_mix, w_main, w_gate, tm=1024, tn=1024):
    t, d = x2.shape
    n = w_main.shape[1]
    return pl.pallas_call(
        _inproj_kernel,
        grid=(t // tm, n // tn),
        in_specs=[
            pl.BlockSpec((tm, d), lambda i, j: (i, 0)),
            pl.BlockSpec((1, d), lambda i, j: (0, 0)),
            pl.BlockSpec((d, tn), lambda i, j: (0, j)),
            pl.BlockSpec((d, LANES), lambda i, j: (0, 0)),
        ],
        out_specs=[
            pl.BlockSpec((tm, tn), lambda i, j: (i, j)),
            pl.BlockSpec((tm, LANES), lambda i, j: (i, 0)),
        ],
        out_shape=[jax.ShapeDtypeStruct((t, n), BF16), jax.ShapeDtypeStruct((t, LANES), F32)],
        scratch_shapes=[pltpu.VMEM((tm, d), BF16)],
        compiler_params=_cparams(("parallel", "arbitrary")),
        name="in_proj",
    )(x2, g_mix.reshape(1, d), w_main, w_gate)


def _prep_in_weights(w_in):
    splits = (GDN_QK, GDN_QK, GDN_V, GDN_HEADS, GDN_HEADS, GDN_V, ML_QK, ML_QK, ML_V, ML_HEADS, ML_HEADS, ML_V)
    offs = [0]
    for wd in splits:
        offs.append(offs[-1] + wd)
    part = lambda i: w_in[:, offs[i]:offs[i + 1]]
    w_main = jnp.concatenate([part(i) for i in (0, 1, 2, 5, 6, 7, 8, 11)], axis=1).astype(BF16)
    gates = jnp.concatenate([part(i) for i in (3, 4, 9, 10)], axis=1)
    w_gate = jnp.pad(gates, ((0, 0), (0, LANES - gates.shape[1]))).astype(BF16)
    return w_main, w_gate


def _tri_inverse_minus_eye(ms, same):
    m16 = same(4)
    m32 = same(5)
    n32_mask = jnp.logical_and(m32, jnp.logical_not(m16))
    a = [jnp.where(m16, -m, 0.0) for m in ms]
    acc = list(a)
    for _ in range(3):
        a = [_bdot(x, x) for x in a]
        acc = [p + x + _bdot(p, x) for p, x in zip(acc, a)]
    for level in range(2):
        ns = [jnp.where(n32_mask, m, 0.0) if level == 0 else jnp.where(m32, 0.0, m) for m in ms]
        ys = [n + _bdot(p, n) for p, n in zip(acc, ns)]
        acc = [p - (y + _bdot(y, p)) for p, y in zip(acc, ys)]
    return acc


def _gdn_kernel(alog_ref, dtb_ref, zq_ref, zk_ref, zv_ref, gz_ref, zg_ref, cwq_ref, cwk_ref, cwv_ref,
                gn_ref, o_ref, s_scr, xq_scr, xk_scr, xv_scr):
    hb = s_scr.shape[0]
    h0 = pl.program_id(1) * hb
    tb = zq_ref.shape[1]
    nchunk = tb // CHUNK
    heads = range(hb)
    lanes = lambda hh: slice(LANES * hh, LANES * (hh + 1))

    @pl.when(pl.program_id(2) == 0)
    def _():
        s_scr[...] = jnp.zeros_like(s_scr)
        for scr in (xq_scr, xk_scr, xv_scr):
            scr[0:8, :] = jnp.zeros((8, scr.shape[1]), F32)

    def conv_silu(z_ref, x_scr, cw_ref):
        x = z_ref[0].astype(F32)
        x_scr[8:8 + tb, :] = x
        w = cw_ref[...]
        acc = x * w[GDN_CONV - 1:GDN_CONV, :]
        for s in range(1, GDN_CONV):
            acc = acc + x_scr[pl.ds(8 - s, tb), :] * w[GDN_CONV - 1 - s:GDN_CONV - s, :]
        x_scr[0:8, :] = x[tb - 8:tb, :]
        return acc * _sigmoid(acc)

    q_all = conv_silu(zq_ref, xq_scr, cwq_ref)
    k_all = conv_silu(zk_ref, xk_scr, cwk_ref)
    v_all = conv_silu(zv_ref, xv_scr, cwv_ref)
    gz_all = gz_ref[0].astype(F32)
    zg = zg_ref[0]
    g_all = -jnp.exp(alog_ref[...]) * _softplus(zg + dtb_ref[...])
    beta_all = _sigmoid(zg)

    r, c, same = _chunk_masks(tb)
    in_chunk = same(6)
    tril = jnp.logical_and(in_chunk, r >= c)
    strict = jnp.logical_and(in_chunk, r > c)
    tril_bf = tril.astype(BF16)

    def head_prep(hh):
        q = q_all[:, lanes(hh)]
        k = k_all[:, lanes(hh)]
        v = v_all[:, lanes(hh)]
        q = q * lax.rsqrt(jnp.sum(q * q, axis=-1, keepdims=True) + EPS) * (GDN_DK ** -0.5)
        k = k * lax.rsqrt(jnp.sum(k * k, axis=-1, keepdims=True) + EPS)
        beta = _lane_pick(beta_all, GATE_B + h0 + hh)
        g = _lane_pick(g_all, GATE_A + h0 + hh)
        gcb = _dot_sel(tril_bf, jnp.broadcast_to(g, (tb, LANES)))
        gc_row = gcb.T[0:1, :]
        gc_col = jnp.concatenate([gcb] * (tb // LANES), axis=1)
        decay = jnp.exp(jnp.where(tril, gc_col - gc_row, NEG_BIG))
        eg = jnp.exp(gcb)
        kb = k * beta
        m_low = jnp.where(strict, _bdot_nt(kb, k) * decay, 0.0)
        rhs = jnp.concatenate([v * beta, kb * eg], axis=1)
        attn = _bdot_nt(q, k) * decay
        kd = k * jnp.exp(_chunk_last(gcb) - gcb)
        return dict(m_low=m_low, rhs=rhs, attn=attn, qd=q * eg, kd=kd, gcb=gcb)

    hp = [head_prep(hh) for hh in heads]
    t_m1 = _tri_inverse_minus_eye([p["m_low"] for p in hp], same)
    uws = [p["rhs"] + _bdot(t, p["rhs"]) for p, t in zip(hp, t_m1)]

    states = [s_scr[hh] for hh in heads]
    outs = [[] for _ in heads]
    for i in range(nchunk):
        sl = slice(CHUNK * i, CHUNK * (i + 1))
        for hh in heads:
            p, uw = hp[hh], uws[hh]
            res1 = _bdot(jnp.concatenate([uw[sl, GDN_DV:], p["qd"][sl]], axis=0), states[hh])
            v_new = uw[sl, :GDN_DV] - res1[:CHUNK]
            o_intra = _bdot(p["attn"][sl, CHUNK * i:CHUNK * (i + 1)], v_new)
            outs[hh].append(res1[CHUNK:] + o_intra)
            g_last = jnp.exp(p["gcb"][CHUNK * (i + 1) - 1:CHUNK * (i + 1), :])
            states[hh] = states[hh] * g_last[:, 0:1] + _bdot_tn(p["kd"][sl], v_new)

    for hh in heads:
        s_scr[hh] = states[hh]
        o = _rms(jnp.concatenate(outs[hh], axis=0), gn_ref[...])
        gz = gz_all[:, lanes(hh)]
        o_ref[0, :, lanes(hh)] = (o * (gz * _sigmoid(gz))).astype(o_ref.dtype)


def _gdn(z_main, z_gate, conv_w, a_log, dt_bias, gdn_norm, tb=GDN_TIME_BLOCK, hb=GDN_HEADS_PER_STEP):
    b, s, _ = z_main.shape
    wid = hb * LANES
    hq, hk, hv, hz = OFF_Q // wid, OFF_K // wid, OFF_V // wid, OFF_GZ // wid
    zspec = lambda off: pl.BlockSpec((1, tb, wid), lambda bi, hi, ti: (bi, ti, off + hi))
    cspec = lambda off: pl.BlockSpec((GDN_CONV, wid), lambda bi, hi, ti: (0, off + hi))
    lane_row = pl.BlockSpec((1, LANES), lambda bi, hi, ti: (0, 0))
    return pl.pallas_call(
        _gdn_kernel,
        grid=(b, GDN_HEADS // hb, s // tb),
        in_specs=[
            lane_row, lane_row,
            zspec(hq), zspec(hk), zspec(hv), zspec(hz),
            pl.BlockSpec((1, tb, LANES), lambda bi, hi, ti: (bi, ti, 0)),
            cspec(hq), cspec(hk), cspec(hv),
            pl.BlockSpec((1, GDN_DV), lambda bi, hi, ti: (0, 0)),
        ],
        out_specs=pl.BlockSpec((1, tb, wid), lambda bi, hi, ti: (bi, ti, hi)),
        out_shape=jax.ShapeDtypeStruct((b, s, GDN_V), BF16),
        scratch_shapes=[
            pltpu.VMEM((hb, GDN_DK, GDN_DV), F32),
            pltpu.VMEM((8 + tb, wid), F32),
            pltpu.VMEM((8 + tb, wid), F32),
            pltpu.VMEM((8 + tb, wid), F32),
        ],
        compiler_params=_cparams(("parallel", "parallel", "arbitrary")),
        name="gdn",
    )(_on_lanes(a_log, GATE_A), _on_lanes(dt_bias, GATE_A), z_main, z_main, z_main, z_main, z_gate,
      conv_w, conv_w, conv_w, gdn_norm.reshape(1, GDN_DV))


def _mlstm_kernel(bias_ref, q_ref, k_ref, v_ref, og_ref, zg_ref, nrm_ref, o_ref, c_scr, m_scr):
    hb = c_scr.shape[0]
    h0 = pl.program_id(1) * hb
    tb = q_ref.shape[1]
    nchunk = tb // CHUNK
    cap = GATE_SOFTCAP

    @pl.when(pl.program_id(2) == 0)
    def _():
        c_scr[...] = jnp.zeros_like(c_scr)
        m_scr[...] = jnp.zeros_like(m_scr)

    capped = cap * jnp.tanh((zg_ref[0] + bias_ref[...]) / cap)
    logf_all = -_softplus(-capped)
    r, c, same = _chunk_masks(tb)
    tril = jnp.logical_and(same(6), r >= c)
    tril_bf = tril.astype(BF16)
    ones = jnp.ones((tb, LANES), F32)

    for hh in range(hb):
        q = q_ref[0, :, ML_DQK * hh:ML_DQK * (hh + 1)].astype(F32)
        k = k_ref[0, :, ML_DQK * hh:ML_DQK * (hh + 1)].astype(F32) * (ML_DQK ** -0.5)
        v = v_ref[0, :, ML_DV * hh:ML_DV * (hh + 1)].astype(F32)
        i_pre = _lane_pick(capped, GATE_I + h0 + hh)
        log_f = _lane_pick(logf_all, GATE_F + h0 + hh)

        fcb = _dot_sel(tril_bf, jnp.broadcast_to(log_f, (tb, LANES)))
        fmi = fcb - i_pre
        fmi_row = fmi.T[0:1, :]
        fc_col = jnp.concatenate([fcb] * (tb // LANES), axis=1)
        d_mat = jnp.where(tril, fc_col - fmi_row, NEG_BIG)
        d_max = jnp.max(d_mat, axis=-1, keepdims=True)
        a_end = _chunk_last(fcb) - fmi

        m_st = m_scr[hh]
        m_rows, decs, wks = [], [], []
        for i in range(nchunk):
            sl = slice(CHUNK * i, CHUNK * (i + 1))
            f_last = fcb[CHUNK * (i + 1) - 1:CHUNK * (i + 1), :]
            a_max = jnp.max(a_end[sl], axis=0, keepdims=True)
            m_new = jnp.maximum(f_last + m_st, a_max)
            m_rows.append(jnp.broadcast_to(m_st, (CHUNK, LANES)))
            decs.append(jnp.exp(f_last + m_st - m_new))
            wks.append(jnp.exp(a_end[sl] - m_new))
            m_st = m_new
        m_scr[hh] = m_st

        inter = fcb + jnp.concatenate(m_rows, axis=0)
        m_i = jnp.maximum(d_max, inter)
        s_inter = jnp.exp(inter - m_i)
        wts = jnp.exp(d_mat - m_i[:, 0:1]) * _bdot_nt(q, k)
        v_aug = jnp.concatenate([v, ones], axis=1)
        intra = _bdot(wts, v_aug)

        cst = c_scr[hh]
        nums = []
        for i in range(nchunk):
            sl = slice(CHUNK * i, CHUNK * (i + 1))
            nums.append(s_inter[sl, 0:1] * _bdot(q[sl], cst) + intra[sl])
            cst = decs[i][:, 0:1] * cst + _bdot_tn(wks[i] * k[sl], v_aug[sl])
        c_scr[hh] = cst

        num_aug = jnp.concatenate(nums, axis=0)
        den = num_aug[:, ML_DV:ML_DV + 1]
        hout = num_aug[:, :ML_DV] / jnp.maximum(jnp.abs(den), jnp.exp(-m_i[:, 0:1]))
        hout = _rms(hout, nrm_ref[hh])
        og = og_ref[0, :, ML_DV * hh:ML_DV * (hh + 1)].astype(F32)
        o_ref[0, :, ML_DV * hh:ML_DV * (hh + 1)] = (hout * _sigmoid(og)).astype(o_ref.dtype)


def _mlstm(z_main, z_gate, ml_b_i, ml_b_f, ml_norm, tb=ML_TIME_BLOCK, hb=ML_HEADS_PER_STEP):
    b, s, _ = z_main.shape
    qw, vw = hb * ML_DQK, hb * ML_DV
    hq, hk = OFF_MQ // qw, OFF_MK // qw
    hv, ho = OFF_MV // vw, OFF_MO // vw
    qspec = lambda off: pl.BlockSpec((1, tb, qw), lambda bi, hi, ti: (bi, ti, off + hi))
    vspec = lambda off: pl.BlockSpec((1, tb, vw), lambda bi, hi, ti: (bi, ti, off + hi))
    gate_bias = _on_lanes(ml_b_i, GATE_I) + _on_lanes(ml_b_f, GATE_F)
    return pl.pallas_call(
        _mlstm_kernel,
        grid=(b, ML_HEADS // hb, s // tb),
        in_specs=[
            pl.BlockSpec((1, LANES), lambda bi, hi, ti: (0, 0)),
            qspec(hq), qspec(hk), vspec(hv), vspec(ho),
            pl.BlockSpec((1, tb, LANES), lambda bi, hi, ti: (bi, ti, 0)),
            pl.BlockSpec((hb, 1, ML_DV), lambda bi, hi, ti: (hi, 0, 0)),
        ],
        out_specs=pl.BlockSpec((1, tb, vw), lambda bi, hi, ti: (bi, ti, hi)),
        out_shape=jax.ShapeDtypeStruct((b, s, ML_V), BF16),
        scratch_shapes=[
            pltpu.VMEM((hb, ML_DQK, ML_DV + LANES), F32),
            pltpu.VMEM((hb, 1, LANES), F32),
        ],
        compiler_params=_cparams(("parallel", "parallel", "arbitrary")),
        name="mlstm",
    )(gate_bias, z_main, z_main, z_main, z_main, z_gate, ml_norm.reshape(ML_HEADS, 1, ML_DV))


PACK_SUB = 8


def _pack_rows(x, o_ref):
    n, d = x.shape
    half = d // 2
    for j in range(PACK_SUB):
        lo = x[:, LANES * j:LANES * (j + 1)].astype(BF16).astype(F32)
        hi = x[:, half + LANES * j:half + LANES * (j + 1)].astype(BF16).astype(F32)
        word = (lax.bitcast_convert_type(lo, jnp.uint32) >> 16) | lax.bitcast_convert_type(hi, jnp.uint32)
        o_ref[pl.ds(j, n, stride=PACK_SUB), :] = word


def _unpack_chunk(word):
    lo = lax.bitcast_convert_type(word << 16, F32)
    hi = lax.bitcast_convert_type(word & jnp.uint32(0xFFFF0000), F32)
    return lo, hi


def _row_tile(ref, row, n=1):
    start = row * PACK_SUB
    if not isinstance(start, int):
        start = pl.multiple_of(start, PACK_SUB)
    return ref.at[pl.ds(start, n * PACK_SUB), :]


def _outproj_kernel(x_ref, yg_ref, ym_ref, wo1_ref, wo2_ref, gf_ref, wr_ref, br_ref,
                    h_ref, u_ref, ti_ref, tg_ref):
    h1 = (x_ref[...] + jnp.dot(yg_ref[...], wo1_ref[...], preferred_element_type=F32)
          + jnp.dot(ym_ref[...], wo2_ref[...], preferred_element_type=F32))
    h_ref[...] = h1
    u = _rms(h1, gf_ref[...])
    _pack_rows(u, u_ref)

    u_hi = u.astype(BF16)
    u_lo = (u - u_hi.astype(F32)).astype(BF16)
    wr = wr_ref[...]
    w_hi = wr.astype(BF16)
    w_lo = (wr - w_hi.astype(F32)).astype(BF16)
    d = lambda a, b: jnp.dot(a, b, preferred_element_type=F32)
    logits = d(u_hi, w_hi) + (d(u_hi, w_lo) + d(u_lo, w_hi)) + br_ref[...]

    lane = lax.broadcasted_iota(jnp.int32, logits.shape, 1)
    lg = jnp.where(lane < N_EXPERTS, logits, NEG_BIG)
    vals, idxs = [], []
    for _ in range(TOP_K):
        m = jnp.max(lg, axis=1, keepdims=True)
        idx = jnp.min(jnp.where(lg == m, lane, LANES), axis=1, keepdims=True)
        vals.append(m)
        idxs.append(idx)
        lg = jnp.where(lane == idx, NEG_BIG, lg)
    es = [jnp.exp(vv - vals[0]) for vv in vals]
    tot = es[0] + es[1] + es[2] + es[3]
    ti = jnp.zeros(logits.shape, jnp.int32)
    tg = jnp.zeros(logits.shape, F32)
    for kk in range(TOP_K):
        ti = jnp.where(lane == kk, idxs[kk], ti)
        tg = jnp.where(lane == kk, es[kk] / tot, tg)
    ti_ref[...] = ti
    tg_ref[...] = tg


def _out_proj(x2, y_gdn, y_ml, w_out_bf, g_ffn, w_router_pad, b_router_pad, tm=512):
    t, d = x2.shape
    row = lambda w: pl.BlockSpec((tm, w), lambda i: (i, 0))
    const = lambda shp: pl.BlockSpec(shp, lambda i: (0, 0))
    return pl.pallas_call(
        _outproj_kernel,
        grid=(t // tm,),
        in_specs=[
            row(d), row(GDN_V), row(ML_V),
            pl.BlockSpec((GDN_V, d), lambda i: (0, 0)),
            pl.BlockSpec((ML_V, d), lambda i: (GDN_V // ML_V, 0)),
            const((1, d)), const((d, LANES)), const((1, LANES)),
        ],
        out_specs=[row(d), pl.BlockSpec((tm * PACK_SUB, LANES), lambda i: (i, 0)), row(LANES), row(LANES)],
        out_shape=[jax.ShapeDtypeStruct((t, d), F32), jax.ShapeDtypeStruct((t * PACK_SUB, LANES), jnp.uint32),
                   jax.ShapeDtypeStruct((t, LANES), jnp.int32), jax.ShapeDtypeStruct((t, LANES), F32)],
        compiler_params=_cparams(("parallel",)),
        name="out_proj_router",
    )(x2, y_gdn, y_ml, w_out_bf, w_out_bf, g_ffn.reshape(1, d), w_router_pad, b_router_pad)


FFN_ROWS = 512
INV_UNROLL = 8


DEINT_GROUP = 2 * LANES
MXU_TILE = 256


def _invert_slice(step, lo_ref, hi_ref, tail_ref, pos_ref, inv_ref):
    def fill8(t, c):
        for k in range(INV_UNROLL):
            inv_ref[t * INV_UNROLL + k] = -1
        return c

    def fill1(r, c):
        inv_ref[r] = -1
        return c

    def per_expert(e, c):
        lax.fori_loop(lo_ref[e], hi_ref[e], fill1, 0)
        return c

    @pl.when(step == 0)
    def _():
        lax.fori_loop(0, FFN_ROWS // INV_UNROLL, fill8, 0)
        lax.fori_loop(tail_ref[0], tail_ref[1], fill8, 0)
        lax.fori_loop(0, lo_ref.shape[0], per_expert, 0)

    chunk = pos_ref.shape[0]

    def body(t, c):
        for k in range(INV_UNROLL):
            a = t * INV_UNROLL + k
            inv_ref[FFN_ROWS + pos_ref[a]] = step * chunk + a
        return c

    lax.fori_loop(0, chunk // INV_UNROLL, body, 0)


def _prep_kernel(lo_ref, hi_ref, tail_ref, pos_ref, wgu_ref, wg_o, wu_o, inv_ref):
    _invert_slice(pl.program_id(0) * pl.num_programs(1) + pl.program_id(1), lo_ref, hi_ref, tail_ref, pos_ref,
                  inv_ref)
    x = wgu_ref[0].astype(BF16)
    r = lax.broadcasted_iota(jnp.int32, (DEINT_GROUP, LANES), 0)
    c = lax.broadcasted_iota(jnp.int32, (DEINT_GROUP, LANES), 1)
    sel_even = (r == 2 * c).astype(BF16)
    sel_odd = (r == 2 * c + 1).astype(BF16)
    for k in range(x.shape[1] // DEINT_GROUP):
        blk = x[:, DEINT_GROUP * k:DEINT_GROUP * (k + 1)]
        wg_o[0, :, LANES * k:LANES * (k + 1)] = jnp.dot(blk, sel_even, preferred_element_type=F32).astype(BF16)
        wu_o[0, :, LANES * k:LANES * (k + 1)] = jnp.dot(blk, sel_odd, preferred_element_type=F32).astype(BF16)


def _prep_weights_and_slots(w_gu, pos_flat, pad_lo, pad_hi, rows_used, n_rows, tn=512):
    n_e, d, f2 = w_gu.shape
    f = f2 // 2
    nj = f // tn
    n_inv = FFN_ROWS + n_rows
    chunk = pos_flat.shape[0] // (n_e * nj)
    assert chunk * n_e * nj == pos_flat.shape[0] and chunk % INV_UNROLL == 0
    tail = jnp.stack([(FFN_ROWS + rows_used) // INV_UNROLL, jnp.int32(n_inv // INV_UNROLL)]).astype(jnp.int32)
    smem = pl.BlockSpec(memory_space=pltpu.SMEM)
    return pl.pallas_call(
        _prep_kernel,
        grid=(n_e, nj),
        in_specs=[
            smem, smem, smem,
            pl.BlockSpec((chunk,), lambda e, j: (e * nj + j,), memory_space=pltpu.SMEM),
            pl.BlockSpec((1, d, 2 * tn), lambda e, j: (e, 0, j)),
        ],
        out_specs=[
            pl.BlockSpec((1, d, tn), lambda e, j: (e, 0, j)),
            pl.BlockSpec((1, d, tn), lambda e, j: (e, 0, j)),
            smem,
        ],
        out_shape=[jax.ShapeDtypeStruct((n_e, d, f), BF16), jax.ShapeDtypeStruct((n_e, d, f), BF16),
                   jax.ShapeDtypeStruct((n_inv,), jnp.int32)],
        compiler_params=_cparams(("arbitrary", "arbitrary")),
        name="expert_weight_prep",
    )(FFN_ROWS + pad_lo, FFN_ROWS + pad_hi, tail, pos_flat, w_gu)


def _ffn_kernel(be_ref, rows_ref, nv_ref, inv_ref, u_hbm, wg_ref, wu_ref, wd_ref, bg_ref, bu_ref, bd_ref,
                y_hbm, xg_buf, xb_scr, acc_scr, o_buf, sem):
    i = pl.program_id(0)
    j = pl.program_id(1)
    nv = nv_ref[0]
    tm, d = xb_scr.shape
    half = d // 2
    trash0 = y_hbm.shape[0] // PACK_SUB - tm

    tf = wg_ref.shape[2]
    n_chunks = tf // MXU_TILE + d // MXU_TILE
    bounds = [tm * g // n_chunks for g in range(n_chunks + 1)]

    def gather_issue(blk, g=None):
        base = (blk + 1) * tm
        for r in range(tm) if g is None else range(bounds[g], bounds[g + 1]):
            tok = jnp.maximum(inv_ref[base + r], 0) >> 2
            pltpu.make_async_copy(_row_tile(u_hbm, tok), _row_tile(xg_buf, r), sem.at[0]).start()

    def gather_wait():
        pltpu.make_async_copy(_row_tile(u_hbm, 0, tm), xg_buf, sem.at[0]).wait()

    def scatter_issue(blk, g=None):
        base = (blk + 1) * tm
        for r in range(tm) if g is None else range(bounds[g], bounds[g + 1]):
            a = inv_ref[base + r]
            pltpu.make_async_copy(_row_tile(o_buf, r), _row_tile(y_hbm, jnp.where(a < 0, trash0 + r, a)),
                                  sem.at[1]).start(priority=r % 2)

    def scatter_wait():
        pltpu.make_async_copy(o_buf, _row_tile(y_hbm, 0, tm), sem.at[1]).wait()

    @pl.when(i < nv)
    def _():
        par = i % 2
        full = rows_ref[i] > tm // 2

        def tile(n, first, between):
            xb = xb_scr[0:n, :]
            g = 0
            acts = []
            for k in range(tf // MXU_TILE):
                cs = slice(MXU_TILE * k, MXU_TILE * (k + 1))
                gate = jnp.minimum(jnp.dot(xb, wg_ref[0, :, cs], preferred_element_type=F32) + bg_ref[0, :, cs],
                                   SWIGLU_LIMIT)
                up = jnp.clip(jnp.dot(xb, wu_ref[0, :, cs], preferred_element_type=F32) + bu_ref[0, :, cs],
                              -SWIGLU_LIMIT, SWIGLU_LIMIT)
                acts.append((gate * _sigmoid(gate * SWIGLU_ALPHA) * (up + 1.0)).astype(BF16))
                between(g)
                g += 1
            act = jnp.concatenate(acts, axis=1)
            for k in range(d // MXU_TILE):
                cs = slice(MXU_TILE * k, MXU_TILE * (k + 1))
                y = jnp.dot(act, wd_ref[0, :, cs].astype(BF16), preferred_element_type=F32)
                if first:
                    acc_scr[par, 0:n, cs] = y + bd_ref[0, :, cs]
                    if n < tm:
                        acc_scr[par, n:tm, cs] = jnp.zeros((tm - n, MXU_TILE), F32)
                else:
                    acc_scr[par, 0:n, cs] += y
                between(g)
                g += 1

        def either_size(fn):
            @pl.when(full)
            def _():
                fn(tm)

            @pl.when(jnp.logical_not(full))
            def _():
                fn(tm // 2)

        @pl.when(j == 0)
        def _():
            @pl.when(i == 0)
            def _():
                acc_scr[1] = jnp.zeros((tm, d), F32)
                gather_issue(0)

            @pl.when(i > 0)
            def _():
                scatter_wait()

            gather_wait()
            for c in range(PACK_SUB):
                lo, hi = _unpack_chunk(xg_buf[pl.ds(c, tm, stride=PACK_SUB), :])
                xb_scr[:, LANES * c:LANES * (c + 1)] = lo.astype(BF16)
                xb_scr[:, half + LANES * c:half + LANES * (c + 1)] = hi.astype(BF16)
            _pack_rows(acc_scr[1 - par], o_buf)
            nxt = jnp.minimum(i + 1, nv - 1)
            either_size(lambda n: tile(n, True, lambda g: gather_issue(nxt, g)))

        @pl.when(j == 1)
        def _():
            either_size(lambda n: tile(n, False, lambda g: scatter_issue(i - 1, g)))

            @pl.when(i == nv - 1)
            def _():
                scatter_wait()
                _pack_rows(acc_scr[par], o_buf)
                scatter_issue(i)
                scatter_wait()
                gather_wait()


def _ffn(u_packed, inv, block_e, block_rows, n_valid, w_g, w_u, w_d, b_g, b_u, b_d, tm=FFN_ROWS):
    n_e, f, d = w_d.shape
    nf = 2
    tf = f // nf
    n_blocks = inv.shape[0] // tm - 1
    n_asg = (u_packed.shape[0] // PACK_SUB) * TOP_K
    ic = lambda i, nv: jnp.minimum(i, nv[0] - 1)
    jc = lambda i, j, nv: jnp.where(i < nv[0], j, nf - 1)
    col = lambda i, j, be, br, nv, iv: (be[ic(i, nv)], 0, jc(i, j, nv))
    return pl.pallas_call(
        _ffn_kernel,
        grid_spec=pltpu.PrefetchScalarGridSpec(
            num_scalar_prefetch=4,
            grid=(n_blocks, nf),
            in_specs=[
                pl.BlockSpec(memory_space=pl.ANY),
                pl.BlockSpec((1, d, tf), col),
                pl.BlockSpec((1, d, tf), col),
                pl.BlockSpec((1, tf, d), lambda i, j, be, br, nv, iv: (be[ic(i, nv)], jc(i, j, nv), 0)),
                pl.BlockSpec((1, 1, tf), col),
                pl.BlockSpec((1, 1, tf), col),
                pl.BlockSpec((1, 1, d), lambda i, j, be, br, nv, iv: (be[ic(i, nv)], 0, 0)),
            ],
            out_specs=pl.BlockSpec(memory_space=pl.ANY),
            scratch_shapes=[
                pltpu.VMEM((tm * PACK_SUB, LANES), jnp.uint32),
                pltpu.VMEM((tm, d), BF16),
                pltpu.VMEM((2, tm, d), F32),
                pltpu.VMEM((tm * PACK_SUB, LANES), jnp.uint32),
                pltpu.SemaphoreType.DMA((2,)),
            ],
        ),
        out_shape=jax.ShapeDtypeStruct(((n_asg + tm) * PACK_SUB, LANES), jnp.uint32),
        compiler_params=_cparams(("arbitrary", "arbitrary")),
        name="expert_ffn",
    )(block_e, block_rows, n_valid, inv, u_packed, w_g, w_u, w_d, b_g.reshape(n_e, 1, f),
      b_u.reshape(n_e, 1, f), b_d.reshape(n_e, 1, d))


COMBINE_ROWS = 256


def _combine_kernel(y_ref, tg_ref, h1_ref, p_ref, wpg_ref, wpp_ref, gp_ref, gpp_ref, gfin_ref, o_ref):
    tc = tg_ref.shape[0]
    tg = tg_ref[...]
    gks = [jnp.broadcast_to(tg[:, kk:kk + 1], (tc, LANES)) for kk in range(TOP_K)]
    lo_parts, hi_parts = [], []
    for c in range(PACK_SUB):
        lo_sum = hi_sum = None
        for kk in range(TOP_K):
            lo, hi = _unpack_chunk(y_ref[pl.ds(PACK_SUB * kk + c, tc, stride=PACK_SUB * TOP_K), :])
            gk = gks[kk]
            lo_sum = gk * lo if lo_sum is None else lo_sum + gk * lo
            hi_sum = gk * hi if hi_sum is None else hi_sum + gk * hi
        lo_parts.append(lo_sum)
        hi_parts.append(hi_sum)
    moe = jnp.concatenate(lo_parts + hi_parts, axis=1)
    h2 = h1_ref[...] + moe
    gate = _sigmoid(_bdot(_rms(h2, gp_ref[...]), wpg_ref[...]))
    pe = _rms(_bdot(p_ref[...], wpp_ref[...]), gpp_ref[...])
    h3 = h2 + gate * pe
    o_ref[...] = _rms(h3, gfin_ref[...])


def _combine(y_tok, tg, h1, p2, w_pg, w_pp, g_ple, g_ple_post, g_final, tc=COMBINE_ROWS):
    t, d = h1.shape
    pd = p2.shape[1]
    row = lambda w: pl.BlockSpec((tc, w), lambda i: (i, 0))
    const = lambda shp: pl.BlockSpec(shp, lambda i: (0, 0))
    return pl.pallas_call(
        _combine_kernel,
        grid=(t // tc,),
        in_specs=[
            pl.BlockSpec((tc * TOP_K * PACK_SUB, LANES), lambda i: (i, 0)),
            row(LANES), row(d), row(pd),
            const((d, d)), const((pd, d)), const((1, d)), const((1, d)), const((1, d)),
        ],
        out_specs=row(d),
        out_shape=jax.ShapeDtypeStruct((t, d), F32),
        compiler_params=_cparams(("parallel",)),
        name="combine_ple",
    )(y_tok, tg, h1, p2, w_pg, w_pp, g_ple.reshape(1, d), g_ple_post.reshape(1, d), g_final.reshape(1, d))


def _routing_tables(top_i, n_tok, tm=FFN_ROWS):
    n_asg = n_tok * TOP_K
    e_flat = top_i.reshape(n_asg)
    onehot = (e_flat[:, None] == jnp.arange(N_EXPERTS, dtype=jnp.int32)[None, :]).astype(jnp.int32)
    grp = onehot.reshape(n_asg // LANES, LANES, N_EXPERTS).astype(F32)
    within = jnp.einsum("ij,gjk->gik", jnp.tril(jnp.ones((LANES, LANES), F32)), grp).astype(jnp.int32)
    tot = within[:, -1, :]
    csum = (within + (jnp.cumsum(tot, axis=0) - tot)[:, None, :]).reshape(n_asg, N_EXPERTS)
    counts = csum[-1]
    padded = (counts + tm - 1) // tm * tm
    pend = jnp.cumsum(padded)
    pstart = pend - padded
    pos = jnp.sum(onehot * (csum - 1 + pstart[None, :]), axis=1).astype(jnp.int32)
    n_blocks = (n_tok * TOP_K) // tm + N_EXPERTS
    block_start = jnp.arange(n_blocks, dtype=jnp.int32) * tm
    block_e = jnp.minimum(jnp.sum(pend[None, :] <= block_start[:, None], axis=1), N_EXPERTS - 1).astype(jnp.int32)
    block_rows = jnp.clip((pstart + counts)[block_e] - block_start, 0, tm).astype(jnp.int32)
    n_valid = (pend[-1] // tm).astype(jnp.int32).reshape(1)
    pad_lo = (pstart + counts).astype(jnp.int32)
    return pos, pad_lo, pend.astype(jnp.int32), block_e, block_rows, n_valid, n_blocks * tm


def kernel(x, p, g_mix, w_in, conv_w, a_log, dt_bias, gdn_norm, ml_b_i, ml_b_f, ml_norm, w_out, g_ffn, w_router, b_router, w_gu, b_gu, w_down, b_down, g_ple, w_ple_gate, w_ple_proj, g_ple_post, g_final):
    b, s, d = x.shape
    n_tok = b * s
    assert w_in.shape[0] == 1, "single-layer block: the final norm is fused into the layer's last kernel"
    l = 0
    x2 = x.reshape(n_tok, d)
    w_main, w_gate = _prep_in_weights(w_in[l])
    z_main, z_gate = _in_proj(x2, g_mix[l], w_main, w_gate)
    z_main = z_main.reshape(b, s, Z_MAIN)
    z_gate = z_gate.reshape(b, s, LANES)
    y_gdn = _gdn(z_main, z_gate, conv_w[l], a_log[l], dt_bias[l], gdn_norm[l])
    y_ml = _mlstm(z_main, z_gate, ml_b_i[l], ml_b_f[l], ml_norm[l])

    w_r = jnp.pad(w_router[l], ((0, 0), (0, LANES - N_EXPERTS)))
    b_r = jnp.pad(b_router[l], (0, LANES - N_EXPERTS)).reshape(1, LANES)
    h1, u2, top_i, top_g = _out_proj(x2, y_gdn.reshape(n_tok, GDN_V), y_ml.reshape(n_tok, ML_V),
                                     w_out[l].astype(BF16), g_ffn[l], w_r, b_r)

    pos, pad_lo, pend, block_e, block_rows, n_valid, n_rows = _routing_tables(top_i[:, :TOP_K], n_tok)
    w_g, w_u, inv = _prep_weights_and_slots(w_gu[l], pos, pad_lo, pend, pend[-1], n_rows)
    y_tok = _ffn(u2, inv, block_e, block_rows, n_valid, w_g, w_u, w_down[l],
                 b_gu[l][:, 0::2], b_gu[l][:, 1::2], b_down[l])
    out = _combine(y_tok, top_g, h1, p[l].reshape(n_tok, -1), w_ple_gate[l].astype(BF16),
                   w_ple_proj[l].astype(BF16), g_ple[l], g_ple_post[l], g_final)
    return out.reshape(b, s, d)
```

```python
import jax
import jax.numpy as jnp
from jax import lax
from jax.experimental import pallas as pl
from jax.experimental.pallas import tpu as pltpu

F32 = jnp.float32
BF16 = jnp.bfloat16

EPS = 1e-6
CHUNK = 64
GDN_HEADS = 8
GDN_DK = 128
GDN_DV = 128
GDN_CONV = 4
ML_HEADS = 4
ML_DQK = 128
ML_DV = 256
GATE_SOFTCAP = 15.0
N_EXPERTS = 32
TOP_K = 4
SWIGLU_LIMIT = 7.0
SWIGLU_ALPHA = 1.702

LANES = 128
NEG_BIG = -1e30

GDN_QK = GDN_HEADS * GDN_DK
GDN_V = GDN_HEADS * GDN_DV
ML_QK = ML_HEADS * ML_DQK
ML_V = ML_HEADS * ML_DV
OFF_Q = 0
OFF_K = OFF_Q + GDN_QK
OFF_V = OFF_K + GDN_QK
OFF_GZ = OFF_V + GDN_V
OFF_MQ = OFF_GZ + GDN_V
OFF_MK = OFF_MQ + ML_QK
OFF_MV = OFF_MK + ML_QK
OFF_MO = OFF_MV + ML_V
Z_MAIN = OFF_MO + ML_V
GATE_A = 0
GATE_B = GATE_A + GDN_HEADS
GATE_I = GATE_B + GDN_HEADS
GATE_F = GATE_I + ML_HEADS

GDN_TIME_BLOCK = 128
ML_TIME_BLOCK = 256
GDN_HEADS_PER_STEP = 8
ML_HEADS_PER_STEP = 4
VMEM_LIMIT = 56 * 1024 * 1024


def _cparams(sem):
    return pltpu.CompilerParams(dimension_semantics=sem, vmem_limit_bytes=VMEM_LIMIT)


def _bdot(a, b):
    return jnp.dot(a.astype(BF16), b.astype(BF16), preferred_element_type=F32)


def _bdot_nt(a, b):
    return lax.dot_general(a.astype(BF16), b.astype(BF16), (((1,), (1,)), ((), ())),
                           preferred_element_type=F32)


def _bdot_tn(a, b):
    return lax.dot_general(a.astype(BF16), b.astype(BF16), (((0,), (0,)), ((), ())),
                           preferred_element_type=F32)


def _split3(x):
    hi = x.astype(BF16)
    r1 = x - hi.astype(F32)
    mid = r1.astype(BF16)
    lo = (r1 - mid.astype(F32)).astype(BF16)
    return hi, mid, lo


def _dot_sel(sel_bf16, x):
    hi, mid, lo = _split3(x)
    d = lambda t: jnp.dot(sel_bf16, t, preferred_element_type=F32)
    return d(hi) + (d(mid) + d(lo))


def _sigmoid(x):
    return 1.0 / (1.0 + jnp.exp(-x))


def _softplus(x):
    return jnp.maximum(x, 0.0) + jnp.log1p(jnp.exp(-jnp.abs(x)))


def _rms(x, g):
    return x * lax.rsqrt(jnp.mean(x * x, axis=-1, keepdims=True) + EPS) * g


def _lane_pick(zg, idx):
    lane = lax.broadcasted_iota(jnp.int32, zg.shape, 1)
    return jnp.sum(jnp.where(lane == idx, zg, 0.0), axis=1, keepdims=True)


def _on_lanes(v, start):
    return jnp.pad(v.astype(F32), (start, LANES - start - v.shape[0])).reshape(1, LANES)


def _chunk_masks(n):
    r = lax.broadcasted_iota(jnp.int32, (n, n), 0)
    c = lax.broadcasted_iota(jnp.int32, (n, n), 1)
    same = lambda s: (r >> s) == (c >> s)
    return r, c, same


def _chunk_last(gc):
    n = gc.shape[0] // CHUNK
    parts = [jnp.broadcast_to(gc[CHUNK * (i + 1) - 1:CHUNK * (i + 1), :], (CHUNK, gc.shape[1]))
             for i in range(n)]
    return jnp.concatenate(parts, axis=0)


def _inproj_kernel(x_ref, g_ref, w_ref, wg_ref, z_ref, zg_ref, u_scr):
    @pl.when(pl.program_id(1) == 0)
    def _():
        u = _rms(x_ref[...], g_ref[...]).astype(BF16)
        u_scr[...] = u
        zg_ref[...] = jnp.dot(u, wg_ref[...], preferred_element_type=F32)

    z_ref[...] = jnp.dot(u_scr[...], w_ref[...], preferred_element_type=F32).astype(z_ref.dtype)


def _in_proj(x2, g_mix, w_main, w_gate, tm=1024, tn=1024):
    t, d = x2.shape
    n = w_main.shape[1]
    return pl.pallas_call(
        _inproj_kernel,
        grid=(t // tm, n // tn),
        in_specs=[
            pl.BlockSpec((tm, d), lambda i, j: (i, 0)),
            pl.BlockSpec((1, d), lambda i, j: (0, 0)),
            pl.BlockSpec((d, tn), lambda i, j: (0, j)),
            pl.BlockSpec((d, LANES), lambda i, j: (0, 0)),
        ],
        out_specs=[
            pl.BlockSpec((tm, tn), lambda i, j: (i, j)),
            pl.BlockSpec((tm, LANES), lambda i, j: (i, 0)),
        ],
        out_shape=[jax.ShapeDtypeStruct((t, n), BF16), jax.ShapeDtypeStruct((t, LANES), F32)],
        scratch_shapes=[pltpu.VMEM((tm, d), BF16)],
        compiler_params=_cparams(("parallel", "arbitrary")),
        name="in_proj",
    )(x2, g_mix.reshape(1, d), w_main, w_gate)


def _prep_in_weights(w_in):
    splits = (GDN_QK, GDN_QK, GDN_V, GDN_HEADS, GDN_HEADS, GDN_V, ML_QK, ML_QK, ML_V, ML_HEADS, ML_HEADS, ML_V)
    offs = [0]
    for wd in splits:
        offs.append(offs[-1] + wd)
    part = lambda i: w_in[:, offs[i]:offs[i + 1]]
    w_main = jnp.concatenate([part(i) for i in (0, 1, 2, 5, 6, 7, 8, 11)], axis=1).astype(BF16)
    gates = jnp.concatenate([part(i) for i in (3, 4, 9, 10)], axis=1)
    w_gate = jnp.pad(gates, ((0, 0), (0, LANES - gates.shape[1]))).astype(BF16)
    return w_main, w_gate


def _tri_inverse_minus_eye(ms, same):
    m16 = same(4)
    m32 = same(5)
    n32_mask = jnp.logical_and(m32, jnp.logical_not(m16))
    a = [jnp.where(m16, -m, 0.0) for m in ms]
    acc = list(a)
    for _ in range(3):
        a = [_bdot(x, x) for x in a]
        acc = [p + x + _bdot(p, x) for p, x in zip(acc, a)]
    for level in range(2):
        ns = [jnp.where(n32_mask, m, 0.0) if level == 0 else jnp.where(m32, 0.0, m) for m in ms]
        ys = [n + _bdot(p, n) for p, n in zip(acc, ns)]
        acc = [p - (y + _bdot(y, p)) for p, y in zip(acc, ys)]
    return acc


def _gdn_kernel(alog_ref, dtb_ref, zq_ref, zk_ref, zv_ref, gz_ref, zg_ref, cwq_ref, cwk_ref, cwv_ref,
                gn_ref, o_ref, s_scr, xq_scr, xk_scr, xv_scr):
    hb = s_scr.shape[0]
    h0 = pl.program_id(1) * hb
    tb = zq_ref.shape[1]
    nchunk = tb // CHUNK
    heads = range(hb)
    lanes = lambda hh: slice(LANES * hh, LANES * (hh + 1))

    @pl.when(pl.program_id(2) == 0)
    def _():
        s_scr[...] = jnp.zeros_like(s_scr)
        for scr in (xq_scr, xk_scr, xv_scr):
            scr[0:8, :] = jnp.zeros((8, scr.shape[1]), F32)

    def conv_silu(z_ref, x_scr, cw_ref):
        x = z_ref[0].astype(F32)
        x_scr[8:8 + tb, :] = x
        w = cw_ref[...]
        acc = x * w[GDN_CONV - 1:GDN_CONV, :]
        for s in range(1, GDN_CONV):
            acc = acc + x_scr[pl.ds(8 - s, tb), :] * w[GDN_CONV - 1 - s:GDN_CONV - s, :]
        x_scr[0:8, :] = x[tb - 8:tb, :]
        return acc * _sigmoid(acc)

    q_all = conv_silu(zq_ref, xq_scr, cwq_ref)
    k_all = conv_silu(zk_ref, xk_scr, cwk_ref)
    v_all = conv_silu(zv_ref, xv_scr, cwv_ref)
    gz_all = gz_ref[0].astype(F32)
    zg = zg_ref[0]
    g_all = -jnp.exp(alog_ref[...]) * _softplus(zg + dtb_ref[...])
    beta_all = _sigmoid(zg)

    r, c, same = _chunk_masks(tb)
    in_chunk = same(6)
    tril = jnp.logical_and(in_chunk, r >= c)
    strict = jnp.logical_and(in_chunk, r > c)
    tril_bf = tril.astype(BF16)

    def head_prep(hh):
        q = q_all[:, lanes(hh)]
        k = k_all[:, lanes(hh)]
        v = v_all[:, lanes(hh)]
        q = q * lax.rsqrt(jnp.sum(q * q, axis=-1, keepdims=True) + EPS) * (GDN_DK ** -0.5)
        k = k * lax.rsqrt(jnp.sum(k * k, axis=-1, keepdims=True) + EPS)
        beta = _lane_pick(beta_all, GATE_B + h0 + hh)
        g = _lane_pick(g_all, GATE_A + h0 + hh)
        gcb = _dot_sel(tril_bf, jnp.broadcast_to(g, (tb, LANES)))
        gc_row = gcb.T[0:1, :]
        gc_col = jnp.concatenate([gcb] * (tb // LANES), axis=1)
        decay = jnp.exp(jnp.where(tril, gc_col - gc_row, NEG_BIG))
        eg = jnp.exp(gcb)
        kb = k * beta
        m_low = jnp.where(strict, _bdot_nt(kb, k) * decay, 0.0)
        rhs = jnp.concatenate([v * beta, kb * eg], axis=1)
        attn = _bdot_nt(q, k) * decay
        kd = k * jnp.exp(_chunk_last(gcb) - gcb)
        return dict(m_low=m_low, rhs=rhs, attn=attn, qd=q * eg, kd=kd, gcb=gcb)

    hp = [head_prep(hh) for hh in heads]
    t_m1 = _tri_inverse_minus_eye([p["m_low"] for p in hp], same)
    uws = [p["rhs"] + _bdot(t, p["rhs"]) for p, t in zip(hp, t_m1)]

    states = [s_scr[hh] for hh in heads]
    outs = [[] for _ in heads]
    for i in range(nchunk):
        sl = slice(CHUNK * i, CHUNK * (i + 1))
        for hh in heads:
            p, uw = hp[hh], uws[hh]
            res1 = _bdot(jnp.concatenate([uw[sl, GDN_DV:], p["qd"][sl]], axis=0), states[hh])
            v_new = uw[sl, :GDN_DV] - res1[:CHUNK]
            o_intra = _bdot(p["attn"][sl, CHUNK * i:CHUNK * (i + 1)], v_new)
            outs[hh].append(res1[CHUNK:] + o_intra)
            g_last = jnp.exp(p["gcb"][CHUNK * (i + 1) - 1:CHUNK * (i + 1), :])
            states[hh] = states[hh] * g_last[:, 0:1] + _bdot_tn(p["kd"][sl], v_new)

    for hh in heads:
        s_scr[hh] = states[hh]
        o = _rms(jnp.concatenate(outs[hh], axis=0), gn_ref[...])
        gz = gz_all[:, lanes(hh)]
        o_ref[0, :, lanes(hh)] = (o * (gz * _sigmoid(gz))).astype(o_ref.dtype)


def _gdn(z_main, z_gate, conv_w, a_log, dt_bias, gdn_norm, tb=GDN_TIME_BLOCK, hb=GDN_HEADS_PER_STEP):
    b, s, _ = z_main.shape
    wid = hb * LANES
    hq, hk, hv, hz = OFF_Q // wid, OFF_K // wid, OFF_V // wid, OFF_GZ // wid
    zspec = lambda off: pl.BlockSpec((1, tb, wid), lambda bi, hi, ti: (bi, ti, off + hi))
    cspec = lambda off: pl.BlockSpec((GDN_CONV, wid), lambda bi, hi, ti: (0, off + hi))
    lane_row = pl.BlockSpec((1, LANES), lambda bi, hi, ti: (0, 0))
    return pl.pallas_call(
        _gdn_kernel,
        grid=(b, GDN_HEADS // hb, s // tb),
        in_specs=[
            lane_row, lane_row,
            zspec(hq), zspec(hk), zspec(hv), zspec(hz),
            pl.BlockSpec((1, tb, LANES), lambda bi, hi, ti: (bi, ti, 0)),
            cspec(hq), cspec(hk), cspec(hv),
            pl.BlockSpec((1, GDN_DV), lambda bi, hi, ti: (0, 0)),
        ],
        out_specs=pl.BlockSpec((1, tb, wid), lambda bi, hi, ti: (bi, ti, hi)),
        out_shape=jax.ShapeDtypeStruct((b, s, GDN_V), BF16),
        scratch_shapes=[
            pltpu.VMEM((hb, GDN_DK, GDN_DV), F32),
            pltpu.VMEM((8 + tb, wid), F32),
            pltpu.VMEM((8 + tb, wid), F32),
            pltpu.VMEM((8 + tb, wid), F32),
        ],
        compiler_params=_cparams(("parallel", "parallel", "arbitrary")),
        name="gdn",
    )(_on_lanes(a_log, GATE_A), _on_lanes(dt_bias, GATE_A), z_main, z_main, z_main, z_main, z_gate,
      conv_w, conv_w, conv_w, gdn_norm.reshape(1, GDN_DV))


def _mlstm_kernel(bias_ref, q_ref, k_ref, v_ref, og_ref, zg_ref, nrm_ref, o_ref, c_scr, m_scr):
    hb = c_scr.shape[0]
    h0 = pl.program_id(1) * hb
    tb = q_ref.shape[1]
    nchunk = tb // CHUNK
    cap = GATE_SOFTCAP

    @pl.when(pl.program_id(2) == 0)
    def _():
        c_scr[...] = jnp.zeros_like(c_scr)
        m_scr[...] = jnp.zeros_like(m_scr)

    capped = cap * jnp.tanh((zg_ref[0] + bias_ref[...]) / cap)
    logf_all = -_softplus(-capped)
    r, c, same = _chunk_masks(tb)
    tril = jnp.logical_and(same(6), r >= c)
    tril_bf = tril.astype(BF16)
    ones = jnp.ones((tb, LANES), F32)

    for hh in range(hb):
        q = q_ref[0, :, ML_DQK * hh:ML_DQK * (hh + 1)].astype(F32)
        k = k_ref[0, :, ML_DQK * hh:ML_DQK * (hh + 1)].astype(F32) * (ML_DQK ** -0.5)
        v = v_ref[0, :, ML_DV * hh:ML_DV * (hh + 1)].astype(F32)
        i_pre = _lane_pick(capped, GATE_I + h0 + hh)
        log_f = _lane_pick(logf_all, GATE_F + h0 + hh)

        fcb = _dot_sel(tril_bf, jnp.broadcast_to(log_f, (tb, LANES)))
        fmi = fcb - i_pre
        fmi_row = fmi.T[0:1, :]
        fc_col = jnp.concatenate([fcb] * (tb // LANES), axis=1)
        d_mat = jnp.where(tril, fc_col - fmi_row, NEG_BIG)
        d_max = jnp.max(d_mat, axis=-1, keepdims=True)
        a_end = _chunk_last(fcb) - fmi

        m_st = m_scr[hh]
        m_rows, decs, wks = [], [], []
        for i in range(nchunk):
            sl = slice(CHUNK * i, CHUNK * (i + 1))
            f_last = fcb[CHUNK * (i + 1) - 1:CHUNK * (i + 1), :]
            a_max = jnp.max(a_end[sl], axis=0, keepdims=True)
            m_new = jnp.maximum(f_last + m_st, a_max)
            m_rows.append(jnp.broadcast_to(m_st, (CHUNK, LANES)))
            decs.append(jnp.exp(f_last + m_st - m_new))
            wks.append(jnp.exp(a_end[sl] - m_new))
            m_st = m_new
        m_scr[hh] = m_st

        inter = fcb + jnp.concatenate(m_rows, axis=0)
        m_i = jnp.maximum(d_max, inter)
        s_inter = jnp.exp(inter - m_i)
        wts = jnp.exp(d_mat - m_i[:, 0:1]) * _bdot_nt(q, k)
        v_aug = jnp.concatenate([v, ones], axis=1)
        intra = _bdot(wts, v_aug)

        cst = c_scr[hh]
        nums = []
        for i in range(nchunk):
            sl = slice(CHUNK * i, CHUNK * (i + 1))
            nums.append(s_inter[sl, 0:1] * _bdot(q[sl], cst) + intra[sl])
            cst = decs[i][:, 0:1] * cst + _bdot_tn(wks[i] * k[sl], v_aug[sl])
        c_scr[hh] = cst

        num_aug = jnp.concatenate(nums, axis=0)
        den = num_aug[:, ML_DV:ML_DV + 1]
        hout = num_aug[:, :ML_DV] / jnp.maximum(jnp.abs(den), jnp.exp(-m_i[:, 0:1]))
        hout = _rms(hout, nrm_ref[hh])
        og = og_ref[0, :, ML_DV * hh:ML_DV * (hh + 1)].astype(F32)
        o_ref[0, :, ML_DV * hh:ML_DV * (hh + 1)] = (hout * _sigmoid(og)).astype(o_ref.dtype)


def _mlstm(z_main, z_gate, ml_b_i, ml_b_f, ml_norm, tb=ML_TIME_BLOCK, hb=ML_HEADS_PER_STEP):
    b, s, _ = z_main.shape
    qw, vw = hb * ML_DQK, hb * ML_DV
    hq, hk = OFF_MQ // qw, OFF_MK // qw
    hv, ho = OFF_MV // vw, OFF_MO // vw
    qspec = lambda off: pl.BlockSpec((1, tb, qw), lambda bi, hi, ti: (bi, ti, off + hi))
    vspec = lambda off: pl.BlockSpec((1, tb, vw), lambda bi, hi, ti: (bi, ti, off + hi))
    gate_bias = _on_lanes(ml_b_i, GATE_I) + _on_lanes(ml_b_f, GATE_F)
    return pl.pallas_call(
        _mlstm_kernel,
        grid=(b, ML_HEADS // hb, s // tb),
        in_specs=[
            pl.BlockSpec((1, LANES), lambda bi, hi, ti: (0, 0)),
            qspec(hq), qspec(hk), vspec(hv), vspec(ho),
            pl.BlockSpec((1, tb, LANES), lambda bi, hi, ti: (bi, ti, 0)),
            pl.BlockSpec((hb, 1, ML_DV), lambda bi, hi, ti: (hi, 0, 0)),
        ],
        out_specs=pl.BlockSpec((1, tb, vw), lambda bi, hi, ti: (bi, ti, hi)),
        out_shape=jax.ShapeDtypeStruct((b, s, ML_V), BF16),
        scratch_shapes=[
            pltpu.VMEM((hb, ML_DQK, ML_DV + LANES), F32),
            pltpu.VMEM((hb, 1, LANES), F32),
        ],
        compiler_params=_cparams(("parallel", "parallel", "arbitrary")),
        name="mlstm",
    )(gate_bias, z_main, z_main, z_main, z_main, z_gate, ml_norm.reshape(ML_HEADS, 1, ML_DV))


PACK_SUB = 8


def _pack_rows(x, o_ref):
    n, d = x.shape
    half = d // 2
    for j in range(PACK_SUB):
        lo = x[:, LANES * j:LANES * (j + 1)].astype(BF16).astype(F32)
        hi = x[:, half + LANES * j:half + LANES * (j + 1)].astype(BF16).astype(F32)
        word = (lax.bitcast_convert_type(lo, jnp.uint32) >> 16) | lax.bitcast_convert_type(hi, jnp.uint32)
        o_ref[pl.ds(j, n, stride=PACK_SUB), :] = word


def _unpack_chunk(word):
    lo = lax.bitcast_convert_type(word << 16, F32)
    hi = lax.bitcast_convert_type(word & jnp.uint32(0xFFFF0000), F32)
    return lo, hi


def _row_tile(ref, row, n=1):
    start = row * PACK_SUB
    if not isinstance(start, int):
        start = pl.multiple_of(start, PACK_SUB)
    return ref.at[pl.ds(start, n * PACK_SUB), :]


def _outproj_kernel(x_ref, yg_ref, ym_ref, wo1_ref, wo2_ref, gf_ref, wr_ref, br_ref,
                    h_ref, u_ref, ti_ref, tg_ref):
    h1 = (x_ref[...] + jnp.dot(yg_ref[...], wo1_ref[...], preferred_element_type=F32)
          + jnp.dot(ym_ref[...], wo2_ref[...], preferred_element_type=F32))
    h_ref[...] = h1
    u = _rms(h1, gf_ref[...])
    _pack_rows(u, u_ref)

    u_hi = u.astype(BF16)
    u_lo = (u - u_hi.astype(F32)).astype(BF16)
    wr = wr_ref[...]
    w_hi = wr.astype(BF16)
    w_lo = (wr - w_hi.astype(F32)).astype(BF16)
    d = lambda a, b: jnp.dot(a, b, preferred_element_type=F32)
    logits = d(u_hi, w_hi) + (d(u_hi, w_lo) + d(u_lo, w_hi)) + br_ref[...]

    lane = lax.broadcasted_iota(jnp.int32, logits.shape, 1)
    lg = jnp.where(lane < N_EXPERTS, logits, NEG_BIG)
    vals, idxs = [], []
    for _ in range(TOP_K):
        m = jnp.max(lg, axis=1, keepdims=True)
        idx = jnp.min(jnp.where(lg == m, lane, LANES), axis=1, keepdims=True)
        vals.append(m)
        idxs.append(idx)
        lg = jnp.where(lane == idx, NEG_BIG, lg)
    es = [jnp.exp(vv - vals[0]) for vv in vals]
    tot = es[0] + es[1] + es[2] + es[3]
    ti = jnp.zeros(logits.shape, jnp.int32)
    tg = jnp.zeros(logits.shape, F32)
    for kk in range(TOP_K):
        ti = jnp.where(lane == kk, idxs[kk], ti)
        tg = jnp.where(lane == kk, es[kk] / tot, tg)
    ti_ref[...] = ti
    tg_ref[...] = tg


def _out_proj(x2, y_gdn, y_ml, w_out_bf, g_ffn, w_router_pad, b_router_pad, tm=512):
    t, d = x2.shape
    row = lambda w: pl.BlockSpec((tm, w), lambda i: (i, 0))
    const = lambda shp: pl.BlockSpec(shp, lambda i: (0, 0))
    return pl.pallas_call(
        _outproj_kernel,
        grid=(t // tm,),
        in_specs=[
            row(d), row(GDN_V), row(ML_V),
            pl.BlockSpec((GDN_V, d), lambda i: (0, 0)),
            pl.BlockSpec((ML_V, d), lambda i: (GDN_V // ML_V, 0)),
            const((1, d)), const((d, LANES)), const((1, LANES)),
        ],
        out_specs=[row(d), pl.BlockSpec((tm * PACK_SUB, LANES), lambda i: (i, 0)), row(LANES), row(LANES)],
        out_shape=[jax.ShapeDtypeStruct((t, d), F32), jax.ShapeDtypeStruct((t * PACK_SUB, LANES), jnp.uint32),
                   jax.ShapeDtypeStruct((t, LANES), jnp.int32), jax.ShapeDtypeStruct((t, LANES), F32)],
        compiler_params=_cparams(("parallel",)),
        name="out_proj_router",
    )(x2, y_gdn, y_ml, w_out_bf, w_out_bf, g_ffn.reshape(1, d), w_router_pad, b_router_pad)


FFN_ROWS = 512
INV_UNROLL = 8


DEINT_GROUP = 2 * LANES
MXU_TILE = 512


def _invert_slice(step, lo_ref, hi_ref, tail_ref, pos_ref, inv_ref):
    def fill8(t, c):
        for k in range(INV_UNROLL):
            inv_ref[t * INV_UNROLL + k] = -1
        return c

    def fill1(r, c):
        inv_ref[r] = -1
        return c

    def per_expert(e, c):
        lax.fori_loop(lo_ref[e], hi_ref[e], fill1, 0)
        return c

    @pl.when(step == 0)
    def _():
        lax.fori_loop(0, FFN_ROWS // INV_UNROLL, fill8, 0)
        lax.fori_loop(tail_ref[0], tail_ref[1], fill8, 0)
        lax.fori_loop(0, lo_ref.shape[0], per_expert, 0)

    chunk = pos_ref.shape[0]

    def body(t, c):
        for k in range(INV_UNROLL):
            a = t * INV_UNROLL + k
            inv_ref[FFN_ROWS + pos_ref[a]] = step * chunk + a
        return c

    lax.fori_loop(0, chunk // INV_UNROLL, body, 0)


def _prep_kernel(lo_ref, hi_ref, tail_ref, pos_ref, wgu_ref, wg_o, wu_o, inv_ref):
    _invert_slice(pl.program_id(0) * pl.num_programs(1) + pl.program_id(1), lo_ref, hi_ref, tail_ref, pos_ref,
                  inv_ref)
    x = wgu_ref[0].astype(BF16)
    r = lax.broadcasted_iota(jnp.int32, (DEINT_GROUP, LANES), 0)
    c = lax.broadcasted_iota(jnp.int32, (DEINT_GROUP, LANES), 1)
    sel_even = (r == 2 * c).astype(BF16)
    sel_odd = (r == 2 * c + 1).astype(BF16)
    for k in range(x.shape[1] // DEINT_GROUP):
        blk = x[:, DEINT_GROUP * k:DEINT_GROUP * (k + 1)]
        wg_o[0, :, LANES * k:LANES * (k + 1)] = jnp.dot(blk, sel_even, preferred_element_type=F32).astype(BF16)
        wu_o[0, :, LANES * k:LANES * (k + 1)] = jnp.dot(blk, sel_odd, preferred_element_type=F32).astype(BF16)


def _prep_weights_and_slots(w_gu, pos_flat, pad_lo, pad_hi, rows_used, n_rows, tn=512):
    n_e, d, f2 = w_gu.shape
    f = f2 // 2
    nj = f // tn
    n_inv = FFN_ROWS + n_rows
    chunk = pos_flat.shape[0] // (n_e * nj)
    assert chunk * n_e * nj == pos_flat.shape[0] and chunk % INV_UNROLL == 0
    tail = jnp.stack([(FFN_ROWS + rows_used) // INV_UNROLL, jnp.int32(n_inv // INV_UNROLL)]).astype(jnp.int32)
    smem = pl.BlockSpec(memory_space=pltpu.SMEM)
    return pl.pallas_call(
        _prep_kernel,
        grid=(n_e, nj),
        in_specs=[
            smem, smem, smem,
            pl.BlockSpec((chunk,), lambda e, j: (e * nj + j,), memory_space=pltpu.SMEM),
            pl.BlockSpec((1, d, 2 * tn), lambda e, j: (e, 0, j)),
        ],
        out_specs=[
            pl.BlockSpec((1, d, tn), lambda e, j: (e, 0, j)),
            pl.BlockSpec((1, d, tn), lambda e, j: (e, 0, j)),
            smem,
        ],
        out_shape=[jax.ShapeDtypeStruct((n_e, d, f), BF16), jax.ShapeDtypeStruct((n_e, d, f), BF16),
                   jax.ShapeDtypeStruct((n_inv,), jnp.int32)],
        compiler_params=_cparams(("arbitrary", "arbitrary")),
        name="expert_weight_prep",
    )(FFN_ROWS + pad_lo, FFN_ROWS + pad_hi, tail, pos_flat, w_gu)


def _ffn_kernel(be_ref, rows_ref, nv_ref, inv_ref, u_hbm, wg_ref, wu_ref, wd_ref, bg_ref, bu_ref, bd_ref,
                y_hbm, xg_buf, xb_scr, acc_scr, o_buf, sem):
    i = pl.program_id(0)
    j = pl.program_id(1)
    nv = nv_ref[0]
    tm, d = xb_scr.shape
    half = d // 2
    trash0 = y_hbm.shape[0] // PACK_SUB - tm

    tf = wg_ref.shape[2]
    n_chunks = tf // MXU_TILE + d // MXU_TILE
    bounds = [tm * g // n_chunks for g in range(n_chunks + 1)]

    def gather_issue(blk, g=None):
        base = (blk + 1) * tm
        for r in range(tm) if g is None else range(bounds[g], bounds[g + 1]):
            tok = jnp.maximum(inv_ref[base + r], 0) >> 2
            pltpu.make_async_copy(_row_tile(u_hbm, tok), _row_tile(xg_buf, r), sem.at[0]).start()

    def gather_wait():
        pltpu.make_async_copy(_row_tile(u_hbm, 0, tm), xg_buf, sem.at[0]).wait()

    def scatter_issue(blk, g=None):
        base = (blk + 1) * tm
        for r in range(tm) if g is None else range(bounds[g], bounds[g + 1]):
            a = inv_ref[base + r]
            pltpu.make_async_copy(_row_tile(o_buf, r), _row_tile(y_hbm, jnp.where(a < 0, trash0 + r, a)),
                                  sem.at[1]).start(priority=r % 2)

    def scatter_wait():
        pltpu.make_async_copy(o_buf, _row_tile(y_hbm, 0, tm), sem.at[1]).wait()

    @pl.when(i < nv)
    def _():
        par = i % 2
        full = rows_ref[i] > tm // 2

        def tile(n, first, between):
            xb = xb_scr[0:n, :]
            g = 0
            acts = []
            for k in range(tf // MXU_TILE):
                cs = slice(MXU_TILE * k, MXU_TILE * (k + 1))
                gate = jnp.minimum(jnp.dot(xb, wg_ref[0, :, cs], preferred_element_type=F32) + bg_ref[0, :, cs],
                                   SWIGLU_LIMIT)
                up = jnp.clip(jnp.dot(xb, wu_ref[0, :, cs], preferred_element_type=F32) + bu_ref[0, :, cs],
                              -SWIGLU_LIMIT, SWIGLU_LIMIT)
                acts.append((gate * _sigmoid(gate * SWIGLU_ALPHA) * (up + 1.0)).astype(BF16))
                between(g)
                g += 1
            act = jnp.concatenate(acts, axis=1)
            for k in range(d // MXU_TILE):
                cs = slice(MXU_TILE * k, MXU_TILE * (k + 1))
                y = jnp.dot(act, wd_ref[0, :, cs].astype(BF16), preferred_element_type=F32)
                if first:
                    acc_scr[par, 0:n, cs] = y + bd_ref[0, :, cs]
                    if n < tm:
                        acc_scr[par, n:tm, cs] = jnp.zeros((tm - n, MXU_TILE), F32)
                else:
                    acc_scr[par, 0:n, cs] += y
                between(g)
                g += 1

        def either_size(fn):
            @pl.when(full)
            def _():
                fn(tm)

            @pl.when(jnp.logical_not(full))
            def _():
                fn(tm // 2)

        @pl.when(j == 0)
        def _():
            @pl.when(i == 0)
            def _():
                acc_scr[1] = jnp.zeros((tm, d), F32)
                gather_issue(0)

            @pl.when(i > 0)
            def _():
                scatter_wait()

            gather_wait()
            for c in range(PACK_SUB):
                lo, hi = _unpack_chunk(xg_buf[pl.ds(c, tm, stride=PACK_SUB), :])
                xb_scr[:, LANES * c:LANES * (c + 1)] = lo.astype(BF16)
                xb_scr[:, half + LANES * c:half + LANES * (c + 1)] = hi.astype(BF16)
            _pack_rows(acc_scr[1 - par], o_buf)
            nxt = jnp.minimum(i + 1, nv - 1)
            either_size(lambda n: tile(n, True, lambda g: gather_issue(nxt, g)))

        @pl.when(j == 1)
        def _():
            either_size(lambda n: tile(n, False, lambda g: scatter_issue(i - 1, g)))

            @pl.when(i == nv - 1)
            def _():
                scatter_wait()
                _pack_rows(acc_scr[par], o_buf)
                scatter_issue(i)
                scatter_wait()
                gather_wait()


def _ffn(u_packed, inv, block_e, block_rows, n_valid, w_g, w_u, w_d, b_g, b_u, b_d, tm=FFN_ROWS):
    n_e, f, d = w_d.shape
    nf = 2
    tf = f // nf
    n_blocks = inv.shape[0] // tm - 1
    n_asg = (u_packed.shape[0] // PACK_SUB) * TOP_K
    ic = lambda i, nv: jnp.minimum(i, nv[0] - 1)
    jc = lambda i, j, nv: jnp.where(i < nv[0], j, nf - 1)
    col = lambda i, j, be, br, nv, iv: (be[ic(i, nv)], 0, jc(i, j, nv))
    return pl.pallas_call(
        _ffn_kernel,
        grid_spec=pltpu.PrefetchScalarGridSpec(
            num_scalar_prefetch=4,
            grid=(n_blocks, nf),
            in_specs=[
                pl.BlockSpec(memory_space=pl.ANY),
                pl.BlockSpec((1, d, tf), col),
                pl.BlockSpec((1, d, tf), col),
                pl.BlockSpec((1, tf, d), lambda i, j, be, br, nv, iv: (be[ic(i, nv)], jc(i, j, nv), 0)),
                pl.BlockSpec((1, 1, tf), col),
                pl.BlockSpec((1, 1, tf), col),
                pl.BlockSpec((1, 1, d), lambda i, j, be, br, nv, iv: (be[ic(i, nv)], 0, 0)),
            ],
            out_specs=pl.BlockSpec(memory_space=pl.ANY),
            scratch_shapes=[
                pltpu.VMEM((tm * PACK_SUB, LANES), jnp.uint32),
                pltpu.VMEM((tm, d), BF16),
                pltpu.VMEM((2, tm, d), F32),
                pltpu.VMEM((tm * PACK_SUB, LANES), jnp.uint32),
                pltpu.SemaphoreType.DMA((2,)),
            ],
        ),
        out_shape=jax.ShapeDtypeStruct(((n_asg + tm) * PACK_SUB, LANES), jnp.uint32),
        compiler_params=_cparams(("arbitrary", "arbitrary")),
        name="expert_ffn",
    )(block_e, block_rows, n_valid, inv, u_packed, w_g, w_u, w_d, b_g.reshape(n_e, 1, f),
      b_u.reshape(n_e, 1, f), b_d.reshape(n_e, 1, d))


COMBINE_ROWS = 256


def _combine_kernel(y_ref, tg_ref, h1_ref, p_ref, wpg_ref, wpp_ref, gp_ref, gpp_ref, gfin_ref, o_ref):
    tc = tg_ref.shape[0]
    tg = tg_ref[...]
    gks = [jnp.broadcast_to(tg[:, kk:kk + 1], (tc, LANES)) for kk in range(TOP_K)]
    lo_parts, hi_parts = [], []
    for c in range(PACK_SUB):
        lo_sum = hi_sum = None
        for kk in range(TOP_K):
            lo, hi = _unpack_chunk(y_ref[pl.ds(PACK_SUB * kk + c, tc, stride=PACK_SUB * TOP_K), :])
            gk = gks[kk]
            lo_sum = gk * lo if lo_sum is None else lo_sum + gk * lo
            hi_sum = gk * hi if hi_sum is None else hi_sum + gk * hi
        lo_parts.append(lo_sum)
        hi_parts.append(hi_sum)
    moe = jnp.concatenate(lo_parts + hi_parts, axis=1)
    h2 = h1_ref[...] + moe
    gate = _sigmoid(_bdot(_rms(h2, gp_ref[...]), wpg_ref[...]))
    pe = _rms(_bdot(p_ref[...], wpp_ref[...]), gpp_ref[...])
    h3 = h2 + gate * pe
    o_ref[...] = _rms(h3, gfin_ref[...])


def _combine(y_tok, tg, h1, p2, w_pg, w_pp, g_ple, g_ple_post, g_final, tc=COMBINE_ROWS):
    t, d = h1.shape
    pd = p2.shape[1]
    row = lambda w: pl.BlockSpec((tc, w), lambda i: (i, 0))
    const = lambda shp: pl.BlockSpec(shp, lambda i: (0, 0))
    return pl.pallas_call(
        _combine_kernel,
        grid=(t // tc,),
        in_specs=[
            pl.BlockSpec((tc * TOP_K * PACK_SUB, LANES), lambda i: (i, 0)),
            row(LANES), row(d), row(pd),
            const((d, d)), const((pd, d)), const((1, d)), const((1, d)), const((1, d)),
        ],
        out_specs=row(d),
        out_shape=jax.ShapeDtypeStruct((t, d), F32),
        compiler_params=_cparams(("parallel",)),
        name="combine_ple",
    )(y_tok, tg, h1, p2, w_pg, w_pp, g_ple.reshape(1, d), g_ple_post.reshape(1, d), g_final.reshape(1, d))


def _routing_tables(top_i, n_tok, tm=FFN_ROWS):
    n_asg = n_tok * TOP_K
    e_flat = top_i.reshape(n_asg)
    onehot = (e_flat[:, None] == jnp.arange(N_EXPERTS, dtype=jnp.int32)[None, :]).astype(jnp.int32)
    grp = onehot.reshape(n_asg // LANES, LANES, N_EXPERTS).astype(F32)
    within = jnp.einsum("ij,gjk->gik", jnp.tril(jnp.ones((LANES, LANES), F32)), grp).astype(jnp.int32)
    tot = within[:, -1, :]
    csum = (within + (jnp.cumsum(tot, axis=0) - tot)[:, None, :]).reshape(n_asg, N_EXPERTS)
    counts = csum[-1]
    padded = (counts + tm - 1) // tm * tm
    pend = jnp.cumsum(padded)
    pstart = pend - padded
    pos = jnp.sum(onehot * (csum - 1 + pstart[None, :]), axis=1).astype(jnp.int32)
    n_blocks = (n_tok * TOP_K) // tm + N_EXPERTS
    block_start = jnp.arange(n_blocks, dtype=jnp.int32) * tm
    block_e = jnp.minimum(jnp.sum(pend[None, :] <= block_start[:, None], axis=1), N_EXPERTS - 1).astype(jnp.int32)
    block_rows = jnp.clip((pstart + counts)[block_e] - block_start, 0, tm).astype(jnp.int32)
    n_valid = (pend[-1] // tm).astype(jnp.int32).reshape(1)
    pad_lo = (pstart + counts).astype(jnp.int32)
    return pos, pad_lo, pend.astype(jnp.int32), block_e, block_rows, n_valid, n_blocks * tm


def kernel(x, p, g_mix, w_in, conv_w, a_log, dt_bias, gdn_norm, ml_b_i, ml_b_f, ml_norm, w_out, g_ffn, w_router, b_router, w_gu, b_gu, w_down, b_down, g_ple, w_ple_gate, w_ple_proj, g_ple_post, g_final):
    b, s, d = x.shape
    n_tok = b * s
    assert w_in.shape[0] == 1, "single-layer block: the final norm is fused into the layer's last kernel"
    l = 0
    x2 = x.reshape(n_tok, d)
    w_main, w_gate = _prep_in_weights(w_in[l])
    z_main, z_gate = _in_proj(x2, g_mix[l], w_main, w_gate)
    z_main = z_main.reshape(b, s, Z_MAIN)
    z_gate = z_gate.reshape(b, s, LANES)
    y_gdn = _gdn(z_main, z_gate, conv_w[l], a_log[l], dt_bias[l], gdn_norm[l])
    y_ml = _mlstm(z_main, z_gate, ml_b_i[l], ml_b_f[l], ml_norm[l])

    w_r = jnp.pad(w_router[l], ((0, 0), (0, LANES - N_EXPERTS)))
    b_r = jnp.pad(b_router[l], (0, LANES - N_EXPERTS)).reshape(1, LANES)
    h1, u2, top_i, top_g = _out_proj(x2, y_gdn.reshape(n_tok, GDN_V), y_ml.reshape(n_tok, ML_V),
                                     w_out[l].astype(BF16), g_ffn[l], w_r, b_r)

    pos, pad_lo, pend, block_e, block_rows, n_valid, n_rows = _routing_tables(top_i[:, :TOP_K], n_tok)
    w_g, w_u, inv = _prep_weights_and_slots(w_gu[l], pos, pad_lo, pend, pend[-1], n_rows)
    y_tok = _ffn(u2, inv, block_e, block_rows, n_valid, w_g, w_u, w_down[l],
                 b_gu[l][:, 0::2], b_gu[l][:, 1::2], b_down[l])
    out = _combine(y_tok, top_g, h1, p[l].reshape(n_tok, -1), w_ple_gate[l].astype(BF16),
                   w_ple_proj[l].astype(BF16), g_ple[l], g_ple_post[l], g_final)
    return out.reshape(b, s, d)
```

```python
import jax
import jax.numpy as jnp
from jax import lax
from jax.experimental import pallas as pl
from jax.experimental.pallas import tpu as pltpu

F32 = jnp.float32
BF16 = jnp.bfloat16

EPS = 1e-6
CHUNK = 64
GDN_HEADS = 8
GDN_DK = 128
GDN_DV = 128
GDN_CONV = 4
ML_HEADS = 4
ML_DQK = 128
ML_DV = 256
GATE_SOFTCAP = 15.0
N_EXPERTS = 32
TOP_K = 4
SWIGLU_LIMIT = 7.0
SWIGLU_ALPHA = 1.702

LANES = 128
HALO = 8
NEG_BIG = -1e30

GDN_QK = GDN_HEADS * GDN_DK
GDN_V = GDN_HEADS * GDN_DV
ML_QK = ML_HEADS * ML_DQK
ML_V = ML_HEADS * ML_DV
OFF_Q = 0
OFF_K = OFF_Q + GDN_QK
OFF_V = OFF_K + GDN_QK
OFF_GZ = OFF_V + GDN_V
OFF_MQ = OFF_GZ + GDN_V
OFF_MK = OFF_MQ + ML_QK
OFF_MV = OFF_MK + ML_QK
OFF_MO = OFF_MV + ML_V
Z_MAIN = OFF_MO + ML_V
GATE_A = 0
GATE_B = GATE_A + GDN_HEADS
GATE_I = GATE_B + GDN_HEADS
GATE_F = GATE_I + ML_HEADS

GDN_TIME_BLOCK = 128
ML_TIME_BLOCK = 256
GDN_HEADS_PER_STEP = 8
ML_HEADS_PER_STEP = 4
VMEM_LIMIT = 56 * 1024 * 1024


def _cparams(sem):
    return pltpu.CompilerParams(dimension_semantics=sem, vmem_limit_bytes=VMEM_LIMIT)


def _bdot(a, b):
    return jnp.dot(a.astype(BF16), b.astype(BF16), preferred_element_type=F32)


def _bdot_nt(a, b):
    return lax.dot_general(a.astype(BF16), b.astype(BF16), (((1,), (1,)), ((), ())),
                           preferred_element_type=F32)


def _bdot_tn(a, b):
    return lax.dot_general(a.astype(BF16), b.astype(BF16), (((0,), (0,)), ((), ())),
                           preferred_element_type=F32)


def _split3(x):
    hi = x.astype(BF16)
    r1 = x - hi.astype(F32)
    mid = r1.astype(BF16)
    lo = (r1 - mid.astype(F32)).astype(BF16)
    return hi, mid, lo


def _dot_sel(sel_bf16, x):
    hi, mid, lo = _split3(x)
    d = lambda t: jnp.dot(sel_bf16, t, preferred_element_type=F32)
    return d(hi) + (d(mid) + d(lo))


def _sigmoid(x):
    return 1.0 / (1.0 + jnp.exp(-x))


def _softplus(x):
    return jnp.maximum(x, 0.0) + jnp.log1p(jnp.exp(-jnp.abs(x)))


def _rms(x, g):
    return x * lax.rsqrt(jnp.mean(x * x, axis=-1, keepdims=True) + EPS) * g


def _lane_pick(zg, idx):
    lane = lax.broadcasted_iota(jnp.int32, zg.shape, 1)
    return jnp.sum(jnp.where(lane == idx, zg, 0.0), axis=1, keepdims=True)


def _on_lanes(v, start):
    return jnp.pad(v.astype(F32), (start, LANES - start - v.shape[0])).reshape(1, LANES)


def _chunk_masks(n):
    r = lax.broadcasted_iota(jnp.int32, (n, n), 0)
    c = lax.broadcasted_iota(jnp.int32, (n, n), 1)
    same = lambda s: (r >> s) == (c >> s)
    return r, c, same


def _chunk_last(gc):
    n = gc.shape[0] // CHUNK
    parts = [jnp.broadcast_to(gc[CHUNK * (i + 1) - 1:CHUNK * (i + 1), :], (CHUNK, gc.shape[1]))
             for i in range(n)]
    return jnp.concatenate(parts, axis=0)


def _inproj_kernel(x_ref, g_ref, w_ref, wg_ref, z_ref, zg_ref, u_scr):
    @pl.when(pl.program_id(1) == 0)
    def _():
        u = _rms(x_ref[...], g_ref[...]).astype(BF16)
        u_scr[...] = u
        zg_ref[...] = jnp.dot(u, wg_ref[...], preferred_element_type=F32)

    z_ref[...] = jnp.dot(u_scr[...], w_ref[...], preferred_element_type=F32).astype(z_ref.dtype)


def _in_proj(x2, g_mix, w_main, w_gate, tm=1024, tn=Z_MAIN // 4):
    t, d = x2.shape
    n = w_main.shape[1]
    return pl.pallas_call(
        _inproj_kernel,
        grid=(t // tm, n // tn),
        in_specs=[
            pl.BlockSpec((tm, d), lambda i, j: (i, 0)),
            pl.BlockSpec((1, d), lambda i, j: (0, 0)),
            pl.BlockSpec((d, tn), lambda i, j: (0, j)),
            pl.BlockSpec((d, LANES), lambda i, j: (0, 0)),
        ],
        out_specs=[
            pl.BlockSpec((tm, tn), lambda i, j: (i, j)),
            pl.BlockSpec((tm, LANES), lambda i, j: (i, 0)),
        ],
        out_shape=[jax.ShapeDtypeStruct((t, n), BF16), jax.ShapeDtypeStruct((t, LANES), F32)],
        scratch_shapes=[pltpu.VMEM((tm, d), BF16)],
        compiler_params=_cparams(("parallel", "arbitrary")),
        name="in_proj",
    )(x2, g_mix.reshape(1, d), w_main, w_gate)


def _prep_in_weights(w_in):
    splits = (GDN_QK, GDN_QK, GDN_V, GDN_HEADS, GDN_HEADS, GDN_V, ML_QK, ML_QK, ML_V, ML_HEADS, ML_HEADS, ML_V)
    offs = [0]
    for wd in splits:
        offs.append(offs[-1] + wd)
    part = lambda i: w_in[:, offs[i]:offs[i + 1]]
    w_main = jnp.concatenate([part(i) for i in (0, 1, 2, 5, 6, 7, 8, 11)], axis=1).astype(BF16)
    gates = jnp.concatenate([part(i) for i in (3, 4, 9, 10)], axis=1)
    w_gate = jnp.pad(gates, ((0, 0), (0, LANES - gates.shape[1]))).astype(BF16)
    return w_main, w_gate


def _tri_inverse_minus_eye(ms, same):
    m16 = same(4)
    m32 = same(5)
    n32_mask = jnp.logical_and(m32, jnp.logical_not(m16))
    a = [jnp.where(m16, -m, 0.0) for m in ms]
    acc = list(a)
    for _ in range(3):
        a = [_bdot(x, x) for x in a]
        acc = [p + x + _bdot(p, x) for p, x in zip(acc, a)]
    for level in range(2):
        ns = [jnp.where(n32_mask, m, 0.0) if level == 0 else jnp.where(m32, 0.0, m) for m in ms]
        ys = [n + _bdot(p, n) for p, n in zip(acc, ns)]
        acc = [p - (y + _bdot(y, p)) for p, y in zip(acc, ys)]
    return acc


def _gdn_kernel(alog_ref, dtb_ref, zq_ref, zk_ref, zv_ref, gz_ref, zg_ref, cwq_ref, cwk_ref, cwv_ref,
                gn_ref, o_ref, s_scr, xq_scr, xk_scr, xv_scr):
    hb = s_scr.shape[0]
    h0 = pl.program_id(1) * hb
    tb = zq_ref.shape[1]
    nchunk = tb // CHUNK
    heads = range(hb)
    lanes = lambda hh: slice(LANES * hh, LANES * (hh + 1))

    @pl.when(pl.program_id(2) == 0)
    def _():
        s_scr[...] = jnp.zeros_like(s_scr)
        for scr in (xq_scr, xk_scr, xv_scr):
            scr[0:HALO, :] = jnp.zeros((HALO, scr.shape[1]), F32)

    def conv_silu(z_ref, x_scr, cw_ref):
        x = z_ref[0].astype(F32)
        x_scr[HALO:HALO + tb, :] = x
        w = cw_ref[...]
        acc = x * w[GDN_CONV - 1:GDN_CONV, :]
        for s in range(1, GDN_CONV):
            acc = acc + x_scr[pl.ds(HALO - s, tb), :] * w[GDN_CONV - 1 - s:GDN_CONV - s, :]
        x_scr[0:HALO, :] = x[tb - HALO:tb, :]
        return acc * _sigmoid(acc)

    q_all = conv_silu(zq_ref, xq_scr, cwq_ref)
    k_all = conv_silu(zk_ref, xk_scr, cwk_ref)
    v_all = conv_silu(zv_ref, xv_scr, cwv_ref)
    gz_all = gz_ref[0].astype(F32)
    zg = zg_ref[0]
    g_all = -jnp.exp(alog_ref[...]) * _softplus(zg + dtb_ref[...])
    beta_all = _sigmoid(zg)

    r, c, same = _chunk_masks(tb)
    in_chunk = same(6)
    tril = jnp.logical_and(in_chunk, r >= c)
    strict = jnp.logical_and(in_chunk, r > c)
    tril_bf = tril.astype(BF16)

    def head_prep(hh):
        q = q_all[:, lanes(hh)]
        k = k_all[:, lanes(hh)]
        v = v_all[:, lanes(hh)]
        q = q * lax.rsqrt(jnp.sum(q * q, axis=-1, keepdims=True) + EPS) * (GDN_DK ** -0.5)
        k = k * lax.rsqrt(jnp.sum(k * k, axis=-1, keepdims=True) + EPS)
        beta = _lane_pick(beta_all, GATE_B + h0 + hh)
        g = _lane_pick(g_all, GATE_A + h0 + hh)
        gcb = _dot_sel(tril_bf, jnp.broadcast_to(g, (tb, LANES)))
        gc_row = gcb.T[0:1, :]
        gc_col = jnp.concatenate([gcb] * (tb // LANES), axis=1)
        decay = jnp.exp(jnp.where(tril, gc_col - gc_row, NEG_BIG))
        eg = jnp.exp(gcb)
        kb = k * beta
        m_low = jnp.where(strict, _bdot_nt(kb, k) * decay, 0.0)
        rhs = jnp.concatenate([v * beta, kb * eg], axis=1)
        attn = _bdot_nt(q, k) * decay
        kd = k * jnp.exp(_chunk_last(gcb) - gcb)
        return dict(m_low=m_low, rhs=rhs, attn=attn, qd=q * eg, kd=kd, gcb=gcb)

    hp = [head_prep(hh) for hh in heads]
    t_m1 = _tri_inverse_minus_eye([p["m_low"] for p in hp], same)
    uws = [p["rhs"] + _bdot(t, p["rhs"]) for p, t in zip(hp, t_m1)]

    states = [s_scr[hh] for hh in heads]
    outs = [[] for _ in heads]
    for i in range(nchunk):
        sl = slice(CHUNK * i, CHUNK * (i + 1))
        for hh in heads:
            p, uw = hp[hh], uws[hh]
            res1 = _bdot(jnp.concatenate([uw[sl, GDN_DV:], p["qd"][sl]], axis=0), states[hh])
            v_new = uw[sl, :GDN_DV] - res1[:CHUNK]
            o_intra = _bdot(p["attn"][sl, CHUNK * i:CHUNK * (i + 1)], v_new)
            outs[hh].append(res1[CHUNK:] + o_intra)
            g_last = jnp.exp(p["gcb"][CHUNK * (i + 1) - 1:CHUNK * (i + 1), :])
            states[hh] = states[hh] * g_last[:, 0:1] + _bdot_tn(p["kd"][sl], v_new)

    for hh in heads:
        s_scr[hh] = states[hh]
        o = _rms(jnp.concatenate(outs[hh], axis=0), gn_ref[...])
        gz = gz_all[:, lanes(hh)]
        o_ref[0, :, lanes(hh)] = (o * (gz * _sigmoid(gz))).astype(o_ref.dtype)


def _gdn(z_main, z_gate, conv_w, a_log, dt_bias, gdn_norm, tb=GDN_TIME_BLOCK, hb=GDN_HEADS_PER_STEP):
    b, s, _ = z_main.shape
    wid = hb * LANES
    hq, hk, hv, hz = OFF_Q // wid, OFF_K // wid, OFF_V // wid, OFF_GZ // wid
    zspec = lambda off: pl.BlockSpec((1, tb, wid), lambda bi, hi, ti: (bi, ti, off + hi))
    cspec = lambda off: pl.BlockSpec((GDN_CONV, wid), lambda bi, hi, ti: (0, off + hi))
    lane_row = pl.BlockSpec((1, LANES), lambda bi, hi, ti: (0, 0))
    return pl.pallas_call(
        _gdn_kernel,
        grid=(b, GDN_HEADS // hb, s // tb),
        in_specs=[
            lane_row, lane_row,
            zspec(hq), zspec(hk), zspec(hv), zspec(hz),
            pl.BlockSpec((1, tb, LANES), lambda bi, hi, ti: (bi, ti, 0)),
            cspec(hq), cspec(hk), cspec(hv),
            pl.BlockSpec((1, GDN_DV), lambda bi, hi, ti: (0, 0)),
        ],
        out_specs=pl.BlockSpec((1, tb, wid), lambda bi, hi, ti: (bi, ti, hi)),
        out_shape=jax.ShapeDtypeStruct((b, s, GDN_V), BF16),
        scratch_shapes=[
            pltpu.VMEM((hb, GDN_DK, GDN_DV), F32),
            pltpu.VMEM((HALO + tb, wid), F32),
            pltpu.VMEM((HALO + tb, wid), F32),
            pltpu.VMEM((HALO + tb, wid), F32),
        ],
        compiler_params=_cparams(("parallel", "parallel", "arbitrary")),
        name="gdn",
    )(_on_lanes(a_log, GATE_A), _on_lanes(dt_bias, GATE_A), z_main, z_main, z_main, z_main, z_gate,
      conv_w, conv_w, conv_w, gdn_norm.reshape(1, GDN_DV))


def _mlstm_kernel(bias_ref, q_ref, k_ref, v_ref, og_ref, zg_ref, nrm_ref, o_ref, c_scr, m_scr):
    hb = c_scr.shape[0]
    h0 = pl.program_id(1) * hb
    tb = q_ref.shape[1]
    nchunk = tb // CHUNK
    cap = GATE_SOFTCAP

    @pl.when(pl.program_id(2) == 0)
    def _():
        c_scr[...] = jnp.zeros_like(c_scr)
        m_scr[...] = jnp.zeros_like(m_scr)

    capped = cap * jnp.tanh((zg_ref[0] + bias_ref[...]) / cap)
    logf_all = -_softplus(-capped)
    r, c, same = _chunk_masks(tb)
    tril = jnp.logical_and(same(6), r >= c)
    tril_bf = tril.astype(BF16)
    ones = jnp.ones((tb, LANES), F32)

    for hh in range(hb):
        q = q_ref[0, :, ML_DQK * hh:ML_DQK * (hh + 1)].astype(F32)
        k = k_ref[0, :, ML_DQK * hh:ML_DQK * (hh + 1)].astype(F32) * (ML_DQK ** -0.5)
        v = v_ref[0, :, ML_DV * hh:ML_DV * (hh + 1)].astype(F32)
        i_pre = _lane_pick(capped, GATE_I + h0 + hh)
        log_f = _lane_pick(logf_all, GATE_F + h0 + hh)

        fcb = _dot_sel(tril_bf, jnp.broadcast_to(log_f, (tb, LANES)))
        fmi = fcb - i_pre
        fmi_row = fmi.T[0:1, :]
        fc_col = jnp.concatenate([fcb] * (tb // LANES), axis=1)
        d_mat = jnp.where(tril, fc_col - fmi_row, NEG_BIG)
        d_max = jnp.max(d_mat, axis=-1, keepdims=True)
        a_end = _chunk_last(fcb) - fmi

        m_st = m_scr[hh]
        m_rows, decs, wks = [], [], []
        for i in range(nchunk):
            sl = slice(CHUNK * i, CHUNK * (i + 1))
            f_last = fcb[CHUNK * (i + 1) - 1:CHUNK * (i + 1), :]
            a_max = jnp.max(a_end[sl], axis=0, keepdims=True)
            m_new = jnp.maximum(f_last + m_st, a_max)
            m_rows.append(jnp.broadcast_to(m_st, (CHUNK, LANES)))
            decs.append(jnp.exp(f_last + m_st - m_new))
            wks.append(jnp.exp(a_end[sl] - m_new))
            m_st = m_new
        m_scr[hh] = m_st

        inter = fcb + jnp.concatenate(m_rows, axis=0)
        m_i = jnp.maximum(d_max, inter)
        s_inter = jnp.exp(inter - m_i)
        wts = jnp.exp(d_mat - m_i[:, 0:1]) * _bdot_nt(q, k)
        v_aug = jnp.concatenate([v, ones], axis=1)
        intra = _bdot(wts, v_aug)

        cst = c_scr[hh]
        nums = []
        for i in range(nchunk):
            sl = slice(CHUNK * i, CHUNK * (i + 1))
            nums.append(s_inter[sl, 0:1] * _bdot(q[sl], cst) + intra[sl])
            cst = decs[i][:, 0:1] * cst + _bdot_tn(wks[i] * k[sl], v_aug[sl])
        c_scr[hh] = cst

        num_aug = jnp.concatenate(nums, axis=0)
        den = num_aug[:, ML_DV:ML_DV + 1]
        hout = num_aug[:, :ML_DV] / jnp.maximum(jnp.abs(den), jnp.exp(-m_i[:, 0:1]))
        hout = _rms(hout, nrm_ref[hh])
        og = og_ref[0, :, ML_DV * hh:ML_DV * (hh + 1)].astype(F32)
        o_ref[0, :, ML_DV * hh:ML_DV * (hh + 1)] = (hout * _sigmoid(og)).astype(o_ref.dtype)


def _mlstm(z_main, z_gate, ml_b_i, ml_b_f, ml_norm, tb=ML_TIME_BLOCK, hb=ML_HEADS_PER_STEP):
    b, s, _ = z_main.shape
    qw, vw = hb * ML_DQK, hb * ML_DV
    hq, hk = OFF_MQ // qw, OFF_MK // qw
    hv, ho = OFF_MV // vw, OFF_MO // vw
    qspec = lambda off: pl.BlockSpec((1, tb, qw), lambda bi, hi, ti: (bi, ti, off + hi))
    vspec = lambda off: pl.BlockSpec((1, tb, vw), lambda bi, hi, ti: (bi, ti, off + hi))
    gate_bias = _on_lanes(ml_b_i, GATE_I) + _on_lanes(ml_b_f, GATE_F)
    return pl.pallas_call(
        _mlstm_kernel,
        grid=(b, ML_HEADS // hb, s // tb),
        in_specs=[
            pl.BlockSpec((1, LANES), lambda bi, hi, ti: (0, 0)),
            qspec(hq), qspec(hk), vspec(hv), vspec(ho),
            pl.BlockSpec((1, tb, LANES), lambda bi, hi, ti: (bi, ti, 0)),
            pl.BlockSpec((hb, 1, ML_DV), lambda bi, hi, ti: (hi, 0, 0)),
        ],
        out_specs=pl.BlockSpec((1, tb, vw), lambda bi, hi, ti: (bi, ti, hi)),
        out_shape=jax.ShapeDtypeStruct((b, s, ML_V), BF16),
        scratch_shapes=[
            pltpu.VMEM((hb, ML_DQK, ML_DV + LANES), F32),
            pltpu.VMEM((hb, 1, LANES), F32),
        ],
        compiler_params=_cparams(("parallel", "parallel", "arbitrary")),
        name="mlstm",
    )(gate_bias, z_main, z_main, z_main, z_main, z_gate, ml_norm.reshape(ML_HEADS, 1, ML_DV))


PACK_SUB = 8


def _pack_rows(x, o_ref):
    n, d = x.shape
    half = d // 2
    for j in range(PACK_SUB):
        lo = x[:, LANES * j:LANES * (j + 1)].astype(BF16).astype(F32)
        hi = x[:, half + LANES * j:half + LANES * (j + 1)].astype(BF16).astype(F32)
        word = (lax.bitcast_convert_type(lo, jnp.uint32) >> 16) | lax.bitcast_convert_type(hi, jnp.uint32)
        o_ref[pl.ds(j, n, stride=PACK_SUB), :] = word


def _unpack_chunk(word):
    lo = lax.bitcast_convert_type(word << 16, F32)
    hi = lax.bitcast_convert_type(word & jnp.uint32(0xFFFF0000), F32)
    return lo, hi


def _row_tile(ref, row, n=1):
    start = row * PACK_SUB
    if not isinstance(start, int):
        start = pl.multiple_of(start, PACK_SUB)
    return ref.at[pl.ds(start, n * PACK_SUB), :]


def _outproj_kernel(x_ref, yg_ref, ym_ref, wo1_ref, wo2_ref, gf_ref, wr_ref, br_ref,
                    h_ref, u_ref, ti_ref, tg_ref):
    h1 = (x_ref[...] + jnp.dot(yg_ref[...], wo1_ref[...], preferred_element_type=F32)
          + jnp.dot(ym_ref[...], wo2_ref[...], preferred_element_type=F32))
    h_ref[...] = h1
    u = _rms(h1, gf_ref[...])
    _pack_rows(u, u_ref)

    u_hi = u.astype(BF16)
    u_lo = (u - u_hi.astype(F32)).astype(BF16)
    wr = wr_ref[...]
    w_hi = wr.astype(BF16)
    w_lo = (wr - w_hi.astype(F32)).astype(BF16)
    d = lambda a, b: jnp.dot(a, b, preferred_element_type=F32)
    logits = d(u_hi, w_hi) + (d(u_hi, w_lo) + d(u_lo, w_hi)) + br_ref[...]

    lane = lax.broadcasted_iota(jnp.int32, logits.shape, 1)
    lg = jnp.where(lane < N_EXPERTS, logits, NEG_BIG)
    vals, idxs = [], []
    for _ in range(TOP_K):
        m = jnp.max(lg, axis=1, keepdims=True)
        idx = jnp.min(jnp.where(lg == m, lane, LANES), axis=1, keepdims=True)
        vals.append(m)
        idxs.append(idx)
        lg = jnp.where(lane == idx, NEG_BIG, lg)
    es = [jnp.exp(vv - vals[0]) for vv in vals]
    tot = es[0] + es[1] + es[2] + es[3]
    ti = jnp.zeros(logits.shape, jnp.int32)
    tg = jnp.zeros(logits.shape, F32)
    for kk in range(TOP_K):
        ti = jnp.where(lane == kk, idxs[kk], ti)
        tg = jnp.where(lane == kk, es[kk] / tot, tg)
    ti_ref[...] = ti
    tg_ref[...] = tg


def _out_proj(x2, y_gdn, y_ml, w_out_bf, g_ffn, w_router_pad, b_router_pad, tm=512):
    t, d = x2.shape
    row = lambda w: pl.BlockSpec((tm, w), lambda i: (i, 0))
    const = lambda shp: pl.BlockSpec(shp, lambda i: (0, 0))
    return pl.pallas_call(
        _outproj_kernel,
        grid=(t // tm,),
        in_specs=[
            row(d), row(GDN_V), row(ML_V),
            pl.BlockSpec((GDN_V, d), lambda i: (0, 0)),
            pl.BlockSpec((ML_V, d), lambda i: (GDN_V // ML_V, 0)),
            const((1, d)), const((d, LANES)), const((1, LANES)),
        ],
        out_specs=[row(d), pl.BlockSpec((tm * PACK_SUB, LANES), lambda i: (i, 0)), row(LANES), row(LANES)],
        out_shape=[jax.ShapeDtypeStruct((t, d), F32), jax.ShapeDtypeStruct((t * PACK_SUB, LANES), jnp.uint32),
                   jax.ShapeDtypeStruct((t, LANES), jnp.int32), jax.ShapeDtypeStruct((t, LANES), F32)],
        compiler_params=_cparams(("parallel",)),
        name="out_proj_router",
    )(x2, y_gdn, y_ml, w_out_bf, w_out_bf, g_ffn.reshape(1, d), w_router_pad, b_router_pad)


FFN_ROWS = 512
INV_UNROLL = 8


DEINT_GROUP = 2 * LANES
MXU_TILE = 512


def _invert_slice(step, lo_ref, hi_ref, tail_ref, pos_ref, inv_ref):
    def fill8(t, c):
        for k in range(INV_UNROLL):
            inv_ref[t * INV_UNROLL + k] = -1
        return c

    def fill1(r, c):
        inv_ref[r] = -1
        return c

    def per_expert(e, c):
        lax.fori_loop(lo_ref[e], hi_ref[e], fill1, 0)
        return c

    @pl.when(step == 0)
    def _():
        lax.fori_loop(0, FFN_ROWS // INV_UNROLL, fill8, 0)
        lax.fori_loop(tail_ref[0], tail_ref[1], fill8, 0)
        lax.fori_loop(0, lo_ref.shape[0], per_expert, 0)

    chunk = pos_ref.shape[0]

    def body(t, c):
        for k in range(INV_UNROLL):
            a = t * INV_UNROLL + k
            inv_ref[FFN_ROWS + pos_ref[a]] = step * chunk + a
        return c

    lax.fori_loop(0, chunk // INV_UNROLL, body, 0)


def _prep_kernel(lo_ref, hi_ref, tail_ref, pos_ref, wgu_ref, wg_o, wu_o, inv_ref):
    _invert_slice(pl.program_id(0) * pl.num_programs(1) + pl.program_id(1), lo_ref, hi_ref, tail_ref, pos_ref,
                  inv_ref)
    x = wgu_ref[0].astype(BF16)
    r = lax.broadcasted_iota(jnp.int32, (DEINT_GROUP, LANES), 0)
    c = lax.broadcasted_iota(jnp.int32, (DEINT_GROUP, LANES), 1)
    sel_even = (r == 2 * c).astype(BF16)
    sel_odd = (r == 2 * c + 1).astype(BF16)
    for k in range(x.shape[1] // DEINT_GROUP):
        blk = x[:, DEINT_GROUP * k:DEINT_GROUP * (k + 1)]
        wg_o[0, :, LANES * k:LANES * (k + 1)] = jnp.dot(blk, sel_even, preferred_element_type=F32).astype(BF16)
        wu_o[0, :, LANES * k:LANES * (k + 1)] = jnp.dot(blk, sel_odd, preferred_element_type=F32).astype(BF16)


def _prep_weights_and_slots(w_gu, pos_flat, pad_lo, pad_hi, rows_used, n_rows, tn=512):
    n_e, d, f2 = w_gu.shape
    f = f2 // 2
    nj = f // tn
    n_inv = FFN_ROWS + n_rows
    chunk = pos_flat.shape[0] // (n_e * nj)
    assert chunk * n_e * nj == pos_flat.shape[0] and chunk % INV_UNROLL == 0
    tail = jnp.stack([(FFN_ROWS + rows_used) // INV_UNROLL, jnp.int32(n_inv // INV_UNROLL)]).astype(jnp.int32)
    smem = pl.BlockSpec(memory_space=pltpu.SMEM)
    return pl.pallas_call(
        _prep_kernel,
        grid=(n_e, nj),
        in_specs=[
            smem, smem, smem,
            pl.BlockSpec((chunk,), lambda e, j: (e * nj + j,), memory_space=pltpu.SMEM),
            pl.BlockSpec((1, d, 2 * tn), lambda e, j: (e, 0, j)),
        ],
        out_specs=[
            pl.BlockSpec((1, d, tn), lambda e, j: (e, 0, j)),
            pl.BlockSpec((1, d, tn), lambda e, j: (e, 0, j)),
            smem,
        ],
        out_shape=[jax.ShapeDtypeStruct((n_e, d, f), BF16), jax.ShapeDtypeStruct((n_e, d, f), BF16),
                   jax.ShapeDtypeStruct((n_inv,), jnp.int32)],
        compiler_params=_cparams(("arbitrary", "arbitrary")),
        name="expert_weight_prep",
    )(FFN_ROWS + pad_lo, FFN_ROWS + pad_hi, tail, pos_flat, w_gu)


def _ffn_kernel(be_ref, rows_ref, nv_ref, inv_ref, u_hbm, wg_ref, wu_ref, wd_ref, bg_ref, bu_ref, bd_ref,
                y_hbm, xg_buf, xb_scr, acc_scr, o_buf, sem):
    i = pl.program_id(0)
    j = pl.program_id(1)
    nv = nv_ref[0]
    tm, d = xb_scr.shape
    half = d // 2
    trash0 = y_hbm.shape[0] // PACK_SUB - tm

    tf = wg_ref.shape[2]
    n_chunks = tf // MXU_TILE + d // MXU_TILE
    bounds = [tm * g // n_chunks for g in range(n_chunks + 1)]

    def gather_issue(blk, g=None):
        base = (blk + 1) * tm
        for r in range(tm) if g is None else range(bounds[g], bounds[g + 1]):
            tok = jnp.maximum(inv_ref[base + r], 0) >> 2
            pltpu.make_async_copy(_row_tile(u_hbm, tok), _row_tile(xg_buf, r), sem.at[0]).start()

    def gather_wait():
        pltpu.make_async_copy(_row_tile(u_hbm, 0, tm), xg_buf, sem.at[0]).wait()

    def scatter_issue(blk, g=None):
        base = (blk + 1) * tm
        for r in range(tm) if g is None else range(bounds[g], bounds[g + 1]):
            a = inv_ref[base + r]
            pltpu.make_async_copy(_row_tile(o_buf, r), _row_tile(y_hbm, jnp.where(a < 0, trash0 + r, a)),
                                  sem.at[1]).start(priority=r % 2)

    def scatter_wait():
        pltpu.make_async_copy(o_buf, _row_tile(y_hbm, 0, tm), sem.at[1]).wait()

    @pl.when(i < nv)
    def _():
        par = i % 2
        full = rows_ref[i] > tm // 2

        def tile(n, first, between):
            xb = xb_scr[0:n, :]
            g = 0
            acts = []
            for k in range(tf // MXU_TILE):
                cs = slice(MXU_TILE * k, MXU_TILE * (k + 1))
                gate = jnp.minimum(jnp.dot(xb, wg_ref[0, :, cs], preferred_element_type=F32) + bg_ref[0, :, cs],
                                   SWIGLU_LIMIT)
                up = jnp.clip(jnp.dot(xb, wu_ref[0, :, cs], preferred_element_type=F32) + bu_ref[0, :, cs],
                              -SWIGLU_LIMIT, SWIGLU_LIMIT)
                acts.append((gate * _sigmoid(gate * SWIGLU_ALPHA) * (up + 1.0)).astype(BF16))
                between(g)
                g += 1
            act = jnp.concatenate(acts, axis=1)
            for k in range(d // MXU_TILE):
                cs = slice(MXU_TILE * k, MXU_TILE * (k + 1))
                y = jnp.dot(act, wd_ref[0, :, cs].astype(BF16), preferred_element_type=F32)
                if first:
                    acc_scr[par, 0:n, cs] = y + bd_ref[0, :, cs]
                    if n < tm:
                        acc_scr[par, n:tm, cs] = jnp.zeros((tm - n, MXU_TILE), F32)
                else:
                    acc_scr[par, 0:n, cs] += y
                between(g)
                g += 1

        def either_size(fn):
            @pl.when(full)
            def _():
                fn(tm)

            @pl.when(jnp.logical_not(full))
            def _():
                fn(tm // 2)

        @pl.when(j == 0)
        def _():
            @pl.when(i == 0)
            def _():
                acc_scr[1] = jnp.zeros((tm, d), F32)
                gather_issue(0)

            @pl.when(i > 0)
            def _():
                scatter_wait()

            gather_wait()
            for c in range(PACK_SUB):
                lo, hi = _unpack_chunk(xg_buf[pl.ds(c, tm, stride=PACK_SUB), :])
                xb_scr[:, LANES * c:LANES * (c + 1)] = lo.astype(BF16)
                xb_scr[:, half + LANES * c:half + LANES * (c + 1)] = hi.astype(BF16)
            _pack_rows(acc_scr[1 - par], o_buf)
            nxt = jnp.minimum(i + 1, nv - 1)
            either_size(lambda n: tile(n, True, lambda g: gather_issue(nxt, g)))

        @pl.when(j == 1)
        def _():
            either_size(lambda n: tile(n, False, lambda g: scatter_issue(i - 1, g)))

            @pl.when(i == nv - 1)
            def _():
                scatter_wait()
                _pack_rows(acc_scr[par], o_buf)
                scatter_issue(i)
                scatter_wait()
                gather_wait()


def _ffn(u_packed, inv, block_e, block_rows, n_valid, w_g, w_u, w_d, b_g, b_u, b_d, tm=FFN_ROWS):
    n_e, f, d = w_d.shape
    nf = 2
    tf = f // nf
    n_blocks = inv.shape[0] // tm - 1
    n_asg = (u_packed.shape[0] // PACK_SUB) * TOP_K
    ic = lambda i, nv: jnp.minimum(i, nv[0] - 1)
    jc = lambda i, j, nv: jnp.where(i < nv[0], j, nf - 1)
    col = lambda i, j, be, br, nv, iv: (be[ic(i, nv)], 0, jc(i, j, nv))
    return pl.pallas_call(
        _ffn_kernel,
        grid_spec=pltpu.PrefetchScalarGridSpec(
            num_scalar_prefetch=4,
            grid=(n_blocks, nf),
            in_specs=[
                pl.BlockSpec(memory_space=pl.ANY),
                pl.BlockSpec((1, d, tf), col),
                pl.BlockSpec((1, d, tf), col),
                pl.BlockSpec((1, tf, d), lambda i, j, be, br, nv, iv: (be[ic(i, nv)], jc(i, j, nv), 0)),
                pl.BlockSpec((1, 1, tf), col),
                pl.BlockSpec((1, 1, tf), col),
                pl.BlockSpec((1, 1, d), lambda i, j, be, br, nv, iv: (be[ic(i, nv)], 0, 0)),
            ],
            out_specs=pl.BlockSpec(memory_space=pl.ANY),
            scratch_shapes=[
                pltpu.VMEM((tm * PACK_SUB, LANES), jnp.uint32),
                pltpu.VMEM((tm, d), BF16),
                pltpu.VMEM((2, tm, d), F32),
                pltpu.VMEM((tm * PACK_SUB, LANES), jnp.uint32),
                pltpu.SemaphoreType.DMA((2,)),
            ],
        ),
        out_shape=jax.ShapeDtypeStruct(((n_asg + tm) * PACK_SUB, LANES), jnp.uint32),
        compiler_params=_cparams(("arbitrary", "arbitrary")),
        name="expert_ffn",
    )(block_e, block_rows, n_valid, inv, u_packed, w_g, w_u, w_d, b_g.reshape(n_e, 1, f),
      b_u.reshape(n_e, 1, f), b_d.reshape(n_e, 1, d))


COMBINE_ROWS = 256


def _combine_kernel(y_ref, tg_ref, h1_ref, p_ref, wpg_ref, wpp_ref, gp_ref, gpp_ref, gfin_ref, o_ref):
    tc = tg_ref.shape[0]
    tg = tg_ref[...]
    gks = [jnp.broadcast_to(tg[:, kk:kk + 1], (tc, LANES)) for kk in range(TOP_K)]
    lo_parts, hi_parts = [], []
    for c in range(PACK_SUB):
        lo_sum = hi_sum = None
        for kk in range(TOP_K):
            lo, hi = _unpack_chunk(y_ref[pl.ds(PACK_SUB * kk + c, tc, stride=PACK_SUB * TOP_K), :])
            gk = gks[kk]
            lo_sum = gk * lo if lo_sum is None else lo_sum + gk * lo
            hi_sum = gk * hi if hi_sum is None else hi_sum + gk * hi
        lo_parts.append(lo_sum)
        hi_parts.append(hi_sum)
    moe = jnp.concatenate(lo_parts + hi_parts, axis=1)
    h2 = h1_ref[...] + moe
    gate = _sigmoid(_bdot(_rms(h2, gp_ref[...]), wpg_ref[...]))
    pe = _rms(_bdot(p_ref[...], wpp_ref[...]), gpp_ref[...])
    h3 = h2 + gate * pe
    o_ref[...] = _rms(h3, gfin_ref[...])


def _combine(y_tok, tg, h1, p2, w_pg, w_pp, g_ple, g_ple_post, g_final, tc=COMBINE_ROWS):
    t, d = h1.shape
    pd = p2.shape[1]
    row = lambda w: pl.BlockSpec((tc, w), lambda i: (i, 0))
    const = lambda shp: pl.BlockSpec(shp, lambda i: (0, 0))
    return pl.pallas_call(
        _combine_kernel,
        grid=(t // tc,),
        in_specs=[
            pl.BlockSpec((tc * TOP_K * PACK_SUB, LANES), lambda i: (i, 0)),
            row(LANES), row(d), row(pd),
            const((d, d)), const((pd, d)), const((1, d)), const((1, d)), const((1, d)),
        ],
        out_specs=row(d),
        out_shape=jax.ShapeDtypeStruct((t, d), F32),
        compiler_params=_cparams(("parallel",)),
        name="combine_ple",
    )(y_tok, tg, h1, p2, w_pg, w_pp, g_ple.reshape(1, d), g_ple_post.reshape(1, d), g_final.reshape(1, d))


def _routing_tables(top_i, n_tok, tm=FFN_ROWS):
    n_asg = n_tok * TOP_K
    e_flat = top_i.reshape(n_asg)
    onehot = (e_flat[:, None] == jnp.arange(N_EXPERTS, dtype=jnp.int32)[None, :]).astype(jnp.int32)
    grp = onehot.reshape(n_asg // LANES, LANES, N_EXPERTS).astype(F32)
    within = jnp.einsum("ij,gjk->gik", jnp.tril(jnp.ones((LANES, LANES), F32)), grp).astype(jnp.int32)
    tot = within[:, -1, :]
    csum = (within + (jnp.cumsum(tot, axis=0) - tot)[:, None, :]).reshape(n_asg, N_EXPERTS)
    counts = csum[-1]
    padded = (counts + tm - 1) // tm * tm
    pend = jnp.cumsum(padded)
    pstart = pend - padded
    pos = jnp.sum(onehot * (csum - 1 + pstart[None, :]), axis=1).astype(jnp.int32)
    n_blocks = (n_tok * TOP_K) // tm + N_EXPERTS
    block_start = jnp.arange(n_blocks, dtype=jnp.int32) * tm
    block_e = jnp.minimum(jnp.sum(pend[None, :] <= block_start[:, None], axis=1), N_EXPERTS - 1).astype(jnp.int32)
    block_rows = jnp.clip((pstart + counts)[block_e] - block_start, 0, tm).astype(jnp.int32)
    n_valid = (pend[-1] // tm).astype(jnp.int32).reshape(1)
    pad_lo = (pstart + counts).astype(jnp.int32)
    return pos, pad_lo, pend.astype(jnp.int32), block_e, block_rows, n_valid, n_blocks * tm


def kernel(x, p, g_mix, w_in, conv_w, a_log, dt_bias, gdn_norm, ml_b_i, ml_b_f, ml_norm, w_out, g_ffn, w_router, b_router, w_gu, b_gu, w_down, b_down, g_ple, w_ple_gate, w_ple_proj, g_ple_post, g_final):
    b, s, d = x.shape
    n_tok = b * s
    assert w_in.shape[0] == 1, "single-layer block: the final norm is fused into the layer's last kernel"
    l = 0
    x2 = x.reshape(n_tok, d)
    w_main, w_gate = _prep_in_weights(w_in[l])
    z_main, z_gate = _in_proj(x2, g_mix[l], w_main, w_gate)
    z_main = z_main.reshape(b, s, Z_MAIN)
    z_gate = z_gate.reshape(b, s, LANES)
    y_gdn = _gdn(z_main, z_gate, conv_w[l], a_log[l], dt_bias[l], gdn_norm[l])
    y_ml = _mlstm(z_main, z_gate, ml_b_i[l], ml_b_f[l], ml_norm[l])

    w_r = jnp.pad(w_router[l], ((0, 0), (0, LANES - N_EXPERTS)))
    b_r = jnp.pad(b_router[l], (0, LANES - N_EXPERTS)).reshape(1, LANES)
    h1, u2, top_i, top_g = _out_proj(x2, y_gdn.reshape(n_tok, GDN_V), y_ml.reshape(n_tok, ML_V),
                                     w_out[l].astype(BF16), g_ffn[l], w_r, b_r)

    pos, pad_lo, pend, block_e, block_rows, n_valid, n_rows = _routing_tables(top_i[:, :TOP_K], n_tok)
    w_g, w_u, inv = _prep_weights_and_slots(w_gu[l], pos, pad_lo, pend, pend[-1], n_rows)
    y_tok = _ffn(u2, inv, block_e, block_rows, n_valid, w_g, w_u, w_down[l],
                 b_gu[l][:, 0::2], b_gu[l][:, 1::2], b_down[l])
    out = _combine(y_tok, top_g, h1, p[l].reshape(n_tok, -1), w_ple_gate[l].astype(BF16),
                   w_ple_proj[l].astype(BF16), g_ple[l], g_ple_post[l], g_final)
    return out.reshape(b, s, d)
```

```python
import jax
import jax.numpy as jnp
from jax import lax
from jax.experimental import pallas as pl
from jax.experimental.pallas import tpu as pltpu

F32 = jnp.float32
BF16 = jnp.bfloat16

EPS = 1e-6
CHUNK = 64
GDN_HEADS = 8
GDN_DK = 128
GDN_DV = 128
GDN_CONV = 4
ML_HEADS = 4
ML_DQK = 128
ML_DV = 256
GATE_SOFTCAP = 15.0
N_EXPERTS = 32
TOP_K = 4
SWIGLU_LIMIT = 7.0
SWIGLU_ALPHA = 1.702

LANES = 128
HALO = 8
NEG_BIG = -1e30

GDN_QK = GDN_HEADS * GDN_DK
GDN_V = GDN_HEADS * GDN_DV
ML_QK = ML_HEADS * ML_DQK
ML_V = ML_HEADS * ML_DV
OFF_Q = 0
OFF_K = OFF_Q + GDN_QK
OFF_V = OFF_K + GDN_QK
OFF_GZ = OFF_V + GDN_V
OFF_MQ = OFF_GZ + GDN_V
OFF_MK = OFF_MQ + ML_QK
OFF_MV = OFF_MK + ML_QK
OFF_MO = OFF_MV + ML_V
Z_MAIN = OFF_MO + ML_V
GATE_A = 0
GATE_B = GATE_A + GDN_HEADS
GATE_I = GATE_B + GDN_HEADS
GATE_F = GATE_I + ML_HEADS

GDN_TIME_BLOCK = 128
ML_TIME_BLOCK = 256
GDN_HEADS_PER_STEP = 8
ML_HEADS_PER_STEP = 4
VMEM_LIMIT = 56 * 1024 * 1024


def _cparams(sem):
    return pltpu.CompilerParams(dimension_semantics=sem, vmem_limit_bytes=VMEM_LIMIT)


def _bdot(a, b):
    return jnp.dot(a.astype(BF16), b.astype(BF16), preferred_element_type=F32)


def _bdot_nt(a, b):
    return lax.dot_general(a.astype(BF16), b.astype(BF16), (((1,), (1,)), ((), ())),
                           preferred_element_type=F32)


def _bdot_tn(a, b):
    return lax.dot_general(a.astype(BF16), b.astype(BF16), (((0,), (0,)), ((), ())),
                           preferred_element_type=F32)


def _split3(x):
    hi = x.astype(BF16)
    r1 = x - hi.astype(F32)
    mid = r1.astype(BF16)
    lo = (r1 - mid.astype(F32)).astype(BF16)
    return hi, mid, lo


def _dot_sel(sel_bf16, x):
    hi, mid, lo = _split3(x)
    d = lambda t: jnp.dot(sel_bf16, t, preferred_element_type=F32)
    return d(hi) + (d(mid) + d(lo))


def _sigmoid(x):
    return 1.0 / (1.0 + jnp.exp(-x))


def _softplus(x):
    return jnp.maximum(x, 0.0) + jnp.log1p(jnp.exp(-jnp.abs(x)))


def _rms(x, g):
    return x * lax.rsqrt(jnp.mean(x * x, axis=-1, keepdims=True) + EPS) * g


def _lane_pick(zg, idx):
    lane = lax.broadcasted_iota(jnp.int32, zg.shape, 1)
    return jnp.sum(jnp.where(lane == idx, zg, 0.0), axis=1, keepdims=True)


def _on_lanes(v, start):
    return jnp.pad(v.astype(F32), (start, LANES - start - v.shape[0])).reshape(1, LANES)


def _chunk_masks(n):
    r = lax.broadcasted_iota(jnp.int32, (n, n), 0)
    c = lax.broadcasted_iota(jnp.int32, (n, n), 1)
    same = lambda s: (r >> s) == (c >> s)
    return r, c, same


def _chunk_last(gc):
    n = gc.shape[0] // CHUNK
    parts = [jnp.broadcast_to(gc[CHUNK * (i + 1) - 1:CHUNK * (i + 1), :], (CHUNK, gc.shape[1]))
             for i in range(n)]
    return jnp.concatenate(parts, axis=0)


def _inproj_kernel(x_ref, g_ref, w_ref, wg_ref, z_ref, zg_ref, u_scr):
    @pl.when(pl.program_id(1) == 0)
    def _():
        u = _rms(x_ref[...], g_ref[...]).astype(BF16)
        u_scr[...] = u
        zg_ref[...] = jnp.dot(u, wg_ref[...], preferred_element_type=F32)

    z_ref[...] = jnp.dot(u_scr[...], w_ref[...], preferred_element_type=F32).astype(z_ref.dtype)


def _in_proj(x2, g_mix, w_main, w_gate, tm=1024, tn=Z_MAIN // 4):
    t, d = x2.shape
    n = w_main.shape[1]
    return pl.pallas_call(
        _inproj_kernel,
        grid=(t // tm, n // tn),
        in_specs=[
            pl.BlockSpec((tm, d), lambda i, j: (i, 0)),
            pl.BlockSpec((1, d), lambda i, j: (0, 0)),
            pl.BlockSpec((d, tn), lambda i, j: (0, j)),
            pl.BlockSpec((d, LANES), lambda i, j: (0, 0)),
        ],
        out_specs=[
            pl.BlockSpec((tm, tn), lambda i, j: (i, j)),
            pl.BlockSpec((tm, LANES), lambda i, j: (i, 0)),
        ],
        out_shape=[jax.ShapeDtypeStruct((t, n), BF16), jax.ShapeDtypeStruct((t, LANES), F32)],
        scratch_shapes=[pltpu.VMEM((tm, d), BF16)],
        compiler_params=_cparams(("parallel", "arbitrary")),
        name="in_proj",
    )(x2, g_mix.reshape(1, d), w_main, w_gate)


def _prep_in_weights(w_in):
    splits = (GDN_QK, GDN_QK, GDN_V, GDN_HEADS, GDN_HEADS, GDN_V, ML_QK, ML_QK, ML_V, ML_HEADS, ML_HEADS, ML_V)
    offs = [0]
    for wd in splits:
        offs.append(offs[-1] + wd)
    part = lambda i: w_in[:, offs[i]:offs[i + 1]]
    w_main = jnp.concatenate([part(i) for i in (0, 1, 2, 5, 6, 7, 8, 11)], axis=1).astype(BF16)
    gates = jnp.concatenate([part(i) for i in (3, 4, 9, 10)], axis=1)
    w_gate = jnp.pad(gates, ((0, 0), (0, LANES - gates.shape[1]))).astype(BF16)
    return w_main, w_gate


def _tri_inverse_minus_eye(ms, same):
    m16 = same(4)
    m32 = same(5)
    n32_mask = jnp.logical_and(m32, jnp.logical_not(m16))
    a = [jnp.where(m16, -m, 0.0) for m in ms]
    acc = list(a)
    for _ in range(3):
        a = [_bdot(x, x) for x in a]
        acc = [p + x + _bdot(p, x) for p, x in zip(acc, a)]
    for level in range(2):
        ns = [jnp.where(n32_mask, m, 0.0) if level == 0 else jnp.where(m32, 0.0, m) for m in ms]
        ys = [n + _bdot(p, n) for p, n in zip(acc, ns)]
        acc = [p - (y + _bdot(y, p)) for p, y in zip(acc, ys)]
    return acc


def _gdn_kernel(alog_ref, dtb_ref, zq_ref, zk_ref, zv_ref, gz_ref, zg_ref, cwq_ref, cwk_ref, cwv_ref,
                gn_ref, o_ref, s_scr, xq_scr, xk_scr, xv_scr):
    hb = s_scr.shape[0]
    h0 = pl.program_id(1) * hb
    tb = zq_ref.shape[1]
    nchunk = tb // CHUNK
    heads = range(hb)
    lanes = lambda hh: slice(LANES * hh, LANES * (hh + 1))

    @pl.when(pl.program_id(2) == 0)
    def _():
        s_scr[...] = jnp.zeros_like(s_scr)
        for scr in (xq_scr, xk_scr, xv_scr):
            scr[0:HALO, :] = jnp.zeros((HALO, scr.shape[1]), F32)

    def conv_silu(z_ref, x_scr, cw_ref):
        x = z_ref[0].astype(F32)
        x_scr[HALO:HALO + tb, :] = x
        w = cw_ref[...]
        acc = x * w[GDN_CONV - 1:GDN_CONV, :]
        for s in range(1, GDN_CONV):
            acc = acc + x_scr[pl.ds(HALO - s, tb), :] * w[GDN_CONV - 1 - s:GDN_CONV - s, :]
        x_scr[0:HALO, :] = x[tb - HALO:tb, :]
        return acc * _sigmoid(acc)

    q_all = conv_silu(zq_ref, xq_scr, cwq_ref)
    k_all = conv_silu(zk_ref, xk_scr, cwk_ref)
    v_all = conv_silu(zv_ref, xv_scr, cwv_ref)
    gz_all = gz_ref[0].astype(F32)
    zg = zg_ref[0]
    g_all = -jnp.exp(alog_ref[...]) * _softplus(zg + dtb_ref[...])
    beta_all = _sigmoid(zg)

    r, c, same = _chunk_masks(tb)
    in_chunk = same(6)
    tril = jnp.logical_and(in_chunk, r >= c)
    strict = jnp.logical_and(in_chunk, r > c)
    tril_bf = tril.astype(BF16)
    gc_all = _dot_sel(tril_bf, g_all)

    def head_prep(hh):
        q = q_all[:, lanes(hh)]
        k = k_all[:, lanes(hh)]
        v = v_all[:, lanes(hh)]
        q = q * lax.rsqrt(jnp.sum(q * q, axis=-1, keepdims=True) + EPS) * (GDN_DK ** -0.5)
        k = k * lax.rsqrt(jnp.sum(k * k, axis=-1, keepdims=True) + EPS)
        beta = _lane_pick(beta_all, GATE_B + h0 + hh)
        gcb = jnp.broadcast_to(_lane_pick(gc_all, GATE_A + h0 + hh), (tb, LANES))
        gc_row = gcb.T[0:1, :]
        gc_col = jnp.concatenate([gcb] * (tb // LANES), axis=1)
        decay = jnp.exp(jnp.where(tril, gc_col - gc_row, NEG_BIG))
        eg = jnp.exp(gcb)
        kb = k * beta
        m_low = jnp.where(strict, _bdot_nt(kb, k) * decay, 0.0)
        rhs = jnp.concatenate([v * beta, kb * eg], axis=1)
        attn = _bdot_nt(q, k) * decay
        kd = k * jnp.exp(_chunk_last(gcb) - gcb)
        return dict(m_low=m_low, rhs=rhs, attn=attn, qd=q * eg, kd=kd, gcb=gcb)

    hp = [head_prep(hh) for hh in heads]
    t_m1 = _tri_inverse_minus_eye([p["m_low"] for p in hp], same)
    uws = [p["rhs"] + _bdot(t, p["rhs"]) for p, t in zip(hp, t_m1)]

    states = [s_scr[hh] for hh in heads]
    outs = [[] for _ in heads]
    for i in range(nchunk):
        sl = slice(CHUNK * i, CHUNK * (i + 1))
        for hh in heads:
            p, uw = hp[hh], uws[hh]
            res1 = _bdot(jnp.concatenate([uw[sl, GDN_DV:], p["qd"][sl]], axis=0), states[hh])
            v_new = uw[sl, :GDN_DV] - res1[:CHUNK]
            o_intra = _bdot(p["attn"][sl, CHUNK * i:CHUNK * (i + 1)], v_new)
            outs[hh].append(res1[CHUNK:] + o_intra)
            g_last = jnp.exp(p["gcb"][CHUNK * (i + 1) - 1:CHUNK * (i + 1), :])
            states[hh] = states[hh] * g_last[:, 0:1] + _bdot_tn(p["kd"][sl], v_new)

    for hh in heads:
        s_scr[hh] = states[hh]
        o = _rms(jnp.concatenate(outs[hh], axis=0), gn_ref[...])
        gz = gz_all[:, lanes(hh)]
        o_ref[0, :, lanes(hh)] = (o * (gz * _sigmoid(gz))).astype(o_ref.dtype)


def _gdn(z_main, z_gate, conv_w, a_log, dt_bias, gdn_norm, tb=GDN_TIME_BLOCK, hb=GDN_HEADS_PER_STEP):
    b, s, _ = z_main.shape
    wid = hb * LANES
    hq, hk, hv, hz = OFF_Q // wid, OFF_K // wid, OFF_V // wid, OFF_GZ // wid
    zspec = lambda off: pl.BlockSpec((1, tb, wid), lambda bi, hi, ti: (bi, ti, off + hi))
    cspec = lambda off: pl.BlockSpec((GDN_CONV, wid), lambda bi, hi, ti: (0, off + hi))
    lane_row = pl.BlockSpec((1, LANES), lambda bi, hi, ti: (0, 0))
    return pl.pallas_call(
        _gdn_kernel,
        grid=(b, GDN_HEADS // hb, s // tb),
        in_specs=[
            lane_row, lane_row,
            zspec(hq), zspec(hk), zspec(hv), zspec(hz),
            pl.BlockSpec((1, tb, LANES), lambda bi, hi, ti: (bi, ti, 0)),
            cspec(hq), cspec(hk), cspec(hv),
            pl.BlockSpec((1, GDN_DV), lambda bi, hi, ti: (0, 0)),
        ],
        out_specs=pl.BlockSpec((1, tb, wid), lambda bi, hi, ti: (bi, ti, hi)),
        out_shape=jax.ShapeDtypeStruct((b, s, GDN_V), BF16),
        scratch_shapes=[
            pltpu.VMEM((hb, GDN_DK, GDN_DV), F32),
            pltpu.VMEM((HALO + tb, wid), F32),
            pltpu.VMEM((HALO + tb, wid), F32),
            pltpu.VMEM((HALO + tb, wid), F32),
        ],
        compiler_params=_cparams(("parallel", "parallel", "arbitrary")),
        name="gdn",
    )(_on_lanes(a_log, GATE_A), _on_lanes(dt_bias, GATE_A), z_main, z_main, z_main, z_main, z_gate,
      conv_w, conv_w, conv_w, gdn_norm.reshape(1, GDN_DV))


def _mlstm_kernel(bias_ref, q_ref, k_ref, v_ref, og_ref, zg_ref, nrm_ref, o_ref, c_scr, m_scr):
    hb = c_scr.shape[0]
    h0 = pl.program_id(1) * hb
    tb = q_ref.shape[1]
    nchunk = tb // CHUNK
    cap = GATE_SOFTCAP

    @pl.when(pl.program_id(2) == 0)
    def _():
        c_scr[...] = jnp.zeros_like(c_scr)
        m_scr[...] = jnp.zeros_like(m_scr)

    capped = cap * jnp.tanh((zg_ref[0] + bias_ref[...]) / cap)
    logf_all = -_softplus(-capped)
    r, c, same = _chunk_masks(tb)
    tril = jnp.logical_and(same(6), r >= c)
    tril_bf = tril.astype(BF16)
    ones = jnp.ones((tb, LANES), F32)
    fc_all = _dot_sel(tril_bf, logf_all)

    for hh in range(hb):
        q = q_ref[0, :, ML_DQK * hh:ML_DQK * (hh + 1)].astype(F32)
        k = k_ref[0, :, ML_DQK * hh:ML_DQK * (hh + 1)].astype(F32) * (ML_DQK ** -0.5)
        v = v_ref[0, :, ML_DV * hh:ML_DV * (hh + 1)].astype(F32)
        i_pre = _lane_pick(capped, GATE_I + h0 + hh)
        fcb = jnp.broadcast_to(_lane_pick(fc_all, GATE_F + h0 + hh), (tb, LANES))
        fmi = fcb - i_pre
        fmi_row = fmi.T[0:1, :]
        fc_col = jnp.concatenate([fcb] * (tb // LANES), axis=1)
        d_mat = jnp.where(tril, fc_col - fmi_row, NEG_BIG)
        d_max = jnp.max(d_mat, axis=-1, keepdims=True)
        a_end = _chunk_last(fcb) - fmi

        m_st = m_scr[hh]
        m_rows, decs, wks = [], [], []
        for i in range(nchunk):
            sl = slice(CHUNK * i, CHUNK * (i + 1))
            f_last = fcb[CHUNK * (i + 1) - 1:CHUNK * (i + 1), :]
            a_max = jnp.max(a_end[sl], axis=0, keepdims=True)
            m_new = jnp.maximum(f_last + m_st, a_max)
            m_rows.append(jnp.broadcast_to(m_st, (CHUNK, LANES)))
            decs.append(jnp.exp(f_last + m_st - m_new))
            wks.append(jnp.exp(a_end[sl] - m_new))
            m_st = m_new
        m_scr[hh] = m_st

        inter = fcb + jnp.concatenate(m_rows, axis=0)
        m_i = jnp.maximum(d_max, inter)
        s_inter = jnp.exp(inter - m_i)
        wts = jnp.exp(d_mat - m_i[:, 0:1]) * _bdot_nt(q, k)
        v_aug = jnp.concatenate([v, ones], axis=1)
        intra = _bdot(wts, v_aug)

        cst = c_scr[hh]
        nums = []
        for i in range(nchunk):
            sl = slice(CHUNK * i, CHUNK * (i + 1))
            nums.append(s_inter[sl, 0:1] * _bdot(q[sl], cst) + intra[sl])
            cst = decs[i][:, 0:1] * cst + _bdot_tn(wks[i] * k[sl], v_aug[sl])
        c_scr[hh] = cst

        num_aug = jnp.concatenate(nums, axis=0)
        den = num_aug[:, ML_DV:ML_DV + 1]
        hout = num_aug[:, :ML_DV] / jnp.maximum(jnp.abs(den), jnp.exp(-m_i[:, 0:1]))
        hout = _rms(hout, nrm_ref[hh])
        og = og_ref[0, :, ML_DV * hh:ML_DV * (hh + 1)].astype(F32)
        o_ref[0, :, ML_DV * hh:ML_DV * (hh + 1)] = (hout * _sigmoid(og)).astype(o_ref.dtype)


def _mlstm(z_main, z_gate, ml_b_i, ml_b_f, ml_norm, tb=ML_TIME_BLOCK, hb=ML_HEADS_PER_STEP):
    b, s, _ = z_main.shape
    qw, vw = hb * ML_DQK, hb * ML_DV
    hq, hk = OFF_MQ // qw, OFF_MK // qw
    hv, ho = OFF_MV // vw, OFF_MO // vw
    qspec = lambda off: pl.BlockSpec((1, tb, qw), lambda bi, hi, ti: (bi, ti, off + hi))
    vspec = lambda off: pl.BlockSpec((1, tb, vw), lambda bi, hi, ti: (bi, ti, off + hi))
    gate_bias = _on_lanes(ml_b_i, GATE_I) + _on_lanes(ml_b_f, GATE_F)
    return pl.pallas_call(
        _mlstm_kernel,
        grid=(b, ML_HEADS // hb, s // tb),
        in_specs=[
            pl.BlockSpec((1, LANES), lambda bi, hi, ti: (0, 0)),
            qspec(hq), qspec(hk), vspec(hv), vspec(ho),
            pl.BlockSpec((1, tb, LANES), lambda bi, hi, ti: (bi, ti, 0)),
            pl.BlockSpec((hb, 1, ML_DV), lambda bi, hi, ti: (hi, 0, 0)),
        ],
        out_specs=pl.BlockSpec((1, tb, vw), lambda bi, hi, ti: (bi, ti, hi)),
        out_shape=jax.ShapeDtypeStruct((b, s, ML_V), BF16),
        scratch_shapes=[
            pltpu.VMEM((hb, ML_DQK, ML_DV + LANES), F32),
            pltpu.VMEM((hb, 1, LANES), F32),
        ],
        compiler_params=_cparams(("parallel", "parallel", "arbitrary")),
        name="mlstm",
    )(gate_bias, z_main, z_main, z_main, z_main, z_gate, ml_norm.reshape(ML_HEADS, 1, ML_DV))


PACK_SUB = 8


def _pack_rows(x, o_ref):
    n, d = x.shape
    half = d // 2
    for j in range(PACK_SUB):
        lo = x[:, LANES * j:LANES * (j + 1)].astype(BF16).astype(F32)
        hi = x[:, half + LANES * j:half + LANES * (j + 1)].astype(BF16).astype(F32)
        word = (lax.bitcast_convert_type(lo, jnp.uint32) >> 16) | lax.bitcast_convert_type(hi, jnp.uint32)
        o_ref[pl.ds(j, n, stride=PACK_SUB), :] = word


def _unpack_chunk(word):
    lo = lax.bitcast_convert_type(word << 16, F32)
    hi = lax.bitcast_convert_type(word & jnp.uint32(0xFFFF0000), F32)
    return lo, hi


def _row_tile(ref, row, n=1):
    start = row * PACK_SUB
    if not isinstance(start, int):
        start = pl.multiple_of(start, PACK_SUB)
    return ref.at[pl.ds(start, n * PACK_SUB), :]


def _outproj_kernel(x_ref, yg_ref, ym_ref, wo1_ref, wo2_ref, gf_ref, wr_ref, br_ref,
                    h_ref, u_ref, ti_ref, tg_ref):
    h1 = (x_ref[...] + jnp.dot(yg_ref[...], wo1_ref[...], preferred_element_type=F32)
          + jnp.dot(ym_ref[...], wo2_ref[...], preferred_element_type=F32))
    h_ref[...] = h1
    u = _rms(h1, gf_ref[...])
    _pack_rows(u, u_ref)

    u_hi = u.astype(BF16)
    u_lo = (u - u_hi.astype(F32)).astype(BF16)
    wr = wr_ref[...]
    w_hi = wr.astype(BF16)
    w_lo = (wr - w_hi.astype(F32)).astype(BF16)
    d = lambda a, b: jnp.dot(a, b, preferred_element_type=F32)
    logits = d(u_hi, w_hi) + (d(u_hi, w_lo) + d(u_lo, w_hi)) + br_ref[...]

    lane = lax.broadcasted_iota(jnp.int32, logits.shape, 1)
    lg = jnp.where(lane < N_EXPERTS, logits, NEG_BIG)
    vals, idxs = [], []
    for _ in range(TOP_K):
        m = jnp.max(lg, axis=1, keepdims=True)
        idx = jnp.min(jnp.where(lg == m, lane, LANES), axis=1, keepdims=True)
        vals.append(m)
        idxs.append(idx)
        lg = jnp.where(lane == idx, NEG_BIG, lg)
    es = [jnp.exp(vv - vals[0]) for vv in vals]
    tot = es[0] + es[1] + es[2] + es[3]
    ti = jnp.zeros(logits.shape, jnp.int32)
    tg = jnp.zeros(logits.shape, F32)
    for kk in range(TOP_K):
        ti = jnp.where(lane == kk, idxs[kk], ti)
        tg = jnp.where(lane == kk, es[kk] / tot, tg)
    ti_ref[...] = ti
    tg_ref[...] = tg


def _out_proj(x2, y_gdn, y_ml, w_out_bf, g_ffn, w_router_pad, b_router_pad, tm=512):
    t, d = x2.shape
    row = lambda w: pl.BlockSpec((tm, w), lambda i: (i, 0))
    const = lambda shp: pl.BlockSpec(shp, lambda i: (0, 0))
    return pl.pallas_call(
        _outproj_kernel,
        grid=(t // tm,),
        in_specs=[
            row(d), row(GDN_V), row(ML_V),
            pl.BlockSpec((GDN_V, d), lambda i: (0, 0)),
            pl.BlockSpec((ML_V, d), lambda i: (GDN_V // ML_V, 0)),
            const((1, d)), const((d, LANES)), const((1, LANES)),
        ],
        out_specs=[row(d), pl.BlockSpec((tm * PACK_SUB, LANES), lambda i: (i, 0)), row(LANES), row(LANES)],
        out_shape=[jax.ShapeDtypeStruct((t, d), F32), jax.ShapeDtypeStruct((t * PACK_SUB, LANES), jnp.uint32),
                   jax.ShapeDtypeStruct((t, LANES), jnp.int32), jax.ShapeDtypeStruct((t, LANES), F32)],
        compiler_params=_cparams(("parallel",)),
        name="out_proj_router",
    )(x2, y_gdn, y_ml, w_out_bf, w_out_bf, g_ffn.reshape(1, d), w_router_pad, b_router_pad)


FFN_ROWS = 512
INV_UNROLL = 8


DEINT_GROUP = 2 * LANES
MXU_TILE = 512


def _invert_slice(step, lo_ref, hi_ref, tail_ref, pos_ref, inv_ref):
    def fill8(t, c):
        for k in range(INV_UNROLL):
            inv_ref[t * INV_UNROLL + k] = -1
        return c

    def fill1(r, c):
        inv_ref[r] = -1
        return c

    def per_expert(e, c):
        lax.fori_loop(lo_ref[e], hi_ref[e], fill1, 0)
        return c

    @pl.when(step == 0)
    def _():
        lax.fori_loop(0, FFN_ROWS // INV_UNROLL, fill8, 0)
        lax.fori_loop(tail_ref[0], tail_ref[1], fill8, 0)
        lax.fori_loop(0, lo_ref.shape[0], per_expert, 0)

    chunk = pos_ref.shape[0]

    def body(t, c):
        for k in range(INV_UNROLL):
            a = t * INV_UNROLL + k
            inv_ref[FFN_ROWS + pos_ref[a]] = step * chunk + a
        return c

    lax.fori_loop(0, chunk // INV_UNROLL, body, 0)


def _prep_kernel(lo_ref, hi_ref, tail_ref, pos_ref, wgu_ref, wg_o, wu_o, inv_ref):
    _invert_slice(pl.program_id(0) * pl.num_programs(1) + pl.program_id(1), lo_ref, hi_ref, tail_ref, pos_ref,
                  inv_ref)
    x = wgu_ref[0].astype(BF16)
    r = lax.broadcasted_iota(jnp.int32, (DEINT_GROUP, LANES), 0)
    c = lax.broadcasted_iota(jnp.int32, (DEINT_GROUP, LANES), 1)
    sel_even = (r == 2 * c).astype(BF16)
    sel_odd = (r == 2 * c + 1).astype(BF16)
    for k in range(x.shape[1] // DEINT_GROUP):
        blk = x[:, DEINT_GROUP * k:DEINT_GROUP * (k + 1)]
        wg_o[0, :, LANES * k:LANES * (k + 1)] = jnp.dot(blk, sel_even, preferred_element_type=F32).astype(BF16)
        wu_o[0, :, LANES * k:LANES * (k + 1)] = jnp.dot(blk, sel_odd, preferred_element_type=F32).astype(BF16)


def _prep_weights_and_slots(w_gu, pos_flat, pad_lo, pad_hi, rows_used, n_rows, tn=512):
    n_e, d, f2 = w_gu.shape
    f = f2 // 2
    nj = f // tn
    n_inv = FFN_ROWS + n_rows
    chunk = pos_flat.shape[0] // (n_e * nj)
    assert chunk * n_e * nj == pos_flat.shape[0] and chunk % INV_UNROLL == 0
    tail = jnp.stack([(FFN_ROWS + rows_used) // INV_UNROLL, jnp.int32(n_inv // INV_UNROLL)]).astype(jnp.int32)
    smem = pl.BlockSpec(memory_space=pltpu.SMEM)
    return pl.pallas_call(
        _prep_kernel,
        grid=(n_e, nj),
        in_specs=[
            smem, smem, smem,
            pl.BlockSpec((chunk,), lambda e, j: (e * nj + j,), memory_space=pltpu.SMEM),
            pl.BlockSpec((1, d, 2 * tn), lambda e, j: (e, 0, j)),
        ],
        out_specs=[
            pl.BlockSpec((1, d, tn), lambda e, j: (e, 0, j)),
            pl.BlockSpec((1, d, tn), lambda e, j: (e, 0, j)),
            smem,
        ],
        out_shape=[jax.ShapeDtypeStruct((n_e, d, f), BF16), jax.ShapeDtypeStruct((n_e, d, f), BF16),
                   jax.ShapeDtypeStruct((n_inv,), jnp.int32)],
        compiler_params=_cparams(("arbitrary", "arbitrary")),
        name="expert_weight_prep",
    )(FFN_ROWS + pad_lo, FFN_ROWS + pad_hi, tail, pos_flat, w_gu)


def _ffn_kernel(be_ref, rows_ref, nv_ref, inv_ref, u_hbm, wg_ref, wu_ref, wd_ref, bg_ref, bu_ref, bd_ref,
                y_hbm, xg_buf, xb_scr, acc_scr, o_buf, sem):
    i = pl.program_id(0)
    j = pl.program_id(1)
    nv = nv_ref[0]
    tm, d = xb_scr.shape
    half = d // 2
    trash0 = y_hbm.shape[0] // PACK_SUB - tm

    tf = wg_ref.shape[2]
    n_chunks = tf // MXU_TILE + d // MXU_TILE
    bounds = [tm * g // n_chunks for g in range(n_chunks + 1)]

    def gather_issue(blk, g=None):
        base = (blk + 1) * tm
        for r in range(tm) if g is None else range(bounds[g], bounds[g + 1]):
            tok = jnp.maximum(inv_ref[base + r], 0) >> 2
            pltpu.make_async_copy(_row_tile(u_hbm, tok), _row_tile(xg_buf, r), sem.at[0]).start()

    def gather_wait():
        pltpu.make_async_copy(_row_tile(u_hbm, 0, tm), xg_buf, sem.at[0]).wait()

    def scatter_issue(blk, g=None):
        base = (blk + 1) * tm
        for r in range(tm) if g is None else range(bounds[g], bounds[g + 1]):
            a = inv_ref[base + r]
            pltpu.make_async_copy(_row_tile(o_buf, r), _row_tile(y_hbm, jnp.where(a < 0, trash0 + r, a)),
                                  sem.at[1]).start(priority=r % 2)

    def scatter_wait():
        pltpu.make_async_copy(o_buf, _row_tile(y_hbm, 0, tm), sem.at[1]).wait()

    @pl.when(i < nv)
    def _():
        par = i % 2
        full = rows_ref[i] > tm // 2

        def tile(n, first, between):
            xb = xb_scr[0:n, :]
            g = 0
            acts = []
            for k in range(tf // MXU_TILE):
                cs = slice(MXU_TILE * k, MXU_TILE * (k + 1))
                gate = jnp.minimum(jnp.dot(xb, wg_ref[0, :, cs], preferred_element_type=F32) + bg_ref[0, :, cs],
                                   SWIGLU_LIMIT)
                up = jnp.clip(jnp.dot(xb, wu_ref[0, :, cs], preferred_element_type=F32) + bu_ref[0, :, cs],
                              -SWIGLU_LIMIT, SWIGLU_LIMIT)
                acts.append((gate * _sigmoid(gate * SWIGLU_ALPHA) * (up + 1.0)).astype(BF16))
                between(g)
                g += 1
            act = jnp.concatenate(acts, axis=1)
            for k in range(d // MXU_TILE):
                cs = slice(MXU_TILE * k, MXU_TILE * (k + 1))
                y = jnp.dot(act, wd_ref[0, :, cs].astype(BF16), preferred_element_type=F32)
                if first:
                    acc_scr[par, 0:n, cs] = y + bd_ref[0, :, cs]
                    if n < tm:
                        acc_scr[par, n:tm, cs] = jnp.zeros((tm - n, MXU_TILE), F32)
                else:
                    acc_scr[par, 0:n, cs] += y
                between(g)
                g += 1

        def either_size(fn):
            @pl.when(full)
            def _():
                fn(tm)

            @pl.when(jnp.logical_not(full))
            def _():
                fn(tm // 2)

        @pl.when(j == 0)
        def _():
            @pl.when(i == 0)
            def _():
                acc_scr[1] = jnp.zeros((tm, d), F32)
                gather_issue(0)

            @pl.when(i > 0)
            def _():
                scatter_wait()

            gather_wait()
            for c in range(PACK_SUB):
                lo, hi = _unpack_chunk(xg_buf[pl.ds(c, tm, stride=PACK_SUB), :])
                xb_scr[:, LANES * c:LANES * (c + 1)] = lo.astype(BF16)
                xb_scr[:, half + LANES * c:half + LANES * (c + 1)] = hi.astype(BF16)
            _pack_rows(acc_scr[1 - par], o_buf)
            nxt = jnp.minimum(i + 1, nv - 1)
            either_size(lambda n: tile(n, True, lambda g: gather_issue(nxt, g)))

        @pl.when(j == 1)
        def _():
            either_size(lambda n: tile(n, False, lambda g: scatter_issue(i - 1, g)))

            @pl.when(i == nv - 1)
            def _():
                scatter_wait()
                _pack_rows(acc_scr[par], o_buf)
                scatter_issue(i)
                scatter_wait()
                gather_wait()


def _ffn(u_packed, inv, block_e, block_rows, n_valid, w_g, w_u, w_d, b_g, b_u, b_d, tm=FFN_ROWS):
    n_e, f, d = w_d.shape
    nf = 2
    tf = f // nf
    n_blocks = inv.shape[0] // tm - 1
    n_asg = (u_packed.shape[0] // PACK_SUB) * TOP_K
    ic = lambda i, nv: jnp.minimum(i, nv[0] - 1)
    jc = lambda i, j, nv: jnp.where(i < nv[0], j, nf - 1)
    col = lambda i, j, be, br, nv, iv: (be[ic(i, nv)], 0, jc(i, j, nv))
    return pl.pallas_call(
        _ffn_kernel,
        grid_spec=pltpu.PrefetchScalarGridSpec(
            num_scalar_prefetch=4,
            grid=(n_blocks, nf),
            in_specs=[
                pl.BlockSpec(memory_space=pl.ANY),
                pl.BlockSpec((1, d, tf), col),
                pl.BlockSpec((1, d, tf), col),
                pl.BlockSpec((1, tf, d), lambda i, j, be, br, nv, iv: (be[ic(i, nv)], jc(i, j, nv), 0)),
                pl.BlockSpec((1, 1, tf), col),
                pl.BlockSpec((1, 1, tf), col),
                pl.BlockSpec((1, 1, d), lambda i, j, be, br, nv, iv: (be[ic(i, nv)], 0, 0)),
            ],
            out_specs=pl.BlockSpec(memory_space=pl.ANY),
            scratch_shapes=[
                pltpu.VMEM((tm * PACK_SUB, LANES), jnp.uint32),
                pltpu.VMEM((tm, d), BF16),
                pltpu.VMEM((2, tm, d), F32),
                pltpu.VMEM((tm * PACK_SUB, LANES), jnp.uint32),
                pltpu.SemaphoreType.DMA((2,)),
            ],
        ),
        out_shape=jax.ShapeDtypeStruct(((n_asg + tm) * PACK_SUB, LANES), jnp.uint32),
        compiler_params=_cparams(("arbitrary", "arbitrary")),
        name="expert_ffn",
    )(block_e, block_rows, n_valid, inv, u_packed, w_g, w_u, w_d, b_g.reshape(n_e, 1, f),
      b_u.reshape(n_e, 1, f), b_d.reshape(n_e, 1, d))


COMBINE_ROWS = 256


def _combine_kernel(y_ref, tg_ref, h1_ref, p_ref, wpg_ref, wpp_ref, gp_ref, gpp_ref, gfin_ref, o_ref):
    tc = tg_ref.shape[0]
    tg = tg_ref[...]
    gks = [jnp.broadcast_to(tg[:, kk:kk + 1], (tc, LANES)) for kk in range(TOP_K)]
    lo_parts, hi_parts = [], []
    for c in range(PACK_SUB):
        lo_sum = hi_sum = None
        for kk in range(TOP_K):
            lo, hi = _unpack_chunk(y_ref[pl.ds(PACK_SUB * kk + c, tc, stride=PACK_SUB * TOP_K), :])
            gk = gks[kk]
            lo_sum = gk * lo if lo_sum is None else lo_sum + gk * lo
            hi_sum = gk * hi if hi_sum is None else hi_sum + gk * hi
        lo_parts.append(lo_sum)
        hi_parts.append(hi_sum)
    moe = jnp.concatenate(lo_parts + hi_parts, axis=1)
    h2 = h1_ref[...] + moe
    gate = _sigmoid(_bdot(_rms(h2, gp_ref[...]), wpg_ref[...]))
    pe = _rms(_bdot(p_ref[...], wpp_ref[...]), gpp_ref[...])
    h3 = h2 + gate * pe
    o_ref[...] = _rms(h3, gfin_ref[...])


def _combine(y_tok, tg, h1, p2, w_pg, w_pp, g_ple, g_ple_post, g_final, tc=COMBINE_ROWS):
    t, d = h1.shape
    pd = p2.shape[1]
    row = lambda w: pl.BlockSpec((tc, w), lambda i: (i, 0))
    const = lambda shp: pl.BlockSpec(shp, lambda i: (0, 0))
    return pl.pallas_call(
        _combine_kernel,
        grid=(t // tc,),
        in_specs=[
            pl.BlockSpec((tc * TOP_K * PACK_SUB, LANES), lambda i: (i, 0)),
            row(LANES), row(d), row(pd),
            const((d, d)), const((pd, d)), const((1, d)), const((1, d)), const((1, d)),
        ],
        out_specs=row(d),
        out_shape=jax.ShapeDtypeStruct((t, d), F32),
        compiler_params=_cparams(("parallel",)),
        name="combine_ple",
    )(y_tok, tg, h1, p2, w_pg, w_pp, g_ple.reshape(1, d), g_ple_post.reshape(1, d), g_final.reshape(1, d))


def _routing_tables(top_i, n_tok, tm=FFN_ROWS):
    n_asg = n_tok * TOP_K
    e_flat = top_i.reshape(n_asg)
    onehot = (e_flat[:, None] == jnp.arange(N_EXPERTS, dtype=jnp.int32)[None, :]).astype(jnp.int32)
    grp = onehot.reshape(n_asg // LANES, LANES, N_EXPERTS).astype(F32)
    within = jnp.einsum("ij,gjk->gik", jnp.tril(jnp.ones((LANES, LANES), F32)), grp).astype(jnp.int32)
    tot = within[:, -1, :]
    csum = (within + (jnp.cumsum(tot, axis=0) - tot)[:, None, :]).reshape(n_asg, N_EXPERTS)
    counts = csum[-1]
    padded = (counts + tm - 1) // tm * tm
    pend = jnp.cumsum(padded)
    pstart = pend - padded
    pos = jnp.sum(onehot * (csum - 1 + pstart[None, :]), axis=1).astype(jnp.int32)
    n_blocks = (n_tok * TOP_K) // tm + N_EXPERTS
    block_start = jnp.arange(n_blocks, dtype=jnp.int32) * tm
    block_e = jnp.minimum(jnp.sum(pend[None, :] <= block_start[:, None], axis=1), N_EXPERTS - 1).astype(jnp.int32)
    block_rows = jnp.clip((pstart + counts)[block_e] - block_start, 0, tm).astype(jnp.int32)
    n_valid = (pend[-1] // tm).astype(jnp.int32).reshape(1)
    pad_lo = (pstart + counts).astype(jnp.int32)
    return pos, pad_lo, pend.astype(jnp.int32), block_e, block_rows, n_valid, n_blocks * tm


def kernel(x, p, g_mix, w_in, conv_w, a_log, dt_bias, gdn_norm, ml_b_i, ml_b_f, ml_norm, w_out, g_ffn, w_router, b_router, w_gu, b_gu, w_down, b_down, g_ple, w_ple_gate, w_ple_proj, g_ple_post, g_final):
    b, s, d = x.shape
    n_tok = b * s
    assert w_in.shape[0] == 1, "single-layer block: the final norm is fused into the layer's last kernel"
    l = 0
    x2 = x.reshape(n_tok, d)
    w_main, w_gate = _prep_in_weights(w_in[l])
    z_main, z_gate = _in_proj(x2, g_mix[l], w_main, w_gate)
    z_main = z_main.reshape(b, s, Z_MAIN)
    z_gate = z_gate.reshape(b, s, LANES)
    y_gdn = _gdn(z_main, z_gate, conv_w[l], a_log[l], dt_bias[l], gdn_norm[l])
    y_ml = _mlstm(z_main, z_gate, ml_b_i[l], ml_b_f[l], ml_norm[l])

    w_r = jnp.pad(w_router[l], ((0, 0), (0, LANES - N_EXPERTS)))
    b_r = jnp.pad(b_router[l], (0, LANES - N_EXPERTS)).reshape(1, LANES)
    h1, u2, top_i, top_g = _out_proj(x2, y_gdn.reshape(n_tok, GDN_V), y_ml.reshape(n_tok, ML_V),
                                     w_out[l].astype(BF16), g_ffn[l], w_r, b_r)

    pos, pad_lo, pend, block_e, block_rows, n_valid, n_rows = _routing_tables(top_i[:, :TOP_K], n_tok)
    w_g, w_u, inv = _prep_weights_and_slots(w_gu[l], pos, pad_lo, pend, pend[-1], n_rows)
    y_tok = _ffn(u2, inv, block_e, block_rows, n_valid, w_g, w_u, w_down[l],
                 b_gu[l][:, 0::2], b_gu[l][:, 1::2], b_down[l])
    out = _combine(y_tok, top_g, h1, p[l].reshape(n_tok, -1), w_ple_gate[l].astype(BF16),
                   w_ple_proj[l].astype(BF16), g_ple[l], g_ple_post[l], g_final)
    return out.reshape(b, s, d)
```

```python
import jax
import jax.numpy as jnp
from jax import lax
from jax.experimental import pallas as pl
from jax.experimental.pallas import tpu as pltpu

F32 = jnp.float32
BF16 = jnp.bfloat16

EPS = 1e-6
CHUNK = 64
GDN_HEADS = 8
GDN_DK = 128
GDN_DV = 128
GDN_CONV = 4
ML_HEADS = 4
ML_DQK = 128
ML_DV = 256
GATE_SOFTCAP = 15.0
N_EXPERTS = 32
TOP_K = 4
SWIGLU_LIMIT = 7.0
SWIGLU_ALPHA = 1.702

LANES = 128
HALO = 8
NEG_BIG = -1e30

GDN_QK = GDN_HEADS * GDN_DK
GDN_V = GDN_HEADS * GDN_DV
ML_QK = ML_HEADS * ML_DQK
ML_V = ML_HEADS * ML_DV
OFF_Q = 0
OFF_K = OFF_Q + GDN_QK
OFF_V = OFF_K + GDN_QK
OFF_GZ = OFF_V + GDN_V
OFF_MQ = OFF_GZ + GDN_V
OFF_MK = OFF_MQ + ML_QK
OFF_MV = OFF_MK + ML_QK
OFF_MO = OFF_MV + ML_V
Z_MAIN = OFF_MO + ML_V
GATE_A = 0
GATE_B = GATE_A + GDN_HEADS
GATE_I = GATE_B + GDN_HEADS
GATE_F = GATE_I + ML_HEADS

GDN_TIME_BLOCK = 128
ML_TIME_BLOCK = 256
GDN_HEADS_PER_STEP = 8
ML_HEADS_PER_STEP = 4
VMEM_LIMIT = 56 * 1024 * 1024


def _cparams(sem):
    return pltpu.CompilerParams(dimension_semantics=sem, vmem_limit_bytes=VMEM_LIMIT)


def _bdot(a, b):
    return jnp.dot(a.astype(BF16), b.astype(BF16), preferred_element_type=F32)


def _bdot_nt(a, b):
    return lax.dot_general(a.astype(BF16), b.astype(BF16), (((1,), (1,)), ((), ())),
                           preferred_element_type=F32)


def _bdot_tn(a, b):
    return lax.dot_general(a.astype(BF16), b.astype(BF16), (((0,), (0,)), ((), ())),
                           preferred_element_type=F32)


def _split3(x):
    hi = x.astype(BF16)
    r1 = x - hi.astype(F32)
    mid = r1.astype(BF16)
    lo = (r1 - mid.astype(F32)).astype(BF16)
    return hi, mid, lo


def _dot_sel(sel_bf16, x):
    hi, mid, lo = _split3(x)
    d = lambda t: jnp.dot(sel_bf16, t, preferred_element_type=F32)
    return d(hi) + (d(mid) + d(lo))


def _sigmoid(x):
    return 1.0 / (1.0 + jnp.exp(-x))


def _softplus(x):
    return jnp.maximum(x, 0.0) + jnp.log1p(jnp.exp(-jnp.abs(x)))


def _rms(x, g):
    return x * lax.rsqrt(jnp.mean(x * x, axis=-1, keepdims=True) + EPS) * g


def _lane_pick(zg, idx):
    lane = lax.broadcasted_iota(jnp.int32, zg.shape, 1)
    return jnp.sum(jnp.where(lane == idx, zg, 0.0), axis=1, keepdims=True)


def _on_lanes(v, start):
    return jnp.pad(v.astype(F32), (start, LANES - start - v.shape[0])).reshape(1, LANES)


def _chunk_masks(n):
    r = lax.broadcasted_iota(jnp.int32, (n, n), 0)
    c = lax.broadcasted_iota(jnp.int32, (n, n), 1)
    same = lambda s: (r >> s) == (c >> s)
    return r, c, same


def _chunk_last(gc):
    n = gc.shape[0] // CHUNK
    parts = [jnp.broadcast_to(gc[CHUNK * (i + 1) - 1:CHUNK * (i + 1), :], (CHUNK, gc.shape[1]))
             for i in range(n)]
    return jnp.concatenate(parts, axis=0)


def _inproj_kernel(x_ref, g_ref, w_ref, wg_ref, z_ref, zg_ref, u_scr):
    @pl.when(pl.program_id(1) == 0)
    def _():
        u = _rms(x_ref[...], g_ref[...]).astype(BF16)
        u_scr[...] = u
        zg_ref[...] = jnp.dot(u, wg_ref[...], preferred_element_type=F32)

    z_ref[...] = jnp.dot(u_scr[...], w_ref[...], preferred_element_type=F32).astype(z_ref.dtype)


def _in_proj(x2, g_mix, w_main, w_gate, tm=1024, tn=Z_MAIN // 4):
    t, d = x2.shape
    n = w_main.shape[1]
    return pl.pallas_call(
        _inproj_kernel,
        grid=(t // tm, n // tn),
        in_specs=[
            pl.BlockSpec((tm, d), lambda i, j: (i, 0)),
            pl.BlockSpec((1, d), lambda i, j: (0, 0)),
            pl.BlockSpec((d, tn), lambda i, j: (0, j)),
            pl.BlockSpec((d, LANES), lambda i, j: (0, 0)),
        ],
        out_specs=[
            pl.BlockSpec((tm, tn), lambda i, j: (i, j)),
            pl.BlockSpec((tm, LANES), lambda i, j: (i, 0)),
        ],
        out_shape=[jax.ShapeDtypeStruct((t, n), BF16), jax.ShapeDtypeStruct((t, LANES), F32)],
        scratch_shapes=[pltpu.VMEM((tm, d), BF16)],
        compiler_params=_cparams(("parallel", "arbitrary")),
        name="in_proj",
    )(x2, g_mix.reshape(1, d), w_main, w_gate)


def _prep_in_weights(w_in):
    splits = (GDN_QK, GDN_QK, GDN_V, GDN_HEADS, GDN_HEADS, GDN_V, ML_QK, ML_QK, ML_V, ML_HEADS, ML_HEADS, ML_V)
    offs = [0]
    for wd in splits:
        offs.append(offs[-1] + wd)
    part = lambda i: w_in[:, offs[i]:offs[i + 1]]
    w_main = jnp.concatenate([part(i) for i in (0, 1, 2, 5, 6, 7, 8, 11)], axis=1).astype(BF16)
    gates = jnp.concatenate([part(i) for i in (3, 4, 9, 10)], axis=1)
    w_gate = jnp.pad(gates, ((0, 0), (0, LANES - gates.shape[1]))).astype(BF16)
    return w_main, w_gate


def _tri_inverse_minus_eye(ms, same):
    m16 = same(4)
    m32 = same(5)
    n32_mask = jnp.logical_and(m32, jnp.logical_not(m16))
    a = [jnp.where(m16, -m, 0.0) for m in ms]
    acc = list(a)
    for _ in range(3):
        a = [_bdot(x, x) for x in a]
        acc = [p + x + _bdot(p, x) for p, x in zip(acc, a)]
    for level in range(2):
        ns = [jnp.where(n32_mask, m, 0.0) if level == 0 else jnp.where(m32, 0.0, m) for m in ms]
        ys = [n + _bdot(p, n) for p, n in zip(acc, ns)]
        acc = [p - (y + _bdot(y, p)) for p, y in zip(acc, ys)]
    return acc


def _gdn_kernel(alog_ref, dtb_ref, zq_ref, zk_ref, zv_ref, gz_ref, zg_ref, cwq_ref, cwk_ref, cwv_ref,
                gn_ref, o_ref, s_scr, xq_scr, xk_scr, xv_scr):
    hb = s_scr.shape[0]
    h0 = pl.program_id(1) * hb
    tb = zq_ref.shape[1]
    nchunk = tb // CHUNK
    heads = range(hb)
    lanes = lambda hh: slice(LANES * hh, LANES * (hh + 1))

    @pl.when(pl.program_id(2) == 0)
    def _():
        s_scr[...] = jnp.zeros_like(s_scr)
        for scr in (xq_scr, xk_scr, xv_scr):
            scr[0:HALO, :] = jnp.zeros((HALO, scr.shape[1]), F32)

    def conv_silu(z_ref, x_scr, cw_ref):
        x = z_ref[0].astype(F32)
        x_scr[HALO:HALO + tb, :] = x
        w = cw_ref[...]
        acc = x * w[GDN_CONV - 1:GDN_CONV, :]
        for s in range(1, GDN_CONV):
            acc = acc + x_scr[pl.ds(HALO - s, tb), :] * w[GDN_CONV - 1 - s:GDN_CONV - s, :]
        x_scr[0:HALO, :] = x[tb - HALO:tb, :]
        return acc * _sigmoid(acc)

    q_all = conv_silu(zq_ref, xq_scr, cwq_ref)
    k_all = conv_silu(zk_ref, xk_scr, cwk_ref)
    v_all = conv_silu(zv_ref, xv_scr, cwv_ref)
    gz_all = gz_ref[0].astype(F32)
    zg = zg_ref[0]
    g_all = -jnp.exp(alog_ref[...]) * _softplus(zg + dtb_ref[...])
    beta_all = _sigmoid(zg)

    r, c, same = _chunk_masks(tb)
    in_chunk = same(6)
    tril = jnp.logical_and(in_chunk, r >= c)
    strict = jnp.logical_and(in_chunk, r > c)
    tril_bf = tril.astype(BF16)
    gc_all = _dot_sel(tril_bf, g_all)

    def head_prep(hh):
        q = q_all[:, lanes(hh)]
        k = k_all[:, lanes(hh)]
        v = v_all[:, lanes(hh)]
        q = q * lax.rsqrt(jnp.sum(q * q, axis=-1, keepdims=True) + EPS) * (GDN_DK ** -0.5)
        k = k * lax.rsqrt(jnp.sum(k * k, axis=-1, keepdims=True) + EPS)
        beta = _lane_pick(beta_all, GATE_B + h0 + hh)
        gcb = jnp.broadcast_to(_lane_pick(gc_all, GATE_A + h0 + hh), (tb, LANES))
        gc_row = gcb.T[0:1, :]
        gc_col = jnp.concatenate([gcb] * (tb // LANES), axis=1)
        decay = jnp.exp(jnp.where(tril, gc_col - gc_row, NEG_BIG))
        eg = jnp.exp(gcb)
        kb = k * beta
        m_low = jnp.where(strict, _bdot_nt(kb, k) * decay, 0.0)
        rhs = jnp.concatenate([v * beta, kb * eg], axis=1)
        attn = _bdot_nt(q, k) * decay
        kd = k * jnp.exp(_chunk_last(gcb) - gcb)
        return dict(m_low=m_low, rhs=rhs, attn=attn, qd=q * eg, kd=kd, gcb=gcb)

    hp = [head_prep(hh) for hh in heads]
    t_m1 = _tri_inverse_minus_eye([p["m_low"] for p in hp], same)
    uws = [p["rhs"] + _bdot(t, p["rhs"]) for p, t in zip(hp, t_m1)]

    states = [s_scr[hh] for hh in heads]
    outs = [[] for _ in heads]
    for i in range(nchunk):
        sl = slice(CHUNK * i, CHUNK * (i + 1))
        for hh in heads:
            p, uw = hp[hh], uws[hh]
            res1 = _bdot(jnp.concatenate([uw[sl, GDN_DV:], p["qd"][sl]], axis=0), states[hh])
            v_new = uw[sl, :GDN_DV] - res1[:CHUNK]
            o_intra = _bdot(p["attn"][sl, CHUNK * i:CHUNK * (i + 1)], v_new)
            outs[hh].append(res1[CHUNK:] + o_intra)
            g_last = jnp.exp(p["gcb"][CHUNK * (i + 1) - 1:CHUNK * (i + 1), :])
            states[hh] = states[hh] * g_last[:, 0:1] + _bdot_tn(p["kd"][sl], v_new)

    for hh in heads:
        s_scr[hh] = states[hh]
        o = _rms(jnp.concatenate(outs[hh], axis=0), gn_ref[...])
        gz = gz_all[:, lanes(hh)]
        o_ref[0, :, lanes(hh)] = (o * (gz * _sigmoid(gz))).astype(o_ref.dtype)


def _gdn(z_main, z_gate, conv_w, a_log, dt_bias, gdn_norm, tb=GDN_TIME_BLOCK, hb=GDN_HEADS_PER_STEP):
    b, s, _ = z_main.shape
    wid = hb * LANES
    hq, hk, hv, hz = OFF_Q // wid, OFF_K // wid, OFF_V // wid, OFF_GZ // wid
    zspec = lambda off: pl.BlockSpec((1, tb, wid), lambda bi, hi, ti: (bi, ti, off + hi))
    cspec = lambda off: pl.BlockSpec((GDN_CONV, wid), lambda bi, hi, ti: (0, off + hi))
    lane_row = pl.BlockSpec((1, LANES), lambda bi, hi, ti: (0, 0))
    return pl.pallas_call(
        _gdn_kernel,
        grid=(b, GDN_HEADS // hb, s // tb),
        in_specs=[
            lane_row, lane_row,
            zspec(hq), zspec(hk), zspec(hv), zspec(hz),
            pl.BlockSpec((1, tb, LANES), lambda bi, hi, ti: (bi, ti, 0)),
            cspec(hq), cspec(hk), cspec(hv),
            pl.BlockSpec((1, GDN_DV), lambda bi, hi, ti: (0, 0)),
        ],
        out_specs=pl.BlockSpec((1, tb, wid), lambda bi, hi, ti: (bi, ti, hi)),
        out_shape=jax.ShapeDtypeStruct((b, s, GDN_V), BF16),
        scratch_shapes=[
            pltpu.VMEM((hb, GDN_DK, GDN_DV), F32),
            pltpu.VMEM((HALO + tb, wid), F32),
            pltpu.VMEM((HALO + tb, wid), F32),
            pltpu.VMEM((HALO + tb, wid), F32),
        ],
        compiler_params=_cparams(("parallel", "parallel", "arbitrary")),
        name="gdn",
    )(_on_lanes(a_log, GATE_A), _on_lanes(dt_bias, GATE_A), z_main, z_main, z_main, z_main, z_gate,
      conv_w, conv_w, conv_w, gdn_norm.reshape(1, GDN_DV))


def _mlstm_kernel(bias_ref, q_ref, k_ref, v_ref, og_ref, zg_ref, nrm_ref, o_ref, c_scr, m_scr):
    hb = c_scr.shape[0]
    h0 = pl.program_id(1) * hb
    tb = q_ref.shape[1]
    nchunk = tb // CHUNK
    cap = GATE_SOFTCAP

    @pl.when(pl.program_id(2) == 0)
    def _():
        c_scr[...] = jnp.zeros_like(c_scr)
        m_scr[...] = jnp.zeros_like(m_scr)

    capped = cap * jnp.tanh((zg_ref[0] + bias_ref[...]) / cap)
    logf_all = -_softplus(-capped)
    r, c, same = _chunk_masks(tb)
    tril = jnp.logical_and(same(6), r >= c)
    tril_bf = tril.astype(BF16)
    ones = jnp.ones((tb, LANES), F32)
    fc_all = _dot_sel(tril_bf, logf_all)

    for hh in range(hb):
        q = q_ref[0, :, ML_DQK * hh:ML_DQK * (hh + 1)].astype(F32)
        k = k_ref[0, :, ML_DQK * hh:ML_DQK * (hh + 1)].astype(F32) * (ML_DQK ** -0.5)
        v = v_ref[0, :, ML_DV * hh:ML_DV * (hh + 1)].astype(F32)
        i_pre = _lane_pick(capped, GATE_I + h0 + hh)
        fcb = jnp.broadcast_to(_lane_pick(fc_all, GATE_F + h0 + hh), (tb, LANES))
        fmi = fcb - i_pre
        fmi_row = fmi.T[0:1, :]
        fc_col = jnp.concatenate([fcb] * (tb // LANES), axis=1)
        d_mat = jnp.where(tril, fc_col - fmi_row, NEG_BIG)
        d_max = jnp.max(d_mat, axis=-1, keepdims=True)
        a_end = _chunk_last(fcb) - fmi

        m_st = m_scr[hh]
        m_rows, decs, wks = [], [], []
        for i in range(nchunk):
            sl = slice(CHUNK * i, CHUNK * (i + 1))
            f_last = fcb[CHUNK * (i + 1) - 1:CHUNK * (i + 1), :]
            a_max = jnp.max(a_end[sl], axis=0, keepdims=True)
            m_new = jnp.maximum(f_last + m_st, a_max)
            m_rows.append(jnp.broadcast_to(m_st, (CHUNK, LANES)))
            decs.append(jnp.exp(f_last + m_st - m_new))
            wks.append(jnp.exp(a_end[sl] - m_new))
            m_st = m_new
        m_scr[hh] = m_st

        inter = fcb + jnp.concatenate(m_rows, axis=0)
        m_i = jnp.maximum(d_max, inter)
        s_inter = jnp.exp(inter - m_i)
        wts = jnp.exp(d_mat - m_i[:, 0:1]) * _bdot_nt(q, k)
        v_aug = jnp.concatenate([v, ones], axis=1)
        intra = _bdot(wts, v_aug)

        cst = c_scr[hh]
        nums = []
        for i in range(nchunk):
            sl = slice(CHUNK * i, CHUNK * (i + 1))
            nums.append(s_inter[sl, 0:1] * _bdot(q[sl], cst) + intra[sl])
            cst = decs[i][:, 0:1] * cst + _bdot_tn(wks[i] * k[sl], v_aug[sl])
        c_scr[hh] = cst

        num_aug = jnp.concatenate(nums, axis=0)
        den = num_aug[:, ML_DV:ML_DV + 1]
        hout = num_aug[:, :ML_DV] / jnp.maximum(jnp.abs(den), jnp.exp(-m_i[:, 0:1]))
        hout = _rms(hout, nrm_ref[hh])
        og = og_ref[0, :, ML_DV * hh:ML_DV * (hh + 1)].astype(F32)
        o_ref[0, :, ML_DV * hh:ML_DV * (hh + 1)] = (hout * _sigmoid(og)).astype(o_ref.dtype)


def _mlstm(z_main, z_gate, ml_b_i, ml_b_f, ml_norm, tb=ML_TIME_BLOCK, hb=ML_HEADS_PER_STEP):
    b, s, _ = z_main.shape
    qw, vw = hb * ML_DQK, hb * ML_DV
    hq, hk = OFF_MQ // qw, OFF_MK // qw
    hv, ho = OFF_MV // vw, OFF_MO // vw
    qspec = lambda off: pl.BlockSpec((1, tb, qw), lambda bi, hi, ti: (bi, ti, off + hi))
    vspec = lambda off: pl.BlockSpec((1, tb, vw), lambda bi, hi, ti: (bi, ti, off + hi))
    gate_bias = _on_lanes(ml_b_i, GATE_I) + _on_lanes(ml_b_f, GATE_F)
    return pl.pallas_call(
        _mlstm_kernel,
        grid=(b, ML_HEADS // hb, s // tb),
        in_specs=[
            pl.BlockSpec((1, LANES), lambda bi, hi, ti: (0, 0)),
            qspec(hq), qspec(hk), vspec(hv), vspec(ho),
            pl.BlockSpec((1, tb, LANES), lambda bi, hi, ti: (bi, ti, 0)),
            pl.BlockSpec((hb, 1, ML_DV), lambda bi, hi, ti: (hi, 0, 0)),
        ],
        out_specs=pl.BlockSpec((1, tb, vw), lambda bi, hi, ti: (bi, ti, hi)),
        out_shape=jax.ShapeDtypeStruct((b, s, ML_V), BF16),
        scratch_shapes=[
            pltpu.VMEM((hb, ML_DQK, ML_DV + LANES), F32),
            pltpu.VMEM((hb, 1, LANES), F32),
        ],
        compiler_params=_cparams(("parallel", "parallel", "arbitrary")),
        name="mlstm",
    )(gate_bias, z_main, z_main, z_main, z_main, z_gate, ml_norm.reshape(ML_HEADS, 1, ML_DV))


PACK_SUB = 8


def _pack_rows(x, o_ref):
    n, d = x.shape
    half = d // 2
    for j in range(PACK_SUB):
        lo = x[:, LANES * j:LANES * (j + 1)].astype(BF16).astype(F32)
        hi = x[:, half + LANES * j:half + LANES * (j + 1)].astype(BF16).astype(F32)
        word = (lax.bitcast_convert_type(lo, jnp.uint32) >> 16) | lax.bitcast_convert_type(hi, jnp.uint32)
        o_ref[pl.ds(j, n, stride=PACK_SUB), :] = word


def _unpack_chunk(word):
    lo = lax.bitcast_convert_type(word << 16, F32)
    hi = lax.bitcast_convert_type(word & jnp.uint32(0xFFFF0000), F32)
    return lo, hi


def _row_tile(ref, row, n=1):
    start = row * PACK_SUB
    if not isinstance(start, int):
        start = pl.multiple_of(start, PACK_SUB)
    return ref.at[pl.ds(start, n * PACK_SUB), :]


def _outproj_kernel(x_ref, yg_ref, ym_ref, wo1_ref, wo2_ref, gf_ref, wr_ref, br_ref,
                    h_ref, u_ref, ti_ref, tg_ref):
    h1 = (x_ref[...] + jnp.dot(yg_ref[...], wo1_ref[...], preferred_element_type=F32)
          + jnp.dot(ym_ref[...], wo2_ref[...], preferred_element_type=F32))
    h_ref[...] = h1
    u = _rms(h1, gf_ref[...])
    _pack_rows(u, u_ref)

    u_hi = u.astype(BF16)
    u_lo = (u - u_hi.astype(F32)).astype(BF16)
    wr = wr_ref[...]
    w_hi = wr.astype(BF16)
    w_lo = (wr - w_hi.astype(F32)).astype(BF16)
    d = lambda a, b: jnp.dot(a, b, preferred_element_type=F32)
    logits = d(u_hi, w_hi) + (d(u_hi, w_lo) + d(u_lo, w_hi)) + br_ref[...]

    lane = lax.broadcasted_iota(jnp.int32, logits.shape, 1)
    lg = jnp.where(lane < N_EXPERTS, logits, NEG_BIG)
    vals, idxs = [], []
    for _ in range(TOP_K):
        m = jnp.max(lg, axis=1, keepdims=True)
        idx = jnp.min(jnp.where(lg == m, lane, LANES), axis=1, keepdims=True)
        vals.append(m)
        idxs.append(idx)
        lg = jnp.where(lane == idx, NEG_BIG, lg)
    es = [jnp.exp(vv - vals[0]) for vv in vals]
    tot = es[0] + es[1] + es[2] + es[3]
    ti = jnp.zeros(logits.shape, jnp.int32)
    tg = jnp.zeros(logits.shape, F32)
    for kk in range(TOP_K):
        ti = jnp.where(lane == kk, idxs[kk], ti)
        tg = jnp.where(lane == kk, es[kk] / tot, tg)
    ti_ref[...] = ti
    tg_ref[...] = tg


def _out_proj(x2, y_gdn, y_ml, w_out_bf, g_ffn, w_router_pad, b_router_pad, tm=512):
    t, d = x2.shape
    row = lambda w: pl.BlockSpec((tm, w), lambda i: (i, 0))
    const = lambda shp: pl.BlockSpec(shp, lambda i: (0, 0))
    return pl.pallas_call(
        _outproj_kernel,
        grid=(t // tm,),
        in_specs=[
            row(d), row(GDN_V), row(ML_V),
            pl.BlockSpec((GDN_V, d), lambda i: (0, 0)),
            pl.BlockSpec((ML_V, d), lambda i: (GDN_V // ML_V, 0)),
            const((1, d)), const((d, LANES)), const((1, LANES)),
        ],
        out_specs=[row(d), pl.BlockSpec((tm * PACK_SUB, LANES), lambda i: (i, 0)), row(LANES), row(LANES)],
        out_shape=[jax.ShapeDtypeStruct((t, d), F32), jax.ShapeDtypeStruct((t * PACK_SUB, LANES), jnp.uint32),
                   jax.ShapeDtypeStruct((t, LANES), jnp.int32), jax.ShapeDtypeStruct((t, LANES), F32)],
        compiler_params=_cparams(("parallel",)),
        name="out_proj_router",
    )(x2, y_gdn, y_ml, w_out_bf, w_out_bf, g_ffn.reshape(1, d), w_router_pad, b_router_pad)


FFN_ROWS = 512
INV_UNROLL = 8


DEINT_GROUP = 2 * LANES
MXU_TILE = 1024


def _invert_slice(step, lo_ref, hi_ref, tail_ref, pos_ref, inv_ref):
    def fill8(t, c):
        for k in range(INV_UNROLL):
            inv_ref[t * INV_UNROLL + k] = -1
        return c

    def fill1(r, c):
        inv_ref[r] = -1
        return c

    def per_expert(e, c):
        lax.fori_loop(lo_ref[e], hi_ref[e], fill1, 0)
        return c

    @pl.when(step == 0)
    def _():
        lax.fori_loop(0, FFN_ROWS // INV_UNROLL, fill8, 0)
        lax.fori_loop(tail_ref[0], tail_ref[1], fill8, 0)
        lax.fori_loop(0, lo_ref.shape[0], per_expert, 0)

    chunk = pos_ref.shape[0]

    def body(t, c):
        for k in range(INV_UNROLL):
            a = t * INV_UNROLL + k
            inv_ref[FFN_ROWS + pos_ref[a]] = step * chunk + a
        return c

    lax.fori_loop(0, chunk // INV_UNROLL, body, 0)


def _prep_kernel(lo_ref, hi_ref, tail_ref, pos_ref, wgu_ref, wg_o, wu_o, inv_ref):
    _invert_slice(pl.program_id(0) * pl.num_programs(1) + pl.program_id(1), lo_ref, hi_ref, tail_ref, pos_ref,
                  inv_ref)
    x = wgu_ref[0].astype(BF16)
    r = lax.broadcasted_iota(jnp.int32, (DEINT_GROUP, LANES), 0)
    c = lax.broadcasted_iota(jnp.int32, (DEINT_GROUP, LANES), 1)
    sel_even = (r == 2 * c).astype(BF16)
    sel_odd = (r == 2 * c + 1).astype(BF16)
    for k in range(x.shape[1] // DEINT_GROUP):
        blk = x[:, DEINT_GROUP * k:DEINT_GROUP * (k + 1)]
        wg_o[0, :, LANES * k:LANES * (k + 1)] = jnp.dot(blk, sel_even, preferred_element_type=F32).astype(BF16)
        wu_o[0, :, LANES * k:LANES * (k + 1)] = jnp.dot(blk, sel_odd, preferred_element_type=F32).astype(BF16)


def _prep_weights_and_slots(w_gu, pos_flat, pad_lo, pad_hi, rows_used, n_rows, tn=512):
    n_e, d, f2 = w_gu.shape
    f = f2 // 2
    nj = f // tn
    n_inv = FFN_ROWS + n_rows
    chunk = pos_flat.shape[0] // (n_e * nj)
    assert chunk * n_e * nj == pos_flat.shape[0] and chunk % INV_UNROLL == 0
    tail = jnp.stack([(FFN_ROWS + rows_used) // INV_UNROLL, jnp.int32(n_inv // INV_UNROLL)]).astype(jnp.int32)
    smem = pl.BlockSpec(memory_space=pltpu.SMEM)
    return pl.pallas_call(
        _prep_kernel,
        grid=(n_e, nj),
        in_specs=[
            smem, smem, smem,
            pl.BlockSpec((chunk,), lambda e, j: (e * nj + j,), memory_space=pltpu.SMEM),
            pl.BlockSpec((1, d, 2 * tn), lambda e, j: (e, 0, j)),
        ],
        out_specs=[
            pl.BlockSpec((1, d, tn), lambda e, j: (e, 0, j)),
            pl.BlockSpec((1, d, tn), lambda e, j: (e, 0, j)),
            smem,
        ],
        out_shape=[jax.ShapeDtypeStruct((n_e, d, f), BF16), jax.ShapeDtypeStruct((n_e, d, f), BF16),
                   jax.ShapeDtypeStruct((n_inv,), jnp.int32)],
        compiler_params=_cparams(("arbitrary", "arbitrary")),
        name="expert_weight_prep",
    )(FFN_ROWS + pad_lo, FFN_ROWS + pad_hi, tail, pos_flat, w_gu)


def _ffn_kernel(be_ref, rows_ref, nv_ref, inv_ref, u_hbm, wg_ref, wu_ref, wd_ref, bg_ref, bu_ref, bd_ref,
                y_hbm, xg_buf, xb_scr, acc_scr, o_buf, sem):
    i = pl.program_id(0)
    j = pl.program_id(1)
    nv = nv_ref[0]
    tm, d = xb_scr.shape
    half = d // 2
    trash0 = y_hbm.shape[0] // PACK_SUB - tm

    tf = wg_ref.shape[2]
    n_chunks = tf // MXU_TILE + d // MXU_TILE
    bounds = [tm * g // n_chunks for g in range(n_chunks + 1)]

    def gather_issue(blk, g=None):
        base = (blk + 1) * tm
        for r in range(tm) if g is None else range(bounds[g], bounds[g + 1]):
            tok = jnp.maximum(inv_ref[base + r], 0) >> 2
            pltpu.make_async_copy(_row_tile(u_hbm, tok), _row_tile(xg_buf, r), sem.at[0]).start()

    def gather_wait():
        pltpu.make_async_copy(_row_tile(u_hbm, 0, tm), xg_buf, sem.at[0]).wait()

    def scatter_issue(blk, g=None):
        base = (blk + 1) * tm
        for r in range(tm) if g is None else range(bounds[g], bounds[g + 1]):
            a = inv_ref[base + r]
            pltpu.make_async_copy(_row_tile(o_buf, r), _row_tile(y_hbm, jnp.where(a < 0, trash0 + r, a)),
                                  sem.at[1]).start(priority=r % 2)

    def scatter_wait():
        pltpu.make_async_copy(o_buf, _row_tile(y_hbm, 0, tm), sem.at[1]).wait()

    @pl.when(i < nv)
    def _():
        par = i % 2
        full = rows_ref[i] > tm // 2

        def tile(n, first, between):
            xb = xb_scr[0:n, :]
            g = 0
            acts = []
            for k in range(tf // MXU_TILE):
                cs = slice(MXU_TILE * k, MXU_TILE * (k + 1))
                gate = jnp.minimum(jnp.dot(xb, wg_ref[0, :, cs], preferred_element_type=F32) + bg_ref[0, :, cs],
                                   SWIGLU_LIMIT)
                up = jnp.clip(jnp.dot(xb, wu_ref[0, :, cs], preferred_element_type=F32) + bu_ref[0, :, cs],
                              -SWIGLU_LIMIT, SWIGLU_LIMIT)
                acts.append((gate * _sigmoid(gate * SWIGLU_ALPHA) * (up + 1.0)).astype(BF16))
                between(g)
                g += 1
            act = jnp.concatenate(acts, axis=1)
            for k in range(d // MXU_TILE):
                cs = slice(MXU_TILE * k, MXU_TILE * (k + 1))
                y = jnp.dot(act, wd_ref[0, :, cs].astype(BF16), preferred_element_type=F32)
                if first:
                    acc_scr[par, 0:n, cs] = y + bd_ref[0, :, cs]
                    if n < tm:
                        acc_scr[par, n:tm, cs] = jnp.zeros((tm - n, MXU_TILE), F32)
                else:
                    acc_scr[par, 0:n, cs] += y
                between(g)
                g += 1

        def either_size(fn):
            @pl.when(full)
            def _():
                fn(tm)

            @pl.when(jnp.logical_not(full))
            def _():
                fn(tm // 2)

        @pl.when(j == 0)
        def _():
            @pl.when(i == 0)
            def _():
                acc_scr[1] = jnp.zeros((tm, d), F32)
                gather_issue(0)

            @pl.when(i > 0)
            def _():
                scatter_wait()

            gather_wait()
            for c in range(PACK_SUB):
                lo, hi = _unpack_chunk(xg_buf[pl.ds(c, tm, stride=PACK_SUB), :])
                xb_scr[:, LANES * c:LANES * (c + 1)] = lo.astype(BF16)
                xb_scr[:, half + LANES * c:half + LANES * (c + 1)] = hi.astype(BF16)
            _pack_rows(acc_scr[1 - par], o_buf)
            nxt = jnp.minimum(i + 1, nv - 1)
            either_size(lambda n: tile(n, True, lambda g: gather_issue(nxt, g)))

        @pl.when(j == 1)
        def _():
            either_size(lambda n: tile(n, False, lambda g: scatter_issue(i - 1, g)))

            @pl.when(i == nv - 1)
            def _():
                scatter_wait()
                _pack_rows(acc_scr[par], o_buf)
                scatter_issue(i)
                scatter_wait()
                gather_wait()


def _ffn(u_packed, inv, block_e, block_rows, n_valid, w_g, w_u, w_d, b_g, b_u, b_d, tm=FFN_ROWS):
    n_e, f, d = w_d.shape
    nf = 2
    tf = f // nf
    n_blocks = inv.shape[0] // tm - 1
    n_asg = (u_packed.shape[0] // PACK_SUB) * TOP_K
    ic = lambda i, nv: jnp.minimum(i, nv[0] - 1)
    jc = lambda i, j, nv: jnp.where(i < nv[0], j, nf - 1)
    col = lambda i, j, be, br, nv, iv: (be[ic(i, nv)], 0, jc(i, j, nv))
    return pl.pallas_call(
        _ffn_kernel,
        grid_spec=pltpu.PrefetchScalarGridSpec(
            num_scalar_prefetch=4,
            grid=(n_blocks, nf),
            in_specs=[
                pl.BlockSpec(memory_space=pl.ANY),
                pl.BlockSpec((1, d, tf), col),
                pl.BlockSpec((1, d, tf), col),
                pl.BlockSpec((1, tf, d), lambda i, j, be, br, nv, iv: (be[ic(i, nv)], jc(i, j, nv), 0)),
                pl.BlockSpec((1, 1, tf), col),
                pl.BlockSpec((1, 1, tf), col),
                pl.BlockSpec((1, 1, d), lambda i, j, be, br, nv, iv: (be[ic(i, nv)], 0, 0)),
            ],
            out_specs=pl.BlockSpec(memory_space=pl.ANY),
            scratch_shapes=[
                pltpu.VMEM((tm * PACK_SUB, LANES), jnp.uint32),
                pltpu.VMEM((tm, d), BF16),
                pltpu.VMEM((2, tm, d), F32),
                pltpu.VMEM((tm * PACK_SUB, LANES), jnp.uint32),
                pltpu.SemaphoreType.DMA((2,)),
            ],
        ),
        out_shape=jax.ShapeDtypeStruct(((n_asg + tm) * PACK_SUB, LANES), jnp.uint32),
        compiler_params=_cparams(("arbitrary", "arbitrary")),
        name="expert_ffn",
    )(block_e, block_rows, n_valid, inv, u_packed, w_g, w_u, w_d, b_g.reshape(n_e, 1, f),
      b_u.reshape(n_e, 1, f), b_d.reshape(n_e, 1, d))


COMBINE_ROWS = 256


def _combine_kernel(y_ref, tg_ref, h1_ref, p_ref, wpg_ref, wpp_ref, gp_ref, gpp_ref, gfin_ref, o_ref):
    tc = tg_ref.shape[0]
    tg = tg_ref[...]
    gks = [jnp.broadcast_to(tg[:, kk:kk + 1], (tc, LANES)) for kk in range(TOP_K)]
    lo_parts, hi_parts = [], []
    for c in range(PACK_SUB):
        lo_sum = hi_sum = None
        for kk in range(TOP_K):
            lo, hi = _unpack_chunk(y_ref[pl.ds(PACK_SUB * kk + c, tc, stride=PACK_SUB * TOP_K), :])
            gk = gks[kk]
            lo_sum = gk * lo if lo_sum is None else lo_sum + gk * lo
            hi_sum = gk * hi if hi_sum is None else hi_sum + gk * hi
        lo_parts.append(lo_sum)
        hi_parts.append(hi_sum)
    moe = jnp.concatenate(lo_parts + hi_parts, axis=1)
    h2 = h1_ref[...] + moe
    gate = _sigmoid(_bdot(_rms(h2, gp_ref[...]), wpg_ref[...]))
    pe = _rms(_bdot(p_ref[...], wpp_ref[...]), gpp_ref[...])
    h3 = h2 + gate * pe
    o_ref[...] = _rms(h3, gfin_ref[...])


def _combine(y_tok, tg, h1, p2, w_pg, w_pp, g_ple, g_ple_post, g_final, tc=COMBINE_ROWS):
    t, d = h1.shape
    pd = p2.shape[1]
    row = lambda w: pl.BlockSpec((tc, w), lambda i: (i, 0))
    const = lambda shp: pl.BlockSpec(shp, lambda i: (0, 0))
    return pl.pallas_call(
        _combine_kernel,
        grid=(t // tc,),
        in_specs=[
            pl.BlockSpec((tc * TOP_K * PACK_SUB, LANES), lambda i: (i, 0)),
            row(LANES), row(d), row(pd),
            const((d, d)), const((pd, d)), const((1, d)), const((1, d)), const((1, d)),
        ],
        out_specs=row(d),
        out_shape=jax.ShapeDtypeStruct((t, d), F32),
        compiler_params=_cparams(("parallel",)),
        name="combine_ple",
    )(y_tok, tg, h1, p2, w_pg, w_pp, g_ple.reshape(1, d), g_ple_post.reshape(1, d), g_final.reshape(1, d))


def _routing_tables(top_i, n_tok, tm=FFN_ROWS):
    n_asg = n_tok * TOP_K
    e_flat = top_i.reshape(n_asg)
    onehot = (e_flat[:, None] == jnp.arange(N_EXPERTS, dtype=jnp.int32)[None, :]).astype(jnp.int32)
    grp = onehot.reshape(n_asg // LANES, LANES, N_EXPERTS).astype(F32)
    within = jnp.einsum("ij,gjk->gik", jnp.tril(jnp.ones((LANES, LANES), F32)), grp).astype(jnp.int32)
    tot = within[:, -1, :]
    csum = (within + (jnp.cumsum(tot, axis=0) - tot)[:, None, :]).reshape(n_asg, N_EXPERTS)
    counts = csum[-1]
    padded = (counts + tm - 1) // tm * tm
    pend = jnp.cumsum(padded)
    pstart = pend - padded
    pos = jnp.sum(onehot * (csum - 1 + pstart[None, :]), axis=1).astype(jnp.int32)
    n_blocks = (n_tok * TOP_K) // tm + N_EXPERTS
    block_start = jnp.arange(n_blocks, dtype=jnp.int32) * tm
    block_e = jnp.minimum(jnp.sum(pend[None, :] <= block_start[:, None], axis=1), N_EXPERTS - 1).astype(jnp.int32)
    block_rows = jnp.clip((pstart + counts)[block_e] - block_start, 0, tm).astype(jnp.int32)
    n_valid = (pend[-1] // tm).astype(jnp.int32).reshape(1)
    pad_lo = (pstart + counts).astype(jnp.int32)
    return pos, pad_lo, pend.astype(jnp.int32), block_e, block_rows, n_valid, n_blocks * tm


def kernel(x, p, g_mix, w_in, conv_w, a_log, dt_bias, gdn_norm, ml_b_i, ml_b_f, ml_norm, w_out, g_ffn, w_router, b_router, w_gu, b_gu, w_down, b_down, g_ple, w_ple_gate, w_ple_proj, g_ple_post, g_final):
    b, s, d = x.shape
    n_tok = b * s
    assert w_in.shape[0] == 1, "single-layer block: the final norm is fused into the layer's last kernel"
    l = 0
    x2 = x.reshape(n_tok, d)
    w_main, w_gate = _prep_in_weights(w_in[l])
    z_main, z_gate = _in_proj(x2, g_mix[l], w_main, w_gate)
    z_main = z_main.reshape(b, s, Z_MAIN)
    z_gate = z_gate.reshape(b, s, LANES)
    y_gdn = _gdn(z_main, z_gate, conv_w[l], a_log[l], dt_bias[l], gdn_norm[l])
    y_ml = _mlstm(z_main, z_gate, ml_b_i[l], ml_b_f[l], ml_norm[l])

    w_r = jnp.pad(w_router[l], ((0, 0), (0, LANES - N_EXPERTS)))
    b_r = jnp.pad(b_router[l], (0, LANES - N_EXPERTS)).reshape(1, LANES)
    h1, u2, top_i, top_g = _out_proj(x2, y_gdn.reshape(n_tok, GDN_V), y_ml.reshape(n_tok, ML_V),
                                     w_out[l].astype(BF16), g_ffn[l], w_r, b_r)

    pos, pad_lo, pend, block_e, block_rows, n_valid, n_rows = _routing_tables(top_i[:, :TOP_K], n_tok)
    w_g, w_u, inv = _prep_weights_and_slots(w_gu[l], pos, pad_lo, pend, pend[-1], n_rows)
    y_tok = _ffn(u2, inv, block_e, block_rows, n_valid, w_g, w_u, w_down[l],
                 b_gu[l][:, 0::2], b_gu[l][:, 1::2], b_down[l])
    out = _combine(y_tok, top_g, h1, p[l].reshape(n_tok, -1), w_ple_gate[l].astype(BF16),
                   w_ple_proj[l].astype(BF16), g_ple[l], g_ple_post[l], g_final)
    return out.reshape(b, s, d)
```
